```python
import math
import jax, jax.numpy as jnp
from jax import lax
import numpy as np

D_MODEL = 4096
BATCH = 4
SEQ = 2048
DEPTH = 2
DEC_BATCH = 8
DEC_SEQ = 8
PAST_LEN = 16384
PAGE_SIZE = 128

N_A_LAYERS = DEPTH // 2
N_B_LAYERS = DEPTH - N_A_LAYERS
D_FF = 11008
EPS = 1e-6
GDN_HEADS = 16
GDN_DK = 128
GDN_DV = 256
CONV_W = 4
GDN_CHUNK = 64
GDN_QK = GDN_HEADS * GDN_DK
GDN_V = GDN_HEADS * GDN_DV
GDN_CONV_CH = 2 * GDN_QK + GDN_V
GDN_IN = GDN_CONV_CH + GDN_V + 2 * GDN_HEADS
NSA_HEADS = 32
NSA_KV_HEADS = 4
HEAD_DIM = 128
GROUP = NSA_HEADS // NSA_KV_HEADS
BLOCK = 64
N_SELECT = 16
N_LOCAL = 2
WINDOW = 512
Q_CHUNK = 32
NSA_Q_IN = NSA_HEADS * HEAD_DIM + 3 * NSA_HEADS
NSA_KV_W = 3 * 2 * NSA_KV_HEADS * HEAD_DIM
SCALE = HEAD_DIM ** -0.5
NEG = -1e30
FORCE = 1e4

kernel_name = "yoco_gdn_nsa_macaron_step"


def rmsnorm(x, g):
    xf = x.astype(jnp.float32)
    y = xf * lax.rsqrt(jnp.mean(xf * xf, axis=-1, keepdims=True) + EPS)
    return (y * g.astype(jnp.float32)).astype(x.dtype)


def l2norm(x):
    xf = x.astype(jnp.float32)
    return xf * lax.rsqrt(jnp.sum(xf * xf, axis=-1, keepdims=True) + EPS)


def swiglu(x, w_gate, w_up, w_down):
    return (jax.nn.silu(x @ w_gate) * (x @ w_up)) @ w_down


def alibi_slopes_grouped():
    h = jnp.arange(1, NSA_HEADS + 1, dtype=jnp.float32)
    return (2.0 ** (-8.0 * h / NSA_HEADS)).reshape(NSA_KV_HEADS, GROUP)


def short_conv(u, buf, w):
    T = u.shape[1]
    full = jnp.concatenate([buf.astype(u.dtype), u], axis=1)
    out = full[:, 0:T] * w[0]
    for i in range(1, CONV_W):
        out = out + full[:, i:i + T] * w[i]
    return jax.nn.silu(out), full[:, T:]


def gated_delta_chunked(q, k, v, g, beta, s0):
    B, T, H, _ = q.shape
    C = min(GDN_CHUNK, T)
    n = -(-T // C)
    pad = n * C - T

    def prep(a):
        a = a.astype(jnp.float32)
        a = jnp.pad(a, [(0, 0), (0, pad)] + [(0, 0)] * (a.ndim - 2))
        return jnp.moveaxis(a.reshape((B, n, C) + a.shape[2:]), 1, 0)

    qc, kc, vc, gc, bc = (prep(a) for a in (q, k, v, g, beta))
    tri_incl = jnp.tril(jnp.ones((C, C), bool))
    tri_strict = jnp.tril(jnp.ones((C, C), bool), -1)
    eye = jnp.eye(C, dtype=jnp.float32)

    def step(S, inp):
        qi, ki, vi, gi, bi = inp
        gam = jnp.moveaxis(jnp.cumsum(gi, axis=1), 2, 1)
        diff = gam[..., :, None] - gam[..., None, :]
        decay = jnp.exp(jnp.where(tri_incl, diff, -jnp.inf))
        kk = jnp.einsum('bthd,bjhd->bhtj', ki, ki)
        qk = jnp.einsum('bthd,bjhd->bhtj', qi, ki)
        b_h = jnp.moveaxis(bi, 2, 1)
        a_mat = jnp.where(tri_strict, decay * kk, 0.0) * b_h[..., :, None]
        gt = jnp.exp(gam)
        ks0 = jnp.einsum('bthd,bhde->bhte', ki, S)
        rhs = b_h[..., None] * (jnp.moveaxis(vi, 2, 1) - gt[..., None] * ks0)
        u = lax.linalg.triangular_solve(eye + a_mat, rhs, left_side=True, lower=True, unit_diagonal=True)
        o = gt[..., None] * jnp.einsum('bthd,bhde->bhte', qi, S) + jnp.einsum('bhtj,bhje->bhte', decay * qk, u)
        wk = jnp.exp(gam[..., -1:] - gam)
        s_new = gt[..., -1][..., None, None] * S + jnp.einsum('bjhd,bhj,bhje->bhde', ki, wk, u)
        return s_new, jnp.moveaxis(o, 1, 2)

    s_fin, o = lax.scan(step, s0.astype(jnp.float32), (qc, kc, vc, gc, bc))
    o = jnp.moveaxis(o, 0, 1).reshape(B, n * C, H, -1)[:, :T]
    return o, s_fin


def gdn_mixer(xn, conv_buf, s0, w_in, conv_w, a_log, dt_bias, o_norm, w_out):
    B, T, _ = xn.shape
    proj = xn @ w_in
    u = proj[..., :GDN_CONV_CH]
    gate = proj[..., GDN_CONV_CH:GDN_CONV_CH + GDN_V].reshape(B, T, GDN_HEADS, GDN_DV)
    a = proj[..., GDN_CONV_CH + GDN_V:GDN_CONV_CH + GDN_V + GDN_HEADS]
    b = proj[..., GDN_CONV_CH + GDN_V + GDN_HEADS:]
    uc, new_buf = short_conv(u, conv_buf, conv_w)
    q = l2norm(uc[..., :GDN_QK].reshape(B, T, GDN_HEADS, GDN_DK)) * GDN_DK ** -0.5
    k = l2norm(uc[..., GDN_QK:2 * GDN_QK].reshape(B, T, GDN_HEADS, GDN_DK))
    v = uc[..., 2 * GDN_QK:].reshape(B, T, GDN_HEADS, GDN_DV)
    g = -jnp.exp(a_log.astype(jnp.float32)) * jax.nn.softplus(a.astype(jnp.float32) + dt_bias.astype(jnp.float32))
    beta = jax.nn.sigmoid(b.astype(jnp.float32))
    o, s_new = gated_delta_chunked(q, k, v, g, beta, s0)
    o = rmsnorm(o, o_norm) * jax.nn.silu(gate.astype(jnp.float32))
    return o.reshape(B, T, GDN_V).astype(xn.dtype) @ w_out, new_buf, s_new


def shared_kv(h, kv_norm, w_kv, k_norm):
    B, T, _ = h.shape
    kv = (rmsnorm(h, kv_norm) @ w_kv).reshape(B, T, 3, 2, NSA_KV_HEADS, HEAD_DIM)
    cmp_rows = kv[:, :, 0]
    sel_rows = jnp.stack([rmsnorm(kv[:, :, 1, 0], k_norm[1]), kv[:, :, 1, 1]], axis=2)
    win_rows = jnp.stack([rmsnorm(kv[:, :, 2, 0], k_norm[2]), kv[:, :, 2, 1]], axis=2)
    return (cmp_rows, sel_rows, win_rows)


def compress_blocks(rows, cmp_pos_w, w_cmp, kn_cmp):
    B, L = rows.shape[:2]
    nb = L // BLOCK
    blk = rows[:, :nb * BLOCK].reshape(B, nb, BLOCK, 2, NSA_KV_HEADS, HEAD_DIM)
    pooled = jnp.einsum('bnjcgd,jc->bncgd', blk, cmp_pos_w)
    comp = jnp.einsum('bncgd,cgde->bncge', pooled, w_cmp)
    return rmsnorm(comp[:, :, 0], kn_cmp), comp[:, :, 1]


def compressed_branch(qg, ck, cv, qpos, slopes):
    nc = ck.shape[1]
    s = jnp.einsum('btgrd,bngd->btgrn', qg, ck).astype(jnp.float32) * SCALE
    dist = (qpos[:, None] - ((jnp.arange(nc) + 1) * BLOCK - 1)[None, :])
    valid = dist >= 0
    s = s - slopes[None, None, :, :, None] * dist.astype(jnp.float32)[None, :, None, None, :]
    s = jnp.where(valid[None, :, None, None, :], s, NEG)
    p = jax.nn.softmax(s, axis=-1) * jnp.any(valid, axis=-1).astype(jnp.float32)[None, :, None, None, None]
    o = jnp.einsum('btgrn,bngd->btgrd', p.astype(cv.dtype), cv)
    return o, jnp.sum(p, axis=3)


def select_blocks(imp, qpos, n_blocks):
    imp = jnp.pad(imp, ((0, 0), (0, 0), (0, 0), (0, n_blocks - imp.shape[-1])))
    blk = jnp.arange(n_blocks)[None, :]
    cur = (qpos // BLOCK)[:, None]
    causal = blk <= cur
    forced = (blk == 0) | (causal & (blk > cur - N_LOCAL))
    score = jnp.where(forced[None, :, None, :], FORCE, jnp.where(causal[None, :, None, :], imp, -1.0))
    _, idx = lax.top_k(score, min(N_SELECT, n_blocks))
    return idx


def gathered_attention(qg, kg, vg, qpos, kpos, slopes):
    s = jnp.einsum('btgrd,btgkjd->btgrkj', qg, kg).astype(jnp.float32) * SCALE
    dist = (qpos[None, :, None, None, None] - kpos)
    s = s - slopes[None, None, :, :, None, None] * dist.astype(jnp.float32)[:, :, :, None]
    s = jnp.where((dist >= 0)[:, :, :, None], s, NEG)
    B, T, G, R, K, J = s.shape
    p = jax.nn.softmax(s.reshape(B, T, G, R, K * J), axis=-1).reshape(s.shape)
    return jnp.einsum('btgrkj,btgkjd->btgrd', p.astype(vg.dtype), vg)


def window_attention(qg, kw, vw, qpos, kpos, slopes):
    s = jnp.einsum('btgrd,bsgd->btgrs', qg, kw).astype(jnp.float32) * SCALE
    dist = qpos[:, None] - kpos[None, :]
    valid = (dist >= 0) & (dist < WINDOW) & (kpos >= 0)[None, :]
    s = s - slopes[None, None, :, :, None] * dist.astype(jnp.float32)[None, :, None, None, :]
    s = jnp.where(valid[None, :, None, None, :], s, NEG)
    p = jax.nn.softmax(s, axis=-1)
    return jnp.einsum('btgrs,bsgd->btgrd', p.astype(vw.dtype), vw)


def nsa_prompt_branches(qg, kv, cmp_pos_w, w_cmp, kn_cmp):
    cmp_rows, sel_rows, win_rows = kv
    B, T = qg.shape[:2]
    slopes = alibi_slopes_grouped()
    qpos = jnp.arange(T)
    ck, cv = compress_blocks(cmp_rows, cmp_pos_w, w_cmp, kn_cmp)
    o_cmp, imp = compressed_branch(qg, ck, cv, qpos, slopes)
    n_blocks = T // BLOCK
    sel_idx = select_blocks(imp, qpos, n_blocks)
    sel_blocks = sel_rows.reshape(B, n_blocks, BLOCK, 2, NSA_KV_HEADS, HEAD_DIM)
    win_pad = jnp.pad(win_rows, ((0, 0), (WINDOW, 0), (0, 0), (0, 0), (0, 0)))
    b_ix = jnp.arange(B)[:, None, None, None]
    g_ix = jnp.arange(NSA_KV_HEADS)[None, None, :, None]

    def chunk(c):
        t0 = c * Q_CHUNK
        qc = lax.dynamic_slice_in_dim(qg, t0, Q_CHUNK, axis=1)
        ic = lax.dynamic_slice_in_dim(sel_idx, t0, Q_CHUNK, axis=1)
        pos_c = t0 + jnp.arange(Q_CHUNK)
        kv_sel = sel_blocks[b_ix, ic, :, :, g_ix]
        kpos = ic[..., None] * BLOCK + jnp.arange(BLOCK)
        o_sel = gathered_attention(qc, kv_sel[..., 0, :], kv_sel[..., 1, :], pos_c, kpos, slopes)
        kv_win = lax.dynamic_slice_in_dim(win_pad, t0, WINDOW + Q_CHUNK, axis=1)
        wpos = t0 - WINDOW + jnp.arange(WINDOW + Q_CHUNK)
        o_win = window_attention(qc, kv_win[:, :, 0], kv_win[:, :, 1], pos_c, wpos, slopes)
        return o_sel, o_win

    o_sel, o_win = lax.map(chunk, jnp.arange(T // Q_CHUNK))
    o_sel = jnp.moveaxis(o_sel, 0, 1).reshape(qg.shape)
    o_win = jnp.moveaxis(o_win, 0, 1).reshape(qg.shape)
    return o_cmp, o_sel, o_win


def nsa_sample_branches(qg, kv, cache_cmp_kv, cache_sel_kv, cache_win_kv, page_table, cmp_pos_w, w_cmp, kn_cmp):
    new_cmp, new_sel, new_win = kv
    DB, S = qg.shape[:2]
    dt = new_cmp.dtype
    n_pages = page_table.shape[1]
    past = n_pages * PAGE_SIZE
    L = past + S
    slopes = alibi_slopes_grouped()
    qpos = past + jnp.arange(S)
    past_cmp = cache_cmp_kv[page_table].reshape(DB, past, 2, NSA_KV_HEADS, HEAD_DIM).astype(dt)
    ck, cv = compress_blocks(jnp.concatenate([past_cmp, new_cmp], axis=1), cmp_pos_w, w_cmp, kn_cmp)
    o_cmp, imp = compressed_branch(qg, ck, cv, qpos, slopes)
    n_blocks = -(-L // BLOCK)
    idx = select_blocks(imp, qpos, n_blocks)
    bpp = PAGE_SIZE // BLOCK
    n_past_blk = past // BLOCK
    n_new_blk = n_blocks - n_past_blk
    pool_blocks = cache_sel_kv.reshape(-1, BLOCK, 2, NSA_KV_HEADS, HEAD_DIM)
    new_blocks = jnp.pad(new_sel, ((0, 0), (0, n_new_blk * BLOCK - S), (0, 0), (0, 0), (0, 0))).reshape(
        DB, n_new_blk, BLOCK, 2, NSA_KV_HEADS, HEAD_DIM)
    b_ix = jnp.arange(DB)[:, None, None, None]
    g_ix = jnp.arange(NSA_KV_HEADS)[None, None, :, None]
    pidx = jnp.minimum(idx, n_past_blk - 1)
    phys = page_table[b_ix, pidx // bpp] * bpp + pidx % bpp
    kv_past = pool_blocks[phys, :, :, g_ix].astype(dt)
    nidx = jnp.clip(idx - n_past_blk, 0, n_new_blk - 1)
    kv_new = new_blocks[b_ix, nidx, :, :, g_ix]
    kv_sel = jnp.where((idx < n_past_blk)[..., None, None, None], kv_past, kv_new)
    kpos = idx[..., None] * BLOCK + jnp.arange(BLOCK)
    o_sel = gathered_attention(qg, kv_sel[..., 0, :], kv_sel[..., 1, :], qpos, kpos, slopes)
    win_all = jnp.concatenate([cache_win_kv.astype(dt), new_win], axis=1)
    wb = cache_win_kv.shape[1]
    wpos = past - wb + jnp.arange(wb + S)
    o_win = window_attention(qg, win_all[:, :, 0], win_all[:, :, 1], qpos, wpos, slopes)
    return o_cmp, o_sel, o_win


def nsa_mixer(xn, kv, attend, w_q, q_norm, w_o):
    B, T, _ = xn.shape
    proj = xn @ w_q
    qg = rmsnorm(proj[..., :NSA_HEADS * HEAD_DIM].reshape(B, T, NSA_KV_HEADS, GROUP, HEAD_DIM), q_norm)
    gates = jax.nn.sigmoid(proj[..., NSA_HEADS * HEAD_DIM:].astype(jnp.float32)).reshape(B, T, 3, NSA_KV_HEADS, GROUP)
    o_cmp, o_sel, o_win = attend(qg, kv)
    o = (gates[:, :, 0, ..., None] * o_cmp + gates[:, :, 1, ..., None] * o_sel
         + gates[:, :, 2, ..., None] * o_win)
    return o.astype(xn.dtype).reshape(B, T, NSA_HEADS * HEAD_DIM) @ w_o


def trunk(h, conv_bufs, deltas, attend, ffn_norm, ffn_w_gate, ffn_w_up, ffn_w_down, mix_norm,
          gdn_w_in, gdn_conv_w, gdn_a_log, gdn_dt_bias, gdn_o_norm, gdn_w_out,
          kv_norm, w_kv, k_norm, nsa_w_q, nsa_q_norm, nsa_w_o):
    new_conv, new_delta, kv = [], [], None
    for layer in range(DEPTH):
        h = h + 0.5 * swiglu(rmsnorm(h, ffn_norm[layer, 0]), ffn_w_gate[layer, 0], ffn_w_up[layer, 0], ffn_w_down[layer, 0])
        xn = rmsnorm(h, mix_norm[layer])
        if layer < N_A_LAYERS:
            o, cb, st = gdn_mixer(xn, conv_bufs[layer], deltas[layer], gdn_w_in[layer], gdn_conv_w[layer],
                                  gdn_a_log[layer], gdn_dt_bias[layer], gdn_o_norm[layer], gdn_w_out[layer])
            new_conv.append(cb)
            new_delta.append(st)
        else:
            j = layer - N_A_LAYERS
            o = nsa_mixer(xn, kv, attend, nsa_w_q[j], nsa_q_norm[j], nsa_w_o[j])
        h = h + o
        h = h + 0.5 * swiglu(rmsnorm(h, ffn_norm[layer, 1]), ffn_w_gate[layer, 1], ffn_w_up[layer, 1], ffn_w_down[layer, 1])
        if layer == N_A_LAYERS - 1:
            kv = shared_kv(h, kv_norm, w_kv, k_norm)
    return h, jnp.stack(new_conv), jnp.stack(new_delta), kv


def setup_inputs(seed: int = 0) -> dict:
    key = jax.random.key(seed)
    ks = jax.random.split(key, 32)
    f32 = jnp.float32
    n_pages = PAST_LEN // PAGE_SIZE
    n_used = DEC_BATCH * n_pages
    n_pool = n_used + (n_used + 3) // 4
    win_buf = min(WINDOW, PAST_LEN)

    def nrm(k, shape, scale=1.0):
        return jax.random.normal(k, shape, f32) * scale

    def gain(k, shape):
        return 1.0 + 0.05 * jax.random.normal(k, shape, f32)

    dt = jnp.exp(jax.random.uniform(ks[16], (N_A_LAYERS, GDN_HEADS), f32, math.log(1e-3), math.log(1e-1)))
    return {
        "x_prompt": nrm(ks[0], (BATCH, SEQ, D_MODEL)),
        "x_sample": nrm(ks[1], (DEC_BATCH, DEC_SEQ, D_MODEL)),
        "state_conv": nrm(ks[2], (N_A_LAYERS, DEC_BATCH, CONV_W - 1, GDN_CONV_CH)),
        "state_delta": nrm(ks[3], (N_A_LAYERS, DEC_BATCH, GDN_HEADS, GDN_DK, GDN_DV), 0.1),
        "cache_cmp_kv": nrm(ks[4], (n_pool, PAGE_SIZE, 2, NSA_KV_HEADS, HEAD_DIM)),
        "cache_sel_kv": nrm(ks[5], (n_pool, PAGE_SIZE, 2, NSA_KV_HEADS, HEAD_DIM)),
        "cache_win_kv": nrm(ks[6], (DEC_BATCH, win_buf, 2, NSA_KV_HEADS, HEAD_DIM)),
        "page_table": jax.random.permutation(ks[7], n_pool)[:n_used].reshape(DEC_BATCH, n_pages).astype(jnp.int32),
        "ffn_norm": gain(ks[8], (DEPTH, 2, D_MODEL)),
        "ffn_w_gate": nrm(ks[9], (DEPTH, 2, D_MODEL, D_FF), D_MODEL ** -0.5),
        "ffn_w_up": nrm(ks[10], (DEPTH, 2, D_MODEL, D_FF), D_MODEL ** -0.5),
        "ffn_w_down": nrm(ks[11], (DEPTH, 2, D_FF, D_MODEL), D_FF ** -0.5),
        "mix_norm": gain(ks[12], (DEPTH, D_MODEL)),
        "gdn_w_in": nrm(ks[13], (N_A_LAYERS, D_MODEL, GDN_IN), D_MODEL ** -0.5),
        "gdn_conv_w": nrm(ks[14], (N_A_LAYERS, CONV_W, GDN_CONV_CH), CONV_W ** -0.5),
        "gdn_a_log": jnp.log(jax.random.uniform(ks[15], (N_A_LAYERS, GDN_HEADS), f32, 1.0, 16.0)),
        "gdn_dt_bias": dt + jnp.log(-jnp.expm1(-dt)),
        "gdn_o_norm": gain(ks[17], (N_A_LAYERS, GDN_DV)),
        "gdn_w_out": nrm(ks[18], (N_A_LAYERS, GDN_V, D_MODEL), GDN_V ** -0.5),
        "kv_norm": gain(ks[19], (D_MODEL,)),
        "w_kv": nrm(ks[20], (D_MODEL, NSA_KV_W), D_MODEL ** -0.5),
        "cmp_pos_w": (1.0 + 0.2 * nrm(ks[21], (BLOCK, 2))) * BLOCK ** -0.5,
        "w_cmp": nrm(ks[22], (2, NSA_KV_HEADS, HEAD_DIM, HEAD_DIM), HEAD_DIM ** -0.5),
        "k_norm": gain(ks[23], (3, HEAD_DIM)),
        "nsa_w_q": nrm(ks[24], (N_B_LAYERS, D_MODEL, NSA_Q_IN), D_MODEL ** -0.5),
        "nsa_q_norm": gain(ks[25], (N_B_LAYERS, HEAD_DIM)),
        "nsa_w_o": nrm(ks[26], (N_B_LAYERS, NSA_HEADS * HEAD_DIM, D_MODEL), (NSA_HEADS * HEAD_DIM) ** -0.5),
    }


def reference(x_prompt, x_sample, state_conv, state_delta, cache_cmp_kv, cache_sel_kv, cache_win_kv, page_table,
              ffn_norm, ffn_w_gate, ffn_w_up, ffn_w_down, mix_norm,
              gdn_w_in, gdn_conv_w, gdn_a_log, gdn_dt_bias, gdn_o_norm, gdn_w_out,
              kv_norm, w_kv, cmp_pos_w, w_cmp, k_norm, nsa_w_q, nsa_q_norm, nsa_w_o):
    weights = (ffn_norm, ffn_w_gate, ffn_w_up, ffn_w_down, mix_norm,
               gdn_w_in, gdn_conv_w, gdn_a_log, gdn_dt_bias, gdn_o_norm, gdn_w_out,
               kv_norm, w_kv, k_norm, nsa_w_q, nsa_q_norm, nsa_w_o)
    B = x_prompt.shape[0]
    zero_conv = jnp.zeros((N_A_LAYERS, B, CONV_W - 1, GDN_CONV_CH), x_prompt.dtype)
    zero_delta = jnp.zeros((N_A_LAYERS, B, GDN_HEADS, GDN_DK, GDN_DV), jnp.float32)

    def attend_prompt(qg, kv):
        return nsa_prompt_branches(qg, kv, cmp_pos_w, w_cmp, k_norm[0])

    def attend_sample(qg, kv):
        return nsa_sample_branches(qg, kv, cache_cmp_kv, cache_sel_kv, cache_win_kv, page_table,
                                   cmp_pos_w, w_cmp, k_norm[0])

    y_prompt, conv_p, delta_p, kv_p = trunk(x_prompt, zero_conv, zero_delta, attend_prompt, *weights)
    y_sample, conv_s, delta_s, kv_s = trunk(x_sample, state_conv, state_delta, attend_sample, *weights)
    cmp_p, sel_p, win_rows_p = kv_p
    cmp_s, sel_s, win_rows_s = kv_s
    win_p = win_rows_p[:, -min(WINDOW, win_rows_p.shape[1]):]
    win_s = jnp.concatenate([cache_win_kv.astype(win_rows_s.dtype), win_rows_s], axis=1)[:, win_rows_s.shape[1]:]
    return (y_prompt, y_sample, conv_p, conv_s, delta_p, delta_s, cmp_p, cmp_s, sel_p, sel_s, win_p, win_s)
```

```python
import functools
import math

import jax
import jax.numpy as jnp
from jax import lax
from jax.experimental import pallas as pl
from jax.experimental.pallas import tpu as pltpu

D_MODEL = 4096
BATCH = 4
SEQ = 2048
DEPTH = 2
DEC_BATCH = 8
DEC_SEQ = 8
PAST_LEN = 16384
PAGE_SIZE = 128
N_A_LAYERS = DEPTH // 2
D_FF = 11008
EPS = 1e-6
GDN_HEADS = 16
GDN_DK = 128
GDN_DV = 256
CONV_W = 4
GDN_CHUNK = 64
GDN_QK = GDN_HEADS * GDN_DK
GDN_V = GDN_HEADS * GDN_DV
GDN_CONV_CH = 2 * GDN_QK + GDN_V
NSA_HEADS = 32
NSA_KV_HEADS = 4
HEAD_DIM = 128
GROUP = NSA_HEADS // NSA_KV_HEADS
BLOCK = 64
N_SELECT = 16
N_LOCAL = 2
WINDOW = 512
Q_CHUNK = 32
SCALE = HEAD_DIM ** -0.5
NEG = -1e30
FORCE = 1e4

N_PROMPT = BATCH * SEQ
N_SAMPLE = DEC_BATCH * DEC_SEQ
N_ROWS = N_PROMPT + N_SAMPLE

VMEM_LIMIT_BYTES = 56 * 1024 * 1024
LANES = 128

NORM_ROWS = 192
MM_ROWS = 1376
DOWN_ROWS = 688


def _params(*sem):
    return pltpu.CompilerParams(dimension_semantics=sem, vmem_limit_bytes=VMEM_LIMIT_BYTES)


def _rmsnorm_kernel(x_ref, g_ref, o_ref):
    x = x_ref[...]
    y = x * lax.rsqrt(jnp.mean(x * x, axis=-1, keepdims=True) + EPS)
    o_ref[...] = (y * g_ref[...]).astype(o_ref.dtype)


def rmsnorm_rows(x, g, out_dtype=jnp.bfloat16):
    n, d = x.shape
    return pl.pallas_call(
        _rmsnorm_kernel,
        grid=(n // NORM_ROWS,),
        in_specs=[pl.BlockSpec((NORM_ROWS, d), lambda i: (i, 0)),
                  pl.BlockSpec((1, d), lambda i: (0, 0))],
        out_specs=pl.BlockSpec((NORM_ROWS, d), lambda i: (i, 0)),
        out_shape=jax.ShapeDtypeStruct((n, d), out_dtype),
        compiler_params=_params("parallel"),
        name="rmsnorm_rows",
    )(x, g.reshape(1, d))


def _mm_kernel(x_ref, w_ref, o_ref):
    o_ref[...] = jnp.dot(x_ref[...], w_ref[...], preferred_element_type=jnp.float32).astype(o_ref.dtype)


def matmul(x, w, tn, out_dtype=jnp.float32, tm=MM_ROWS):
    m, k = x.shape
    n = w.shape[1]
    return pl.pallas_call(
        _mm_kernel,
        grid=(m // tm, n // tn),
        in_specs=[pl.BlockSpec((tm, k), lambda i, j: (i, 0)),
                  pl.BlockSpec((k, tn), lambda i, j: (0, j))],
        out_specs=pl.BlockSpec((tm, tn), lambda i, j: (i, j)),
        out_shape=jax.ShapeDtypeStruct((m, n), out_dtype),
        compiler_params=_params("parallel", "arbitrary"),
        name="matmul",
    )(x, w)


def _mm_res_kernel(x_ref, w_ref, r_ref, o_ref, *, scale):
    acc = jnp.dot(x_ref[...], w_ref[...], preferred_element_type=jnp.float32)
    o_ref[...] = r_ref[...] + scale * acc


def matmul_residual(x, w, res, scale, tn, tm):
    m, k = x.shape
    n = w.shape[1]
    return pl.pallas_call(
        functools.partial(_mm_res_kernel, scale=scale),
        grid=(m // tm, n // tn),
        in_specs=[pl.BlockSpec((tm, k), lambda i, j: (i, 0)),
                  pl.BlockSpec((k, tn), lambda i, j: (0, j)),
                  pl.BlockSpec((tm, tn), lambda i, j: (i, j))],
        out_specs=pl.BlockSpec((tm, tn), lambda i, j: (i, j)),
        out_shape=jax.ShapeDtypeStruct((m, n), jnp.float32),
        compiler_params=_params("parallel", "arbitrary"),
        name="matmul_residual",
    )(x, w, res)


def _swiglu_up_kernel(x_ref, wg_ref, wu_ref, o_ref):
    x = x_ref[...]
    g = jnp.dot(x, wg_ref[...], preferred_element_type=jnp.float32)
    u = jnp.dot(x, wu_ref[...], preferred_element_type=jnp.float32)
    o_ref[...] = (g * jax.nn.sigmoid(g) * u).astype(o_ref.dtype)


def swiglu_up(x, wg, wu, tn=256, tm=MM_ROWS):
    m, k = x.shape
    n = wg.shape[1]
    return pl.pallas_call(
        _swiglu_up_kernel,
        grid=(m // tm, n // tn),
        in_specs=[pl.BlockSpec((tm, k), lambda i, j: (i, 0)),
                  pl.BlockSpec((k, tn), lambda i, j: (0, j)),
                  pl.BlockSpec((k, tn), lambda i, j: (0, j))],
        out_specs=pl.BlockSpec((tm, tn), lambda i, j: (i, j)),
        out_shape=jax.ShapeDtypeStruct((m, n), jnp.bfloat16),
        compiler_params=_params("parallel", "arbitrary"),
        name="swiglu_up",
    )(x, wg, wu)


def ffn_half(h, norm_g, wg, wu, wd):
    xn = rmsnorm_rows(h, norm_g)
    act = swiglu_up(xn, wg, wu)
    return matmul_residual(act, wd, h, 0.5, tn=256, tm=DOWN_ROWS)


def _rms(x, g):
    return x * lax.rsqrt(jnp.mean(x * x, axis=-1, keepdims=True) + EPS) * g


def _l2(x):
    return x * lax.rsqrt(jnp.sum(x * x, axis=-1, keepdims=True) + EPS)


def _alibi():
    h = jnp.arange(1, NSA_HEADS + 1, dtype=jnp.float32)
    return (2.0 ** (-8.0 * h / NSA_HEADS)).reshape(NSA_KV_HEADS, GROUP)


def _short_conv(u, buf, w):
    T = u.shape[1]
    full = jnp.concatenate([buf.astype(u.dtype), u], axis=1)
    out = full[:, 0:T] * w[0]
    for i in range(1, CONV_W):
        out = out + full[:, i:i + T] * w[i]
    return jax.nn.silu(out), full[:, T:]


def _gated_delta_chunked(q, k, v, g, beta, s0):
    B, T, H, _ = q.shape
    C = min(GDN_CHUNK, T)
    n = -(-T // C)
    pad = n * C - T

    def prep(a):
        a = a.astype(jnp.float32)
        a = jnp.pad(a, [(0, 0), (0, pad)] + [(0, 0)] * (a.ndim - 2))
        return jnp.moveaxis(a.reshape((B, n, C) + a.shape[2:]), 1, 0)

    qc, kc, vc, gc, bc = (prep(a) for a in (q, k, v, g, beta))
    tri_incl = jnp.tril(jnp.ones((C, C), bool))
    tri_strict = jnp.tril(jnp.ones((C, C), bool), -1)
    eye = jnp.eye(C, dtype=jnp.float32)

    def step(S, inp):
        qi, ki, vi, gi, bi = inp
        gam = jnp.moveaxis(jnp.cumsum(gi, axis=1), 2, 1)
        diff = gam[..., :, None] - gam[..., None, :]
        decay = jnp.exp(jnp.where(tri_incl, diff, -jnp.inf))
        kk = jnp.einsum('bthd,bjhd->bhtj', ki, ki)
        qk = jnp.einsum('bthd,bjhd->bhtj', qi, ki)
        b_h = jnp.moveaxis(bi, 2, 1)
        a_mat = jnp.where(tri_strict, decay * kk, 0.0) * b_h[..., :, None]
        gt = jnp.exp(gam)
        ks0 = jnp.einsum('bthd,bhde->bhte', ki, S)
        rhs = b_h[..., None] * (jnp.moveaxis(vi, 2, 1) - gt[..., None] * ks0)
        u = lax.linalg.triangular_solve(eye + a_mat, rhs, left_side=True, lower=True, unit_diagonal=True)
        o = gt[..., None] * jnp.einsum('bthd,bhde->bhte', qi, S) + jnp.einsum('bhtj,bhje->bhte', decay * qk, u)
        wk = jnp.exp(gam[..., -1:] - gam)
        s_new = gt[..., -1][..., None, None] * S + jnp.einsum('bjhd,bhj,bhje->bhde', ki, wk, u)
        return s_new, jnp.moveaxis(o, 1, 2)

    s_fin, o = lax.scan(step, s0.astype(jnp.float32), (qc, kc, vc, gc, bc))
    o = jnp.moveaxis(o, 0, 1).reshape(B, n * C, H, -1)[:, :T]
    return o, s_fin


def _gdn_core(u, gate, a, b, conv_buf, s0, conv_w, a_log, dt_bias, o_norm):
    B, T, _ = u.shape
    uc, new_buf = _short_conv(u, conv_buf, conv_w)
    q = _l2(uc[..., :GDN_QK].reshape(B, T, GDN_HEADS, GDN_DK)) * GDN_DK ** -0.5
    k = _l2(uc[..., GDN_QK:2 * GDN_QK].reshape(B, T, GDN_HEADS, GDN_DK))
    v = uc[..., 2 * GDN_QK:].reshape(B, T, GDN_HEADS, GDN_DV)
    g = -jnp.exp(a_log) * jax.nn.softplus(a + dt_bias)
    beta = jax.nn.sigmoid(b)
    o, s_new = _gated_delta_chunked(q, k, v, g, beta, s0)
    o = _rms(o, o_norm) * jax.nn.silu(gate.reshape(B, T, GDN_HEADS, GDN_DV))
    return o.reshape(B, T, GDN_V), new_buf, s_new


def _compress_blocks(rows, cmp_pos_w, w_cmp, kn_cmp):
    B, L = rows.shape[:2]
    nb = L // BLOCK
    blk = rows[:, :nb * BLOCK].reshape(B, nb, BLOCK, 2, NSA_KV_HEADS, HEAD_DIM)
    pooled = jnp.einsum('bnjcgd,jc->bncgd', blk, cmp_pos_w)
    comp = jnp.einsum('bncgd,cgde->bncge', pooled, w_cmp)
    return _rms(comp[:, :, 0], kn_cmp), comp[:, :, 1]


def _compressed_branch(qg, ck, cv, qpos, slopes):
    nc = ck.shape[1]
    s = jnp.einsum('btgrd,bngd->btgrn', qg, ck) * SCALE
    dist = (qpos[:, None] - ((jnp.arange(nc) + 1) * BLOCK - 1)[None, :])
    valid = dist >= 0
    s = s - slopes[None, None, :, :, None] * dist.astype(jnp.float32)[None, :, None, None, :]
    s = jnp.where(valid[None, :, None, None, :], s, NEG)
    p = jax.nn.softmax(s, axis=-1) * jnp.any(valid, axis=-1).astype(jnp.float32)[None, :, None, None, None]
    o = jnp.einsum('btgrn,bngd->btgrd', p, cv)
    return o, jnp.sum(p, axis=3)


def _select_blocks(imp, qpos, n_blocks):
    imp = jnp.pad(imp, ((0, 0), (0, 0), (0, 0), (0, n_blocks - imp.shape[-1])))
    blk = jnp.arange(n_blocks)[None, :]
    cur = (qpos // BLOCK)[:, None]
    causal = blk <= cur
    forced = (blk == 0) | (causal & (blk > cur - N_LOCAL))
    score = jnp.where(forced[None, :, None, :], FORCE, jnp.where(causal[None, :, None, :], imp, -1.0))
    _, idx = lax.top_k(score, min(N_SELECT, n_blocks))
    return idx


def _gathered_attention(qg, kg, vg, qpos, kpos, slopes):
    s = jnp.einsum('btgrd,btgkjd->btgrkj', qg, kg) * SCALE
    dist = (qpos[None, :, None, None, None] - kpos)
    s = s - slopes[None, None, :, :, None, None] * dist.astype(jnp.float32)[:, :, :, None]
    s = jnp.where((dist >= 0)[:, :, :, None], s, NEG)
    B, T, G, R, K, J = s.shape
    p = jax.nn.softmax(s.reshape(B, T, G, R, K * J), axis=-1).reshape(s.shape)
    return jnp.einsum('btgrkj,btgkjd->btgrd', p, vg)


def _window_attention(qg, kw, vw, qpos, kpos, slopes):
    s = jnp.einsum('btgrd,bsgd->btgrs', qg, kw) * SCALE
    dist = qpos[:, None] - kpos[None, :]
    valid = (dist >= 0) & (dist < WINDOW) & (kpos >= 0)[None, :]
    s = s - slopes[None, None, :, :, None] * dist.astype(jnp.float32)[None, :, None, None, :]
    s = jnp.where(valid[None, :, None, None, :], s, NEG)
    p = jax.nn.softmax(s, axis=-1)
    return jnp.einsum('btgrs,bsgd->btgrd', p, vw)


def _nsa_prompt(qg, kv, cmp_pos_w, w_cmp, kn_cmp):
    cmp_rows, sel_rows, win_rows = kv
    B, T = qg.shape[:2]
    slopes = _alibi()
    qpos = jnp.arange(T)
    ck, cv = _compress_blocks(cmp_rows, cmp_pos_w, w_cmp, kn_cmp)
    o_cmp, imp = _compressed_branch(qg, ck, cv, qpos, slopes)
    n_blocks = T // BLOCK
    sel_idx = _select_blocks(imp, qpos, n_blocks)
    sel_blocks = sel_rows.reshape(B, n_blocks, BLOCK, 2, NSA_KV_HEADS, HEAD_DIM)
    win_pad = jnp.pad(win_rows, ((0, 0), (WINDOW, 0), (0, 0), (0, 0), (0, 0)))
    b_ix = jnp.arange(B)[:, None, None, None]
    g_ix = jnp.arange(NSA_KV_HEADS)[None, None, :, None]

    def chunk(c):
        t0 = c * Q_CHUNK
        qc = lax.dynamic_slice_in_dim(qg, t0, Q_CHUNK, axis=1)
        ic = lax.dynamic_slice_in_dim(sel_idx, t0, Q_CHUNK, axis=1)
        pos_c = t0 + jnp.arange(Q_CHUNK)
        kv_sel = sel_blocks[b_ix, ic, :, :, g_ix]
        kpos = ic[..., None] * BLOCK + jnp.arange(BLOCK)
        o_sel = _gathered_attention(qc, kv_sel[..., 0, :], kv_sel[..., 1, :], pos_c, kpos, slopes)
        kv_win = lax.dynamic_slice_in_dim(win_pad, t0, WINDOW + Q_CHUNK, axis=1)
        wpos = t0 - WINDOW + jnp.arange(WINDOW + Q_CHUNK)
        o_win = _window_attention(qc, kv_win[:, :, 0], kv_win[:, :, 1], pos_c, wpos, slopes)
        return o_sel, o_win

    o_sel, o_win = lax.map(chunk, jnp.arange(T // Q_CHUNK))
    o_sel = jnp.moveaxis(o_sel, 0, 1).reshape(qg.shape)
    o_win = jnp.moveaxis(o_win, 0, 1).reshape(qg.shape)
    return o_cmp, o_sel, o_win


def _nsa_sample(qg, kv, cache_cmp_kv, cache_sel_kv, cache_win_kv, page_table, cmp_pos_w, w_cmp, kn_cmp):
    new_cmp, new_sel, new_win = kv
    DB, S = qg.shape[:2]
    n_pages = page_table.shape[1]
    past = n_pages * PAGE_SIZE
    L = past + S
    slopes = _alibi()
    qpos = past + jnp.arange(S)
    past_cmp = cache_cmp_kv[page_table].reshape(DB, past, 2, NSA_KV_HEADS, HEAD_DIM)
    ck, cv = _compress_blocks(jnp.concatenate([past_cmp, new_cmp], axis=1), cmp_pos_w, w_cmp, kn_cmp)
    o_cmp, imp = _compressed_branch(qg, ck, cv, qpos, slopes)
    n_blocks = -(-L // BLOCK)
    idx = _select_blocks(imp, qpos, n_blocks)
    bpp = PAGE_SIZE // BLOCK
    n_past_blk = past // BLOCK
    n_new_blk = n_blocks - n_past_blk
    pool_blocks = cache_sel_kv.reshape(-1, BLOCK, 2, NSA_KV_HEADS, HEAD_DIM)
    new_blocks = jnp.pad(new_sel, ((0, 0), (0, n_new_blk * BLOCK - S), (0, 0), (0, 0), (0, 0))).reshape(
        DB, n_new_blk, BLOCK, 2, NSA_KV_HEADS, HEAD_DIM)
    b_ix = jnp.arange(DB)[:, None, None, None]
    g_ix = jnp.arange(NSA_KV_HEADS)[None, None, :, None]
    pidx = jnp.minimum(idx, n_past_blk - 1)
    phys = page_table[b_ix, pidx // bpp] * bpp + pidx % bpp
    kv_past = pool_blocks[phys, :, :, g_ix]
    nidx = jnp.clip(idx - n_past_blk, 0, n_new_blk - 1)
    kv_new = new_blocks[b_ix, nidx, :, :, g_ix]
    kv_sel = jnp.where((idx < n_past_blk)[..., None, None, None], kv_past, kv_new)
    kpos = idx[..., None] * BLOCK + jnp.arange(BLOCK)
    o_sel = _gathered_attention(qg, kv_sel[..., 0, :], kv_sel[..., 1, :], qpos, kpos, slopes)
    win_all = jnp.concatenate([cache_win_kv, new_win], axis=1)
    wb = cache_win_kv.shape[1]
    wpos = past - wb + jnp.arange(wb + S)
    o_win = _window_attention(qg, win_all[:, :, 0], win_all[:, :, 1], qpos, wpos, slopes)
    return o_cmp, o_sel, o_win


def _nsa_combine(q, gates_logit, branches, q_norm):
    B, T = q.shape[:2]
    qg = _rms(q.reshape(B, T, NSA_KV_HEADS, GROUP, HEAD_DIM), q_norm)
    gates = jax.nn.sigmoid(gates_logit).reshape(B, T, 3, NSA_KV_HEADS, GROUP)
    o_cmp, o_sel, o_win = branches(qg)
    o = (gates[:, :, 0, ..., None] * o_cmp + gates[:, :, 1, ..., None] * o_sel
         + gates[:, :, 2, ..., None] * o_win)
    return o.reshape(B * T, NSA_HEADS * HEAD_DIM)


def _split_rows(x):
    return (x[:N_PROMPT].reshape((BATCH, SEQ) + x.shape[1:]),
            x[N_PROMPT:].reshape((DEC_BATCH, DEC_SEQ) + x.shape[1:]))


def _pad_cols(w, n):
    return jnp.pad(w, ((0, 0), (0, n - w.shape[1])))


def kernel(x_prompt, x_sample, state_conv, state_delta, cache_cmp_kv, cache_sel_kv, cache_win_kv, page_table,
           ffn_norm, ffn_w_gate, ffn_w_up, ffn_w_down, mix_norm,
           gdn_w_in, gdn_conv_w, gdn_a_log, gdn_dt_bias, gdn_o_norm, gdn_w_out,
           kv_norm, w_kv, cmp_pos_w, w_cmp, k_norm, nsa_w_q, nsa_q_norm, nsa_w_o):
    bf = jnp.bfloat16
    h = jnp.concatenate([x_prompt.reshape(N_PROMPT, D_MODEL), x_sample.reshape(N_SAMPLE, D_MODEL)], axis=0)
    wg, wu, wd = ffn_w_gate.astype(bf), ffn_w_up.astype(bf), ffn_w_down.astype(bf)
    n_main = GDN_CONV_CH + GDN_V

    h = ffn_half(h, ffn_norm[0, 0], wg[0, 0], wu[0, 0], wd[0, 0])
    xn = rmsnorm_rows(h, mix_norm[0])
    w_in = gdn_w_in[0]
    proj = matmul(xn, w_in[:, :n_main].astype(bf), tn=512)
    ab = matmul(xn, _pad_cols(w_in[:, n_main:], LANES).astype(bf), tn=LANES)
    u_p, u_s = _split_rows(proj[:, :GDN_CONV_CH])
    gate_p, gate_s = _split_rows(proj[:, GDN_CONV_CH:])
    a_p, a_s = _split_rows(ab[:, :GDN_HEADS])
    b_p, b_s = _split_rows(ab[:, GDN_HEADS:2 * GDN_HEADS])
    zero_conv = jnp.zeros((BATCH, CONV_W - 1, GDN_CONV_CH), jnp.float32)
    zero_delta = jnp.zeros((BATCH, GDN_HEADS, GDN_DK, GDN_DV), jnp.float32)
    gargs = (gdn_conv_w[0], gdn_a_log[0], gdn_dt_bias[0], gdn_o_norm[0])
    o_p, conv_p, delta_p = _gdn_core(u_p, gate_p, a_p, b_p, zero_conv, zero_delta, *gargs)
    o_s, conv_s, delta_s = _gdn_core(u_s, gate_s, a_s, b_s, state_conv[0], state_delta[0], *gargs)
    o = jnp.concatenate([o_p.reshape(N_PROMPT, GDN_V), o_s.reshape(N_SAMPLE, GDN_V)], axis=0).astype(bf)
    h = matmul_residual(o, gdn_w_out[0].astype(bf), h, 1.0, tn=512, tm=MM_ROWS)
    h = ffn_half(h, ffn_norm[0, 1], wg[0, 1], wu[0, 1], wd[0, 1])

    kv = matmul(rmsnorm_rows(h, kv_norm), w_kv.astype(bf), tn=512)
    kv = kv.reshape(N_ROWS, 3, 2, NSA_KV_HEADS, HEAD_DIM)
    cmp_rows = kv[:, 0]
    sel_rows = jnp.stack([_rms(kv[:, 1, 0], k_norm[1]), kv[:, 1, 1]], axis=1)
    win_rows = jnp.stack([_rms(kv[:, 2, 0], k_norm[2]), kv[:, 2, 1]], axis=1)
    cmp_p, cmp_s = _split_rows(cmp_rows)
    sel_p, sel_s = _split_rows(sel_rows)
    win_rows_p, win_rows_s = _split_rows(win_rows)

    h = ffn_half(h, ffn_norm[1, 0], wg[1, 0], wu[1, 0], wd[1, 0])
    xn = rmsnorm_rows(h, mix_norm[1])
    w_q = nsa_w_q[0]
    n_q = NSA_HEADS * HEAD_DIM
    q = matmul(xn, w_q[:, :n_q].astype(bf), tn=512)
    gl = matmul(xn, _pad_cols(w_q[:, n_q:], LANES).astype(bf), tn=LANES)[:, :3 * NSA_HEADS]
    q_p, q_s = _split_rows(q)
    gl_p, gl_s = _split_rows(gl)
    o_p = _nsa_combine(q_p, gl_p, lambda qg: _nsa_prompt(qg, (cmp_p, sel_p, win_rows_p), cmp_pos_w, w_cmp, k_norm[0]),
                       nsa_q_norm[0])
    o_s = _nsa_combine(q_s, gl_s, lambda qg: _nsa_sample(qg, (cmp_s, sel_s, win_rows_s), cache_cmp_kv, cache_sel_kv,
                                                         cache_win_kv, page_table, cmp_pos_w, w_cmp, k_norm[0]),
                       nsa_q_norm[0])
    o = jnp.concatenate([o_p, o_s], axis=0).astype(bf)
    h = matmul_residual(o, nsa_w_o[0].astype(bf), h, 1.0, tn=512, tm=MM_ROWS)
    h = ffn_half(h, ffn_norm[1, 1], wg[1, 1], wu[1, 1], wd[1, 1])

    y_p, y_s = _split_rows(h)
    win_p = win_rows_p[:, -min(WINDOW, SEQ):]
    win_s = jnp.concatenate([cache_win_kv, win_rows_s], axis=1)[:, DEC_SEQ:]
    return (y_p, y_s, conv_p[None], conv_s[None], delta_p[None], delta_s[None],
            cmp_p, cmp_s, sel_p, sel_s, win_p, win_s)
```

```python
import functools
import math

import jax
import jax.numpy as jnp
from jax import lax
from jax.experimental import pallas as pl
from jax.experimental.pallas import tpu as pltpu

D_MODEL = 4096
BATCH = 4
SEQ = 2048
DEPTH = 2
DEC_BATCH = 8
DEC_SEQ = 8
PAST_LEN = 16384
PAGE_SIZE = 128
N_A_LAYERS = DEPTH // 2
D_FF = 11008
EPS = 1e-6
GDN_HEADS = 16
GDN_DK = 128
GDN_DV = 256
CONV_W = 4
GDN_CHUNK = 64
GDN_QK = GDN_HEADS * GDN_DK
GDN_V = GDN_HEADS * GDN_DV
GDN_CONV_CH = 2 * GDN_QK + GDN_V
NSA_HEADS = 32
NSA_KV_HEADS = 4
HEAD_DIM = 128
GROUP = NSA_HEADS // NSA_KV_HEADS
BLOCK = 64
N_SELECT = 16
N_LOCAL = 2
WINDOW = 512
Q_CHUNK = 32
SCALE = HEAD_DIM ** -0.5
NEG = -1e30
FORCE = 1e4

N_PROMPT = BATCH * SEQ
N_SAMPLE = DEC_BATCH * DEC_SEQ
N_ROWS = N_PROMPT + N_SAMPLE

VMEM_LIMIT_BYTES = 56 * 1024 * 1024
LANES = 128

NORM_ROWS = 192
MM_ROWS = 1376
DOWN_ROWS = 688


def _params(*sem):
    return pltpu.CompilerParams(dimension_semantics=sem, vmem_limit_bytes=VMEM_LIMIT_BYTES)


def _rmsnorm_kernel(x_ref, g_ref, o_ref):
    x = x_ref[...]
    y = x * lax.rsqrt(jnp.mean(x * x, axis=-1, keepdims=True) + EPS)
    o_ref[...] = (y * g_ref[...]).astype(o_ref.dtype)


def rmsnorm_rows(x, g, out_dtype=jnp.bfloat16):
    n, d = x.shape
    return pl.pallas_call(
        _rmsnorm_kernel,
        grid=(n // NORM_ROWS,),
        in_specs=[pl.BlockSpec((NORM_ROWS, d), lambda i: (i, 0)),
                  pl.BlockSpec((1, d), lambda i: (0, 0))],
        out_specs=pl.BlockSpec((NORM_ROWS, d), lambda i: (i, 0)),
        out_shape=jax.ShapeDtypeStruct((n, d), out_dtype),
        compiler_params=_params("parallel"),
        name="rmsnorm_rows",
    )(x, g.reshape(1, d))


def _mm_kernel(x_ref, w_ref, o_ref):
    o_ref[...] = jnp.dot(x_ref[...], w_ref[...], preferred_element_type=jnp.float32).astype(o_ref.dtype)


def matmul(x, w, tn, out_dtype=jnp.float32, tm=MM_ROWS):
    m, k = x.shape
    n = w.shape[1]
    return pl.pallas_call(
        _mm_kernel,
        grid=(m // tm, n // tn),
        in_specs=[pl.BlockSpec((tm, k), lambda i, j: (i, 0)),
                  pl.BlockSpec((k, tn), lambda i, j: (0, j))],
        out_specs=pl.BlockSpec((tm, tn), lambda i, j: (i, j)),
        out_shape=jax.ShapeDtypeStruct((m, n), out_dtype),
        compiler_params=_params("parallel", "arbitrary"),
        name="matmul",
    )(x, w)


def _mm_res_kernel(x_ref, w_ref, r_ref, o_ref, *, scale):
    acc = jnp.dot(x_ref[...], w_ref[...], preferred_element_type=jnp.float32)
    o_ref[...] = r_ref[...] + scale * acc


def matmul_residual(x, w, res, scale, tn, tm):
    m, k = x.shape
    n = w.shape[1]
    return pl.pallas_call(
        functools.partial(_mm_res_kernel, scale=scale),
        grid=(m // tm, n // tn),
        in_specs=[pl.BlockSpec((tm, k), lambda i, j: (i, 0)),
                  pl.BlockSpec((k, tn), lambda i, j: (0, j)),
                  pl.BlockSpec((tm, tn), lambda i, j: (i, j))],
        out_specs=pl.BlockSpec((tm, tn), lambda i, j: (i, j)),
        out_shape=jax.ShapeDtypeStruct((m, n), jnp.float32),
        compiler_params=_params("parallel", "arbitrary"),
        name="matmul_residual",
    )(x, w, res)


def _swiglu_up_kernel(x_ref, wg_ref, wu_ref, o_ref):
    x = x_ref[...]
    g = jnp.dot(x, wg_ref[...], preferred_element_type=jnp.float32)
    u = jnp.dot(x, wu_ref[...], preferred_element_type=jnp.float32)
    o_ref[...] = (g * jax.nn.sigmoid(g) * u).astype(o_ref.dtype)


def swiglu_up(x, wg, wu, tn=256, tm=MM_ROWS):
    m, k = x.shape
    n = wg.shape[1]
    return pl.pallas_call(
        _swiglu_up_kernel,
        grid=(m // tm, n // tn),
        in_specs=[pl.BlockSpec((tm, k), lambda i, j: (i, 0)),
                  pl.BlockSpec((k, tn), lambda i, j: (0, j)),
                  pl.BlockSpec((k, tn), lambda i, j: (0, j))],
        out_specs=pl.BlockSpec((tm, tn), lambda i, j: (i, j)),
        out_shape=jax.ShapeDtypeStruct((m, n), jnp.bfloat16),
        compiler_params=_params("parallel", "arbitrary"),
        name="swiglu_up",
    )(x, wg, wu)


def ffn_half(h, norm_g, wg, wu, wd):
    xn = rmsnorm_rows(h, norm_g)
    act = swiglu_up(xn, wg, wu)
    return matmul_residual(act, wd, h, 0.5, tn=256, tm=DOWN_ROWS)


_HI = lax.Precision.HIGHEST
_NT = (((1,), (1,)), ((), ()))
_TN = (((0,), (0,)), ((), ()))


def _bdot(a, b, dims=None):
    a = a.astype(jnp.bfloat16)
    b = b.astype(jnp.bfloat16)
    if dims is None:
        return jnp.dot(a, b, preferred_element_type=jnp.float32)
    return lax.dot_general(a, b, dims, preferred_element_type=jnp.float32)


def _hdot(a, b, dims=None):
    if dims is None:
        return jnp.dot(a, b, precision=_HI, preferred_element_type=jnp.float32)
    return lax.dot_general(a, b, dims, precision=_HI, preferred_element_type=jnp.float32)


def _iota2(shape, axis):
    return lax.broadcasted_iota(jnp.int32, shape, axis)


def _sigmoid(x):
    return 1.0 / (1.0 + jnp.exp(-x))


GDN_HB = 4
CONV_PAD = 8


def _gdn_kernel(uq_ref, uk_ref, uv_ref, gate_ref, ab_ref, cq_ref, ck_ref, cv_ref, s0_ref,
                wq_ref, wk_ref, wv_ref, hp_ref, onorm_ref,
                o_ref, s_ref, extq, extk, extv, *, hb, chunk, valid_rows):
    C = chunk
    c = pl.program_id(2)

    @pl.when(c == 0)
    def _():
        s_ref[...] = s0_ref[...]
        extq[0:CONV_PAD, :] = cq_ref[0]
        extk[0:CONV_PAD, :] = ck_ref[0]
        extv[0:CONV_PAD, :] = cv_ref[0]

    def conv(ext, u_ref, w_ref):
        ext[CONV_PAD:CONV_PAD + C, :] = u_ref[...]
        base = CONV_PAD - (CONV_W - 1)
        acc = ext[base:base + C, :] * w_ref[0:1, :]
        for i in range(1, CONV_W):
            acc = acc + ext[base + i:base + i + C, :] * w_ref[i:i + 1, :]
        ext[0:CONV_PAD, :] = ext[C:C + CONV_PAD, :]
        return acc * _sigmoid(acc)

    qc = conv(extq, uq_ref, wq_ref)
    kc = conv(extk, uk_ref, wk_ref)
    vc = conv(extv, uv_ref, wv_ref)

    row = _iota2((C, C), 0)
    col = _iota2((C, C), 1)
    tri_incl = row >= col
    tri_strict = row > col
    eye = (row == col).astype(jnp.float32)
    row_ok = None
    if valid_rows < C:
        row_ok = _iota2((C, 1), 0) < valid_rows

    ab = ab_ref[...]
    x = ab + hp_ref[0, 1:2, :]
    softplus = jnp.maximum(x, 0.0) + jnp.log(1.0 + jnp.exp(-jnp.abs(x)))
    gmat = -jnp.exp(hp_ref[0, 0:1, :]) * softplus
    if row_ok is not None:
        gmat = jnp.where(row_ok, gmat, 0.0)
    beta = _sigmoid(ab)
    gam = _hdot(tri_incl.astype(jnp.float32), gmat)
    eye_l = (_iota2((LANES, LANES), 0) == _iota2((LANES, LANES), 1)).astype(jnp.float32)
    gam_t = _hdot(eye_l, gam, _NT)

    for h in range(hb):
        gc = gam[:, h:h + 1]
        gr = gam_t[h:h + 1, :]
        bc = beta[:, hb + h:hb + h + 1]
        glast = gam[C - 1:C, h:h + 1]
        q = qc[:, h * GDN_DK:(h + 1) * GDN_DK]
        k = kc[:, h * GDN_DK:(h + 1) * GDN_DK]
        v = vc[:, h * GDN_DV:(h + 1) * GDN_DV]
        q = q * lax.rsqrt(jnp.sum(q * q, axis=-1, keepdims=True) + EPS) * GDN_DK ** -0.5
        k = k * lax.rsqrt(jnp.sum(k * k, axis=-1, keepdims=True) + EPS)
        if row_ok is not None:
            q = jnp.where(row_ok, q, 0.0)
            k = jnp.where(row_ok, k, 0.0)
            v = jnp.where(row_ok, v, 0.0)
        decay = jnp.exp(jnp.where(tri_incl, gc - gr, -jnp.inf))
        kk = _bdot(k, k, _NT)
        qk = _bdot(q, k, _NT)
        a_mat = jnp.where(tri_strict, decay * kk, 0.0) * bc
        inv = eye - a_mat
        pw = a_mat
        for _ in range(int(math.log2(C)) - 1):
            pw = _hdot(pw, pw)
            inv = inv + _hdot(inv, pw)
        s_old = s_ref[0, h]
        kq_s = _bdot(jnp.concatenate([k, q], axis=0), s_old)
        gt = jnp.exp(gc)
        rhs = bc * (v - gt * kq_s[:C])
        u = _hdot(inv, rhs)
        o = gt * kq_s[C:] + _bdot(decay * qk, u)
        wk = jnp.exp(glast - gc)
        s_ref[0, h] = jnp.exp(glast) * s_old + _bdot(k * wk, u, _TN)
        on = o * lax.rsqrt(jnp.mean(o * o, axis=-1, keepdims=True) + EPS) * onorm_ref[...]
        gate = gate_ref[:, h * GDN_DV:(h + 1) * GDN_DV]
        o_ref[:, h * GDN_DV:(h + 1) * GDN_DV] = (on * gate * _sigmoid(gate)).astype(o_ref.dtype)


def gdn_mixer(proj, ab, conv_init, s0, conv_w8, hp, o_norm, *, batch, n_chunks, valid_rows=GDN_CHUNK, hb=GDN_HB):
    C = GDN_CHUNK
    ng = GDN_HEADS // hb
    rows = batch * n_chunks * C
    qw, vw = hb * GDN_DK, hb * GDN_DV

    def rowblk(off):
        return lambda b, g, c: (b * n_chunks + c, off + g)

    def fixed3(off):
        return lambda b, g, c: (b, 0, off + g)

    def wblk(off):
        return lambda b, g, c: (0, off + g)

    in_specs = [
        pl.BlockSpec((C, qw), rowblk(0)),
        pl.BlockSpec((C, qw), rowblk(ng)),
        pl.BlockSpec((C, vw), rowblk(ng)),
        pl.BlockSpec((C, vw), rowblk(2 * ng)),
        pl.BlockSpec((C, LANES), rowblk(0)),
        pl.BlockSpec((1, CONV_PAD, qw), fixed3(0)),
        pl.BlockSpec((1, CONV_PAD, qw), fixed3(ng)),
        pl.BlockSpec((1, CONV_PAD, vw), fixed3(ng)),
        pl.BlockSpec((1, hb, GDN_DK, GDN_DV), lambda b, g, c: (b, g, 0, 0)),
        pl.BlockSpec((CONV_PAD, qw), wblk(0)),
        pl.BlockSpec((CONV_PAD, qw), wblk(ng)),
        pl.BlockSpec((CONV_PAD, vw), wblk(ng)),
        pl.BlockSpec((1, CONV_PAD, LANES), lambda b, g, c: (g, 0, 0)),
        pl.BlockSpec((1, GDN_DV), lambda b, g, c: (0, 0)),
    ]
    out_specs = [
        pl.BlockSpec((C, vw), rowblk(0)),
        pl.BlockSpec((1, hb, GDN_DK, GDN_DV), lambda b, g, c: (b, g, 0, 0)),
    ]
    return pl.pallas_call(
        functools.partial(_gdn_kernel, hb=hb, chunk=C, valid_rows=valid_rows),
        grid=(batch, ng, n_chunks),
        in_specs=in_specs,
        out_specs=out_specs,
        out_shape=[jax.ShapeDtypeStruct((rows, GDN_V), jnp.bfloat16),
                   jax.ShapeDtypeStruct((batch, GDN_HEADS, GDN_DK, GDN_DV), jnp.float32)],
        scratch_shapes=[pltpu.VMEM((CONV_PAD + C, qw), jnp.float32),
                        pltpu.VMEM((CONV_PAD + C, qw), jnp.float32),
                        pltpu.VMEM((CONV_PAD + C, vw), jnp.float32)],
        compiler_params=_params("parallel", "parallel", "arbitrary"),
        name="gdn_mixer",
    )(proj, proj, proj, proj, ab, conv_init, conv_init, conv_init, s0,
      conv_w8, conv_w8, conv_w8, hp, o_norm.reshape(1, GDN_DV))


def gdn_group_cols(w_ab, hb=GDN_HB):
    d = w_ab.shape[0]
    ng = GDN_HEADS // hb
    a = w_ab[:, :GDN_HEADS].reshape(d, ng, hb)
    b = w_ab[:, GDN_HEADS:].reshape(d, ng, hb)
    blk = jnp.concatenate([a, b, jnp.zeros((d, ng, LANES - 2 * hb), w_ab.dtype)], axis=-1)
    return blk.reshape(d, ng * LANES)


def gdn_head_params(a_log, dt_bias, hb=GDN_HB):
    ng = GDN_HEADS // hb
    rows = jnp.stack([a_log.reshape(ng, hb), dt_bias.reshape(ng, hb)], axis=1)
    return jnp.pad(rows, ((0, 0), (0, CONV_PAD - 2), (0, LANES - hb)))


KV_W = 3 * 2 * NSA_KV_HEADS * HEAD_DIM
KV_HALF = NSA_KV_HEADS * HEAD_DIM
N_KV_COLS = KV_W // HEAD_DIM
NB_PAD = LANES


def _kv_finish_kernel(kv_ref, gain_ref, o32_ref, o16_ref):
    for j in range(N_KV_COLS):
        sl = slice(j * HEAD_DIM, (j + 1) * HEAD_DIM)
        x = kv_ref[:, sl]
        if j in _NORMED_KV_COLS:
            x = x * lax.rsqrt(jnp.mean(x * x, axis=-1, keepdims=True) + EPS) * gain_ref[0:1, sl]
        o32_ref[:, sl] = x
        o16_ref[:, sl] = x.astype(o16_ref.dtype)


_NORMED_KV_COLS = tuple(range(2 * NSA_KV_HEADS, 3 * NSA_KV_HEADS)) + tuple(range(4 * NSA_KV_HEADS, 5 * NSA_KV_HEADS))


def kv_finish(kv, k_norm):
    n = kv.shape[0]
    ones = jnp.ones((KV_HALF,), jnp.float32)
    gain = jnp.concatenate([ones, ones, jnp.tile(k_norm[1], NSA_KV_HEADS), ones,
                            jnp.tile(k_norm[2], NSA_KV_HEADS), ones])
    gain = jnp.broadcast_to(gain[None], (8, KV_W))
    return pl.pallas_call(
        _kv_finish_kernel,
        grid=(n // NORM_ROWS,),
        in_specs=[pl.BlockSpec((NORM_ROWS, KV_W), lambda i: (i, 0)),
                  pl.BlockSpec((8, KV_W), lambda i: (0, 0))],
        out_specs=[pl.BlockSpec((NORM_ROWS, KV_W), lambda i: (i, 0)),
                   pl.BlockSpec((NORM_ROWS, KV_W), lambda i: (i, 0))],
        out_shape=[jax.ShapeDtypeStruct((n, KV_W), jnp.float32),
                   jax.ShapeDtypeStruct((n, KV_W), jnp.bfloat16)],
        compiler_params=_params("parallel"),
        name="kv_finish",
    )(kv, gain)


def _project_blocks(pooled_k, pooled_v, wcmp_ref, kn_ref, ck_ref, cv_ref, nb):
    ck_ref[...] = jnp.zeros_like(ck_ref)
    cv_ref[...] = jnp.zeros_like(cv_ref)
    for g in range(NSA_KV_HEADS):
        sl = slice(g * HEAD_DIM, (g + 1) * HEAD_DIM)
        k = _hdot(pooled_k[:, sl], wcmp_ref[0, g])
        k = k * lax.rsqrt(jnp.mean(k * k, axis=-1, keepdims=True) + EPS) * kn_ref[...]
        v = _hdot(pooled_v[:, sl], wcmp_ref[1, g])
        ck_ref[0, 0:nb, sl] = k.astype(ck_ref.dtype)
        cv_ref[0, 0:nb, sl] = v.astype(cv_ref.dtype)


def _compress_prompt_kernel(rows_ref, pw_ref, wcmp_ref, kn_ref, ck_ref, cv_ref, *, nb):
    pooled_k = _hdot(pw_ref[0], rows_ref[:, 0:KV_HALF])
    pooled_v = _hdot(pw_ref[1], rows_ref[:, KV_HALF:2 * KV_HALF])
    _project_blocks(pooled_k, pooled_v, wcmp_ref, kn_ref, ck_ref, cv_ref, nb)


def compress_prompt(kv32, cmp_pos_w, w_cmp, kn_cmp, *, batch, seq):
    nb = seq // BLOCK
    pw = jnp.einsum('nm,jc->cnmj', jnp.eye(nb, dtype=jnp.float32), cmp_pos_w).reshape(2, nb, seq)
    shape = jax.ShapeDtypeStruct((batch, NB_PAD, KV_HALF), jnp.bfloat16)
    return pl.pallas_call(
        functools.partial(_compress_prompt_kernel, nb=nb),
        grid=(batch,),
        in_specs=[pl.BlockSpec((seq, 2 * KV_HALF), lambda b: (b, 0)),
                  pl.BlockSpec((2, nb, seq), lambda b: (0, 0, 0)),
                  pl.BlockSpec((2, NSA_KV_HEADS, HEAD_DIM, HEAD_DIM), lambda b: (0, 0, 0, 0)),
                  pl.BlockSpec((1, HEAD_DIM), lambda b: (0, 0))],
        out_specs=[pl.BlockSpec((1, NB_PAD, KV_HALF), lambda b: (b, 0, 0)),
                   pl.BlockSpec((1, NB_PAD, KV_HALF), lambda b: (b, 0, 0))],
        out_shape=[shape, shape],
        compiler_params=_params("parallel"),
        name="compress_prompt",
    )(kv32, pw, w_cmp, kn_cmp.reshape(1, HEAD_DIM))


NSA_TQ = 256


def _softmax_rows(s):
    m = jnp.max(s, axis=-1, keepdims=True)
    e = jnp.exp(s - m)
    return e, jnp.sum(e, axis=-1, keepdims=True)


def _select_mask(imp, qpos, nb):
    lane = _iota2(imp.shape, 1)
    cur = qpos // BLOCK
    causal = lane <= cur
    forced = (lane == 0) | (causal & (lane > cur - N_LOCAL))
    score = jnp.where(forced, FORCE, jnp.where(causal, imp, -1.0))
    score = jnp.where(lane < nb, score, -2.0)
    rank = jnp.zeros(imp.shape, jnp.float32)
    for j in range(nb):
        cj = score[:, j:j + 1]
        rank = rank + jnp.where((cj > score) | ((cj == score) & (lane > j)), 1.0, 0.0)
    return (rank < float(min(N_SELECT, nb))) & (lane < nb)


def _nsa_prompt_kernel(q_ref, gl_ref, ck_ref, cv_ref, ksel_ref, vsel_ref, kwin_ref, vwin_ref,
                       qn_ref, sl_ref, ex_ref, o_ref, *, tq, seq):
    nb = seq // BLOCK
    t0 = pl.program_id(2) * tq
    qpos = t0 + _iota2((tq, 1), 0)
    gates = _sigmoid(gl_ref[...])
    qs = []
    for r in range(GROUP):
        x = q_ref[:, r * HEAD_DIM:(r + 1) * HEAD_DIM]
        x = x * lax.rsqrt(jnp.mean(x * x, axis=-1, keepdims=True) + EPS) * qn_ref[...]
        qs.append((x * SCALE).astype(jnp.bfloat16))
    slopes = [sl_ref[0, r:r + 1, 0:1] for r in range(GROUP)]

    lane = _iota2((1, NB_PAD), 1)
    blk_end = (lane + 1) * BLOCK - 1
    valid_c = (qpos >= blk_end) & (lane < nb)
    off_c = jnp.where(lane < nb, NEG, -jnp.inf)
    any_c = (qpos >= BLOCK - 1).astype(jnp.float32)
    rel_c = (blk_end - t0).astype(jnp.float32)
    ck = ck_ref[0]
    cv = cv_ref[0]
    imp = jnp.zeros((tq, NB_PAD), jnp.float32)
    outs = []
    for r in range(GROUP):
        s = _bdot(qs[r], ck, _NT) + slopes[r] * rel_c
        e, l = _softmax_rows(jnp.where(valid_c, s, off_c))
        p = e / l * any_c
        imp = imp + p
        outs.append(gates[:, r:r + 1] * _bdot(p, cv))

    sel = _select_mask(imp, qpos, nb)
    kpos = _iota2((1, seq), 1)
    allowed = (_bdot(sel.astype(jnp.bfloat16), ex_ref[...]) > 0.5) & (kpos <= qpos)
    mask_s = jnp.where(allowed, 0.0, NEG)
    rel_s = (kpos - t0).astype(jnp.float32)
    ksel = ksel_ref[...]
    vsel = vsel_ref[...]
    for r in range(GROUP):
        s = _bdot(qs[r], ksel, _NT) + slopes[r] * rel_s + mask_s
        e, l = _softmax_rows(s)
        outs[r] = outs[r] + gates[:, GROUP + r:GROUP + r + 1] * (_bdot(e, vsel) / l)

    nw = WINDOW + tq
    start = pl.multiple_of(jnp.maximum(t0 - WINDOW, 0), LANES)
    kwin = kwin_ref[pl.ds(start, nw), :]
    vwin = vwin_ref[pl.ds(start, nw), :]
    kpos_w = start + _iota2((1, nw), 1)
    dist = qpos - kpos_w
    mask_w = jnp.where((dist >= 0) & (dist < WINDOW), 0.0, NEG)
    rel_w = (kpos_w - t0).astype(jnp.float32)
    for r in range(GROUP):
        s = _bdot(qs[r], kwin, _NT) + slopes[r] * rel_w + mask_w
        e, l = _softmax_rows(s)
        o = outs[r] + gates[:, 2 * GROUP + r:2 * GROUP + r + 1] * (_bdot(e, vwin) / l)
        o_ref[:, r * HEAD_DIM:(r + 1) * HEAD_DIM] = o.astype(o_ref.dtype)


def nsa_slopes():
    h = jnp.arange(1, NSA_HEADS + 1, dtype=jnp.float32)
    s = (2.0 ** (-8.0 * h / NSA_HEADS)).reshape(NSA_KV_HEADS, GROUP, 1)
    return jnp.broadcast_to(s, (NSA_KV_HEADS, GROUP, LANES))


def nsa_gate_cols(w_g):
    d = w_g.shape[0]
    w = w_g.reshape(d, 3, NSA_KV_HEADS, GROUP).transpose(0, 2, 1, 3).reshape(d, NSA_KV_HEADS, 3 * GROUP)
    return jnp.pad(w, ((0, 0), (0, 0), (0, LANES - 3 * GROUP))).reshape(d, NSA_KV_HEADS * LANES)


def nsa_prompt(q, gl, ck, cv, kv16, q_norm, *, batch, seq, tq=NSA_TQ):
    nt = seq // tq
    nb = seq // BLOCK
    expand = (jnp.arange(NB_PAD)[:, None] == (jnp.arange(seq)[None, :] // BLOCK)).astype(jnp.bfloat16)
    kvw = 2 * NSA_KV_HEADS

    def kvspec(col0):
        return pl.BlockSpec((seq, HEAD_DIM), lambda b, g, i: (b, col0 + g))

    return pl.pallas_call(
        functools.partial(_nsa_prompt_kernel, tq=tq, seq=seq),
        grid=(batch, NSA_KV_HEADS, nt),
        in_specs=[pl.BlockSpec((tq, GROUP * HEAD_DIM), lambda b, g, i: (b * nt + i, g)),
                  pl.BlockSpec((tq, LANES), lambda b, g, i: (b * nt + i, g)),
                  pl.BlockSpec((1, NB_PAD, HEAD_DIM), lambda b, g, i: (b, 0, g)),
                  pl.BlockSpec((1, NB_PAD, HEAD_DIM), lambda b, g, i: (b, 0, g)),
                  kvspec(kvw), kvspec(kvw + NSA_KV_HEADS), kvspec(2 * kvw), kvspec(2 * kvw + NSA_KV_HEADS),
                  pl.BlockSpec((1, HEAD_DIM), lambda b, g, i: (0, 0)),
                  pl.BlockSpec((1, GROUP, LANES), lambda b, g, i: (g, 0, 0)),
                  pl.BlockSpec((NB_PAD, seq), lambda b, g, i: (0, 0))],
        out_specs=pl.BlockSpec((tq, GROUP * HEAD_DIM), lambda b, g, i: (b * nt + i, g)),
        out_shape=jax.ShapeDtypeStruct((batch * seq, NSA_HEADS * HEAD_DIM), jnp.bfloat16),
        compiler_params=_params("parallel", "parallel", "arbitrary"),
        name="nsa_prompt",
    )(q, gl, ck, cv, kv16, kv16, kv16, kv16, q_norm.reshape(1, HEAD_DIM), nsa_slopes(), expand)


N_PAGES = PAST_LEN // PAGE_SIZE
PAGES_PER_STEP = 4
BLOCKS_PER_PAGE = PAGE_SIZE // BLOCK
N_PAST_BLOCKS = PAST_LEN // BLOCK
N_SAMPLE_BLOCKS = -(-(PAST_LEN + DEC_SEQ) // BLOCK)
SEL_LANES = -(-N_SAMPLE_BLOCKS // LANES) * LANES
QROWS = GROUP * DEC_SEQ
PAGE_W = 2 * KV_HALF


def _page_specs():
    def spec(j):
        return pl.BlockSpec((1, PAGE_SIZE, PAGE_W),
                            lambda b, i, pt: (pt[b * N_PAGES + PAGES_PER_STEP * i + j], 0, 0))
    return [spec(j) for j in range(PAGES_PER_STEP)]


def _cmp_pool_kernel(pt_ref, p0, p1, p2, p3, w_ref, o_ref):
    i = pl.program_id(1)
    rows = []
    for p in (p0, p1, p2, p3):
        prod = p[0] * w_ref[...]
        for h in range(BLOCKS_PER_PAGE):
            rows.append(jnp.sum(prod[h * BLOCK:(h + 1) * BLOCK], axis=0, keepdims=True))
    n = PAGES_PER_STEP * BLOCKS_PER_PAGE
    o_ref[0, pl.ds(pl.multiple_of(i * n, n), n), :] = jnp.concatenate(rows, axis=0)


def cmp_pool_pages(cache, page_table, cmp_pos_w):
    wt = jnp.repeat(jnp.tile(cmp_pos_w, (BLOCKS_PER_PAGE, 1)), KV_HALF, axis=1)
    return pl.pallas_call(
        _cmp_pool_kernel,
        grid_spec=pltpu.PrefetchScalarGridSpec(
            num_scalar_prefetch=1,
            grid=(DEC_BATCH, N_PAGES // PAGES_PER_STEP),
            in_specs=_page_specs() + [pl.BlockSpec((PAGE_SIZE, PAGE_W), lambda b, i, pt: (0, 0))],
            out_specs=pl.BlockSpec((1, N_PAST_BLOCKS, PAGE_W), lambda b, i, pt: (b, 0, 0))),
        out_shape=jax.ShapeDtypeStruct((DEC_BATCH, N_PAST_BLOCKS, PAGE_W), jnp.float32),
        compiler_params=_params("parallel", "arbitrary"),
        name="cmp_pool_pages",
    )(page_table.reshape(-1), *([cache.reshape(-1, PAGE_SIZE, PAGE_W)] * PAGES_PER_STEP), wt)


def _sample_queries(q_ref, qn_ref, g):
    parts = []
    for r in range(GROUP):
        c0 = (g * GROUP + r) * HEAD_DIM
        x = q_ref[:, c0:c0 + HEAD_DIM]
        x = x * lax.rsqrt(jnp.mean(x * x, axis=-1, keepdims=True) + EPS) * qn_ref[...]
        parts.append(x * SCALE)
    return jnp.concatenate(parts, axis=0).astype(jnp.bfloat16)


def _head_major_col(ref2d, lane0):
    return jnp.concatenate([ref2d[:, lane0 + r:lane0 + r + 1] for r in range(GROUP)], axis=0)


def _slope_col(sl_ref, g):
    return jnp.concatenate([jnp.broadcast_to(sl_ref[g, r:r + 1, 0:1], (DEC_SEQ, 1)) for r in range(GROUP)], axis=0)


def _sample_select_kernel(pooled_ref, wcmp_ref, kn_ref, q_ref, gl_ref, qn_ref, sl_ref, wcache_ref, new_ref, ex_ref,
                          ocw_ref, mask_ref, newmask_ref):
    srow = _iota2((QROWS, 1), 0) % DEC_SEQ
    qpos = PAST_LEN + srow
    gl = gl_ref[...]
    nbp = N_PAST_BLOCKS
    lane_c = _iota2((1, nbp), 1)
    blk_end = (lane_c + 1) * BLOCK - 1
    rel_c = (blk_end - PAST_LEN).astype(jnp.float32)
    nw = wcache_ref.shape[1]
    nwk = nw + LANES
    jw = _iota2((1, nwk), 1)
    kpos_w = jnp.where(jw < nw, PAST_LEN - nw + jw, PAST_LEN + jw - nw)
    dist_w = qpos - kpos_w
    ok_w = (dist_w >= 0) & (dist_w < WINDOW) & (jw < nw + DEC_SEQ)
    mask_w = jnp.where(ok_w, 0.0, NEG)
    rel_w = (kpos_w - PAST_LEN).astype(jnp.float32)
    pad_rows = jnp.zeros((LANES - DEC_SEQ, HEAD_DIM), jnp.float32)
    for g in range(NSA_KV_HEADS):
        sl = slice(g * HEAD_DIM, (g + 1) * HEAD_DIM)
        slv = slice(KV_HALF + g * HEAD_DIM, KV_HALF + (g + 1) * HEAD_DIM)
        ck = _hdot(pooled_ref[0, :, sl], wcmp_ref[0, g])
        ck = ck * lax.rsqrt(jnp.mean(ck * ck, axis=-1, keepdims=True) + EPS) * kn_ref[...]
        cv = _hdot(pooled_ref[0, :, slv], wcmp_ref[1, g])
        qg = _sample_queries(q_ref, qn_ref, g)
        slope = _slope_col(sl_ref, g)
        gates = _sigmoid(gl[:, g * LANES:(g + 1) * LANES])
        s = _bdot(qg, ck, _NT) + slope * rel_c
        e, l = _softmax_rows(jnp.where(qpos >= blk_end, s, NEG))
        p = e / l
        o = _head_major_col(gates, 0) * _bdot(p, cv)
        imp = p[0:DEC_SEQ]
        for r in range(1, GROUP):
            imp = imp + p[r * DEC_SEQ:(r + 1) * DEC_SEQ]
        imp = jnp.concatenate([imp, jnp.zeros((DEC_SEQ, SEL_LANES - nbp), jnp.float32)], axis=1)
        sel = _select_mask(imp, PAST_LEN + _iota2((DEC_SEQ, 1), 0), N_SAMPLE_BLOCKS).astype(jnp.bfloat16)
        for half in range(nbp // LANES):
            keys = _bdot(sel[:, half * LANES:(half + 1) * LANES], ex_ref[...])
            mask_ref[0, g * DEC_SEQ:(g + 1) * DEC_SEQ, half * LANES * BLOCK:(half + 1) * LANES * BLOCK] = keys
        newmask_ref[0, g * DEC_SEQ:(g + 1) * DEC_SEQ, :] = jnp.broadcast_to(
            sel[:, nbp:nbp + 1].astype(jnp.float32), (DEC_SEQ, LANES))
        kw = jnp.concatenate([wcache_ref[0, :, sl], new_ref[:, 4 * KV_HALF + g * HEAD_DIM:4 * KV_HALF + (g + 1) * HEAD_DIM],
                              pad_rows], axis=0)
        vw = jnp.concatenate([wcache_ref[0, :, slv], new_ref[:, 5 * KV_HALF + g * HEAD_DIM:5 * KV_HALF + (g + 1) * HEAD_DIM],
                              pad_rows], axis=0)
        s = _bdot(qg, kw, _NT) + slope * rel_w + mask_w
        e, l = _softmax_rows(s)
        o = o + _head_major_col(gates, 2 * GROUP) * (_bdot(e, vw) / l)
        for r in range(GROUP):
            c0 = (g * GROUP + r) * HEAD_DIM
            ocw_ref[:, c0:c0 + HEAD_DIM] = o[r * DEC_SEQ:(r + 1) * DEC_SEQ]


def nsa_sample_select(pooled, w_cmp, kn_cmp, q, gl, q_norm, cache_win, kv32):
    expand = (jnp.arange(LANES)[:, None] == (jnp.arange(LANES * BLOCK)[None, :] // BLOCK)).astype(jnp.bfloat16)
    row0 = N_PROMPT // DEC_SEQ
    nw = cache_win.shape[1]
    return pl.pallas_call(
        _sample_select_kernel,
        grid=(DEC_BATCH,),
        in_specs=[pl.BlockSpec((1, N_PAST_BLOCKS, PAGE_W), lambda b: (b, 0, 0)),
                  pl.BlockSpec((2, NSA_KV_HEADS, HEAD_DIM, HEAD_DIM), lambda b: (0, 0, 0, 0)),
                  pl.BlockSpec((1, HEAD_DIM), lambda b: (0, 0)),
                  pl.BlockSpec((DEC_SEQ, NSA_HEADS * HEAD_DIM), lambda b: (row0 + b, 0)),
                  pl.BlockSpec((DEC_SEQ, NSA_KV_HEADS * LANES), lambda b: (row0 + b, 0)),
                  pl.BlockSpec((1, HEAD_DIM), lambda b: (0, 0)),
                  pl.BlockSpec((NSA_KV_HEADS, GROUP, LANES), lambda b: (0, 0, 0)),
                  pl.BlockSpec((1, nw, PAGE_W), lambda b: (b, 0, 0)),
                  pl.BlockSpec((DEC_SEQ, KV_W), lambda b: (row0 + b, 0)),
                  pl.BlockSpec((LANES, LANES * BLOCK), lambda b: (0, 0))],
        out_specs=[pl.BlockSpec((DEC_SEQ, NSA_HEADS * HEAD_DIM), lambda b: (b, 0)),
                   pl.BlockSpec((1, NSA_KV_HEADS * DEC_SEQ, PAST_LEN), lambda b: (b, 0, 0)),
                   pl.BlockSpec((1, NSA_KV_HEADS * DEC_SEQ, LANES), lambda b: (b, 0, 0))],
        out_shape=[jax.ShapeDtypeStruct((N_SAMPLE, NSA_HEADS * HEAD_DIM), jnp.float32),
                   jax.ShapeDtypeStruct((DEC_BATCH, NSA_KV_HEADS * DEC_SEQ, PAST_LEN), jnp.float32),
                   jax.ShapeDtypeStruct((DEC_BATCH, NSA_KV_HEADS * DEC_SEQ, LANES), jnp.float32)],
        compiler_params=_params("parallel"),
        name="nsa_sample_select",
    )(pooled, w_cmp, kn_cmp.reshape(1, HEAD_DIM), q, gl, q_norm.reshape(1, HEAD_DIM), nsa_slopes(),
      cache_win.reshape(DEC_BATCH, nw, PAGE_W), kv32, expand)


def _sample_sel_kernel(pt_ref, p0, p1, p2, p3, mask_ref, newmask_ref, q_ref, gl_ref, qn_ref, sl_ref, new_ref, ocw_ref,
                       o_ref, qs, m_sc, l_sc, acc_sc):
    i = pl.program_id(1)
    srow = _iota2((QROWS, 1), 0) % DEC_SEQ

    @pl.when(i == 0)
    def _():
        for g in range(NSA_KV_HEADS):
            qs[g] = _sample_queries(q_ref, qn_ref, g)
        m_sc[...] = jnp.full(m_sc.shape, NEG, jnp.float32)
        l_sc[...] = jnp.zeros(l_sc.shape, jnp.float32)
        acc_sc[...] = jnp.zeros(acc_sc.shape, jnp.float32)

    def accumulate(g, k, v, bias, keep):
        s = _bdot(qs[g], k, _NT) + bias + jnp.where(keep > 0.5, 0.0, NEG)
        m_old = m_sc[g]
        m_new = jnp.maximum(m_old, jnp.max(s, axis=-1, keepdims=True))
        alpha = jnp.exp(m_old - m_new)
        p = jnp.exp(s - m_new) * keep
        l_sc[g] = alpha * l_sc[g] + jnp.sum(p, axis=-1, keepdims=True)
        acc_sc[g] = alpha * acc_sc[g] + _bdot(p, v)
        m_sc[g] = m_new

    lane = _iota2((1, PAGE_SIZE), 1)
    for j, p in enumerate((p0, p1, p2, p3)):
        rel = ((i * PAGES_PER_STEP + j) * PAGE_SIZE - PAST_LEN + lane).astype(jnp.float32)
        for g in range(NSA_KV_HEADS):
            keep = mask_ref[0, g * DEC_SEQ:(g + 1) * DEC_SEQ, j * PAGE_SIZE:(j + 1) * PAGE_SIZE]
            keep = jnp.concatenate([keep] * GROUP, axis=0)
            accumulate(g, p[0, :, g * HEAD_DIM:(g + 1) * HEAD_DIM],
                       p[0, :, KV_HALF + g * HEAD_DIM:KV_HALF + (g + 1) * HEAD_DIM],
                       _slope_col(sl_ref, g) * rel, keep)

    @pl.when(i == pl.num_programs(1) - 1)
    def _():
        gl = gl_ref[...]
        pad_rows = jnp.zeros((PAGE_SIZE - DEC_SEQ, HEAD_DIM), jnp.float32)
        rel = lane.astype(jnp.float32)
        causal = ((lane <= srow) & (lane < DEC_SEQ)).astype(jnp.float32)
        for g in range(NSA_KV_HEADS):
            k = jnp.concatenate([new_ref[:, 2 * KV_HALF + g * HEAD_DIM:2 * KV_HALF + (g + 1) * HEAD_DIM], pad_rows], axis=0)
            v = jnp.concatenate([new_ref[:, 3 * KV_HALF + g * HEAD_DIM:3 * KV_HALF + (g + 1) * HEAD_DIM], pad_rows], axis=0)
            keep = jnp.concatenate([newmask_ref[0, g * DEC_SEQ:(g + 1) * DEC_SEQ, :]] * GROUP, axis=0) * causal
            accumulate(g, k, v, _slope_col(sl_ref, g) * rel, keep)
            gates = _sigmoid(gl[:, g * LANES:(g + 1) * LANES])
            o = _head_major_col(gates, GROUP) * (acc_sc[g] / l_sc[g])
            for r in range(GROUP):
                c0 = (g * GROUP + r) * HEAD_DIM
                o_ref[:, c0:c0 + HEAD_DIM] = ocw_ref[:, c0:c0 + HEAD_DIM] + o[r * DEC_SEQ:(r + 1) * DEC_SEQ]


def nsa_sample_sel(cache, page_table, mask, newmask, q, gl, q_norm, kv32, ocw):
    row0 = N_PROMPT // DEC_SEQ
    nrow = NSA_KV_HEADS * DEC_SEQ
    step_keys = PAGES_PER_STEP * PAGE_SIZE
    return pl.pallas_call(
        _sample_sel_kernel,
        grid_spec=pltpu.PrefetchScalarGridSpec(
            num_scalar_prefetch=1,
            grid=(DEC_BATCH, N_PAGES // PAGES_PER_STEP),
            in_specs=_page_specs() + [
                pl.BlockSpec((1, nrow, step_keys), lambda b, i, pt: (b, 0, i)),
                pl.BlockSpec((1, nrow, LANES), lambda b, i, pt: (b, 0, 0)),
                pl.BlockSpec((DEC_SEQ, NSA_HEADS * HEAD_DIM), lambda b, i, pt: (row0 + b, 0)),
                pl.BlockSpec((DEC_SEQ, NSA_KV_HEADS * LANES), lambda b, i, pt: (row0 + b, 0)),
                pl.BlockSpec((1, HEAD_DIM), lambda b, i, pt: (0, 0)),
                pl.BlockSpec((NSA_KV_HEADS, GROUP, LANES), lambda b, i, pt: (0, 0, 0)),
                pl.BlockSpec((DEC_SEQ, KV_W), lambda b, i, pt: (row0 + b, 0)),
                pl.BlockSpec((DEC_SEQ, NSA_HEADS * HEAD_DIM), lambda b, i, pt: (b, 0))],
            out_specs=pl.BlockSpec((DEC_SEQ, NSA_HEADS * HEAD_DIM), lambda b, i, pt: (b, 0)),
            scratch_shapes=[pltpu.VMEM((NSA_KV_HEADS, QROWS, HEAD_DIM), jnp.bfloat16),
                            pltpu.VMEM((NSA_KV_HEADS, QROWS, 1), jnp.float32),
                            pltpu.VMEM((NSA_KV_HEADS, QROWS, 1), jnp.float32),
                            pltpu.VMEM((NSA_KV_HEADS, QROWS, HEAD_DIM), jnp.float32)]),
        out_shape=jax.ShapeDtypeStruct((N_SAMPLE, NSA_HEADS * HEAD_DIM), jnp.float32),
        compiler_params=_params("parallel", "arbitrary"),
        name="nsa_sample_sel",
    )(page_table.reshape(-1), *([cache.reshape(-1, PAGE_SIZE, PAGE_W)] * PAGES_PER_STEP), mask, newmask,
      q, gl, q_norm.reshape(1, HEAD_DIM), nsa_slopes(), kv32, ocw)


def nsa_sample(q, gl, kv32, cache_cmp_kv, cache_sel_kv, cache_win_kv, page_table, cmp_pos_w, w_cmp, kn_cmp, q_norm):
    pooled = cmp_pool_pages(cache_cmp_kv, page_table, cmp_pos_w)
    ocw, mask, newmask = nsa_sample_select(pooled, w_cmp, kn_cmp, q, gl, q_norm, cache_win_kv, kv32)
    return nsa_sample_sel(cache_sel_kv, page_table, mask, newmask, q, gl, q_norm, kv32, ocw)


def _rms(x, g):
    return x * lax.rsqrt(jnp.mean(x * x, axis=-1, keepdims=True) + EPS) * g


def _l2(x):
    return x * lax.rsqrt(jnp.sum(x * x, axis=-1, keepdims=True) + EPS)


def _alibi():
    h = jnp.arange(1, NSA_HEADS + 1, dtype=jnp.float32)
    return (2.0 ** (-8.0 * h / NSA_HEADS)).reshape(NSA_KV_HEADS, GROUP)


def _short_conv(u, buf, w):
    T = u.shape[1]
    full = jnp.concatenate([buf.astype(u.dtype), u], axis=1)
    out = full[:, 0:T] * w[0]
    for i in range(1, CONV_W):
        out = out + full[:, i:i + T] * w[i]
    return jax.nn.silu(out), full[:, T:]


def _gated_delta_chunked(q, k, v, g, beta, s0):
    B, T, H, _ = q.shape
    C = min(GDN_CHUNK, T)
    n = -(-T // C)
    pad = n * C - T

    def prep(a):
        a = a.astype(jnp.float32)
        a = jnp.pad(a, [(0, 0), (0, pad)] + [(0, 0)] * (a.ndim - 2))
        return jnp.moveaxis(a.reshape((B, n, C) + a.shape[2:]), 1, 0)

    qc, kc, vc, gc, bc = (prep(a) for a in (q, k, v, g, beta))
    tri_incl = jnp.tril(jnp.ones((C, C), bool))
    tri_strict = jnp.tril(jnp.ones((C, C), bool), -1)
    eye = jnp.eye(C, dtype=jnp.float32)

    def step(S, inp):
        qi, ki, vi, gi, bi = inp
        gam = jnp.moveaxis(jnp.cumsum(gi, axis=1), 2, 1)
        diff = gam[..., :, None] - gam[..., None, :]
        decay = jnp.exp(jnp.where(tri_incl, diff, -jnp.inf))
        kk = jnp.einsum('bthd,bjhd->bhtj', ki, ki)
        qk = jnp.einsum('bthd,bjhd->bhtj', qi, ki)
        b_h = jnp.moveaxis(bi, 2, 1)
        a_mat = jnp.where(tri_strict, decay * kk, 0.0) * b_h[..., :, None]
        gt = jnp.exp(gam)
        ks0 = jnp.einsum('bthd,bhde->bhte', ki, S)
        rhs = b_h[..., None] * (jnp.moveaxis(vi, 2, 1) - gt[..., None] * ks0)
        u = lax.linalg.triangular_solve(eye + a_mat, rhs, left_side=True, lower=True, unit_diagonal=True)
        o = gt[..., None] * jnp.einsum('bthd,bhde->bhte', qi, S) + jnp.einsum('bhtj,bhje->bhte', decay * qk, u)
        wk = jnp.exp(gam[..., -1:] - gam)
        s_new = gt[..., -1][..., None, None] * S + jnp.einsum('bjhd,bhj,bhje->bhde', ki, wk, u)
        return s_new, jnp.moveaxis(o, 1, 2)

    s_fin, o = lax.scan(step, s0.astype(jnp.float32), (qc, kc, vc, gc, bc))
    o = jnp.moveaxis(o, 0, 1).reshape(B, n * C, H, -1)[:, :T]
    return o, s_fin


def _gdn_core(u, gate, a, b, conv_buf, s0, conv_w, a_log, dt_bias, o_norm):
    B, T, _ = u.shape
    uc, new_buf = _short_conv(u, conv_buf, conv_w)
    q = _l2(uc[..., :GDN_QK].reshape(B, T, GDN_HEADS, GDN_DK)) * GDN_DK ** -0.5
    k = _l2(uc[..., GDN_QK:2 * GDN_QK].reshape(B, T, GDN_HEADS, GDN_DK))
    v = uc[..., 2 * GDN_QK:].reshape(B, T, GDN_HEADS, GDN_DV)
    g = -jnp.exp(a_log) * jax.nn.softplus(a + dt_bias)
    beta = jax.nn.sigmoid(b)
    o, s_new = _gated_delta_chunked(q, k, v, g, beta, s0)
    o = _rms(o, o_norm) * jax.nn.silu(gate.reshape(B, T, GDN_HEADS, GDN_DV))
    return o.reshape(B, T, GDN_V), new_buf, s_new


def _compress_blocks(rows, cmp_pos_w, w_cmp, kn_cmp):
    B, L = rows.shape[:2]
    nb = L // BLOCK
    blk = rows[:, :nb * BLOCK].reshape(B, nb, BLOCK, 2, NSA_KV_HEADS, HEAD_DIM)
    pooled = jnp.einsum('bnjcgd,jc->bncgd', blk, cmp_pos_w)
    comp = jnp.einsum('bncgd,cgde->bncge', pooled, w_cmp)
    return _rms(comp[:, :, 0], kn_cmp), comp[:, :, 1]


def _compressed_branch(qg, ck, cv, qpos, slopes):
    nc = ck.shape[1]
    s = jnp.einsum('btgrd,bngd->btgrn', qg, ck) * SCALE
    dist = (qpos[:, None] - ((jnp.arange(nc) + 1) * BLOCK - 1)[None, :])
    valid = dist >= 0
    s = s - slopes[None, None, :, :, None] * dist.astype(jnp.float32)[None, :, None, None, :]
    s = jnp.where(valid[None, :, None, None, :], s, NEG)
    p = jax.nn.softmax(s, axis=-1) * jnp.any(valid, axis=-1).astype(jnp.float32)[None, :, None, None, None]
    o = jnp.einsum('btgrn,bngd->btgrd', p, cv)
    return o, jnp.sum(p, axis=3)


def _select_blocks(imp, qpos, n_blocks):
    imp = jnp.pad(imp, ((0, 0), (0, 0), (0, 0), (0, n_blocks - imp.shape[-1])))
    blk = jnp.arange(n_blocks)[None, :]
    cur = (qpos // BLOCK)[:, None]
    causal = blk <= cur
    forced = (blk == 0) | (causal & (blk > cur - N_LOCAL))
    score = jnp.where(forced[None, :, None, :], FORCE, jnp.where(causal[None, :, None, :], imp, -1.0))
    _, idx = lax.top_k(score, min(N_SELECT, n_blocks))
    return idx


def _gathered_attention(qg, kg, vg, qpos, kpos, slopes):
    s = jnp.einsum('btgrd,btgkjd->btgrkj', qg, kg) * SCALE
    dist = (qpos[None, :, None, None, None] - kpos)
    s = s - slopes[None, None, :, :, None, None] * dist.astype(jnp.float32)[:, :, :, None]
    s = jnp.where((dist >= 0)[:, :, :, None], s, NEG)
    B, T, G, R, K, J = s.shape
    p = jax.nn.softmax(s.reshape(B, T, G, R, K * J), axis=-1).reshape(s.shape)
    return jnp.einsum('btgrkj,btgkjd->btgrd', p, vg)


def _window_attention(qg, kw, vw, qpos, kpos, slopes):
    s = jnp.einsum('btgrd,bsgd->btgrs', qg, kw) * SCALE
    dist = qpos[:, None] - kpos[None, :]
    valid = (dist >= 0) & (dist < WINDOW) & (kpos >= 0)[None, :]
    s = s - slopes[None, None, :, :, None] * dist.astype(jnp.float32)[None, :, None, None, :]
    s = jnp.where(valid[None, :, None, None, :], s, NEG)
    p = jax.nn.softmax(s, axis=-1)
    return jnp.einsum('btgrs,bsgd->btgrd', p, vw)


def _nsa_prompt(qg, kv, cmp_pos_w, w_cmp, kn_cmp):
    cmp_rows, sel_rows, win_rows = kv
    B, T = qg.shape[:2]
    slopes = _alibi()
    qpos = jnp.arange(T)
    ck, cv = _compress_blocks(cmp_rows, cmp_pos_w, w_cmp, kn_cmp)
    o_cmp, imp = _compressed_branch(qg, ck, cv, qpos, slopes)
    n_blocks = T // BLOCK
    sel_idx = _select_blocks(imp, qpos, n_blocks)
    sel_blocks = sel_rows.reshape(B, n_blocks, BLOCK, 2, NSA_KV_HEADS, HEAD_DIM)
    win_pad = jnp.pad(win_rows, ((0, 0), (WINDOW, 0), (0, 0), (0, 0), (0, 0)))
    b_ix = jnp.arange(B)[:, None, None, None]
    g_ix = jnp.arange(NSA_KV_HEADS)[None, None, :, None]

    def chunk(c):
        t0 = c * Q_CHUNK
        qc = lax.dynamic_slice_in_dim(qg, t0, Q_CHUNK, axis=1)
        ic = lax.dynamic_slice_in_dim(sel_idx, t0, Q_CHUNK, axis=1)
        pos_c = t0 + jnp.arange(Q_CHUNK)
        kv_sel = sel_blocks[b_ix, ic, :, :, g_ix]
        kpos = ic[..., None] * BLOCK + jnp.arange(BLOCK)
        o_sel = _gathered_attention(qc, kv_sel[..., 0, :], kv_sel[..., 1, :], pos_c, kpos, slopes)
        kv_win = lax.dynamic_slice_in_dim(win_pad, t0, WINDOW + Q_CHUNK, axis=1)
        wpos = t0 - WINDOW + jnp.arange(WINDOW + Q_CHUNK)
        o_win = _window_attention(qc, kv_win[:, :, 0], kv_win[:, :, 1], pos_c, wpos, slopes)
        return o_sel, o_win

    o_sel, o_win = lax.map(chunk, jnp.arange(T // Q_CHUNK))
    o_sel = jnp.moveaxis(o_sel, 0, 1).reshape(qg.shape)
    o_win = jnp.moveaxis(o_win, 0, 1).reshape(qg.shape)
    return o_cmp, o_sel, o_win


def _nsa_sample(qg, kv, cache_cmp_kv, cache_sel_kv, cache_win_kv, page_table, cmp_pos_w, w_cmp, kn_cmp):
    new_cmp, new_sel, new_win = kv
    DB, S = qg.shape[:2]
    n_pages = page_table.shape[1]
    past = n_pages * PAGE_SIZE
    L = past + S
    slopes = _alibi()
    qpos = past + jnp.arange(S)
    past_cmp = cache_cmp_kv[page_table].reshape(DB, past, 2, NSA_KV_HEADS, HEAD_DIM)
    ck, cv = _compress_blocks(jnp.concatenate([past_cmp, new_cmp], axis=1), cmp_pos_w, w_cmp, kn_cmp)
    o_cmp, imp = _compressed_branch(qg, ck, cv, qpos, slopes)
    n_blocks = -(-L // BLOCK)
    idx = _select_blocks(imp, qpos, n_blocks)
    bpp = PAGE_SIZE // BLOCK
    n_past_blk = past // BLOCK
    n_new_blk = n_blocks - n_past_blk
    pool_blocks = cache_sel_kv.reshape(-1, BLOCK, 2, NSA_KV_HEADS, HEAD_DIM)
    new_blocks = jnp.pad(new_sel, ((0, 0), (0, n_new_blk * BLOCK - S), (0, 0), (0, 0), (0, 0))).reshape(
        DB, n_new_blk, BLOCK, 2, NSA_KV_HEADS, HEAD_DIM)
    b_ix = jnp.arange(DB)[:, None, None, None]
    g_ix = jnp.arange(NSA_KV_HEADS)[None, None, :, None]
    pidx = jnp.minimum(idx, n_past_blk - 1)
    phys = page_table[b_ix, pidx // bpp] * bpp + pidx % bpp
    kv_past = pool_blocks[phys, :, :, g_ix]
    nidx = jnp.clip(idx - n_past_blk, 0, n_new_blk - 1)
    kv_new = new_blocks[b_ix, nidx, :, :, g_ix]
    kv_sel = jnp.where((idx < n_past_blk)[..., None, None, None], kv_past, kv_new)
    kpos = idx[..., None] * BLOCK + jnp.arange(BLOCK)
    o_sel = _gathered_attention(qg, kv_sel[..., 0, :], kv_sel[..., 1, :], qpos, kpos, slopes)
    win_all = jnp.concatenate([cache_win_kv, new_win], axis=1)
    wb = cache_win_kv.shape[1]
    wpos = past - wb + jnp.arange(wb + S)
    o_win = _window_attention(qg, win_all[:, :, 0], win_all[:, :, 1], qpos, wpos, slopes)
    return o_cmp, o_sel, o_win


def _nsa_combine(q, gates_logit, branches, q_norm):
    B, T = q.shape[:2]
    qg = _rms(q.reshape(B, T, NSA_KV_HEADS, GROUP, HEAD_DIM), q_norm)
    gates = jax.nn.sigmoid(gates_logit).reshape(B, T, 3, NSA_KV_HEADS, GROUP)
    o_cmp, o_sel, o_win = branches(qg)
    o = (gates[:, :, 0, ..., None] * o_cmp + gates[:, :, 1, ..., None] * o_sel
         + gates[:, :, 2, ..., None] * o_win)
    return o.reshape(B * T, NSA_HEADS * HEAD_DIM)


def _split_rows(x):
    return (x[:N_PROMPT].reshape((BATCH, SEQ) + x.shape[1:]),
            x[N_PROMPT:].reshape((DEC_BATCH, DEC_SEQ) + x.shape[1:]))


def _pad_cols(w, n):
    return jnp.pad(w, ((0, 0), (0, n - w.shape[1])))


def kernel(x_prompt, x_sample, state_conv, state_delta, cache_cmp_kv, cache_sel_kv, cache_win_kv, page_table,
           ffn_norm, ffn_w_gate, ffn_w_up, ffn_w_down, mix_norm,
           gdn_w_in, gdn_conv_w, gdn_a_log, gdn_dt_bias, gdn_o_norm, gdn_w_out,
           kv_norm, w_kv, cmp_pos_w, w_cmp, k_norm, nsa_w_q, nsa_q_norm, nsa_w_o):
    bf = jnp.bfloat16
    h = jnp.concatenate([x_prompt.reshape(N_PROMPT, D_MODEL), x_sample.reshape(N_SAMPLE, D_MODEL)], axis=0)
    wg, wu, wd = ffn_w_gate.astype(bf), ffn_w_up.astype(bf), ffn_w_down.astype(bf)
    n_main = GDN_CONV_CH + GDN_V

    h = ffn_half(h, ffn_norm[0, 0], wg[0, 0], wu[0, 0], wd[0, 0])
    xn = rmsnorm_rows(h, mix_norm[0])
    w_in = gdn_w_in[0]
    proj = matmul(xn, w_in[:, :n_main].astype(bf), tn=512)
    ab = matmul(xn, gdn_group_cols(w_in[:, n_main:]).astype(bf), tn=LANES)
    conv_w8 = jnp.pad(gdn_conv_w[0], ((0, CONV_PAD - CONV_W), (0, 0)))
    hp = gdn_head_params(gdn_a_log[0], gdn_dt_bias[0])
    hist = CONV_W - 1

    def pad_sample(x):
        x = x[N_PROMPT:].reshape(DEC_BATCH, DEC_SEQ, x.shape[1])
        return jnp.pad(x, ((0, 0), (0, GDN_CHUNK - DEC_SEQ), (0, 0))).reshape(DEC_BATCH * GDN_CHUNK, x.shape[2])

    o_p, delta_p = gdn_mixer(proj, ab, jnp.zeros((BATCH, CONV_PAD, GDN_CONV_CH), jnp.float32),
                             jnp.zeros((BATCH, GDN_HEADS, GDN_DK, GDN_DV), jnp.float32),
                             conv_w8, hp, gdn_o_norm[0], batch=BATCH, n_chunks=SEQ // GDN_CHUNK)
    o_s, delta_s = gdn_mixer(pad_sample(proj), pad_sample(ab),
                             jnp.pad(state_conv[0], ((0, 0), (CONV_PAD - hist, 0), (0, 0))), state_delta[0],
                             conv_w8, hp, gdn_o_norm[0], batch=DEC_BATCH, n_chunks=1, valid_rows=DEC_SEQ)
    o_s = o_s.reshape(DEC_BATCH, GDN_CHUNK, GDN_V)[:, :DEC_SEQ].reshape(N_SAMPLE, GDN_V)
    u_p, u_s = _split_rows(proj[:, :GDN_CONV_CH])
    conv_p = u_p[:, SEQ - hist:]
    conv_s = jnp.concatenate([state_conv[0], u_s], axis=1)[:, DEC_SEQ:]
    o = jnp.concatenate([o_p, o_s], axis=0)
    h = matmul_residual(o, gdn_w_out[0].astype(bf), h, 1.0, tn=512, tm=MM_ROWS)
    h = ffn_half(h, ffn_norm[0, 1], wg[0, 1], wu[0, 1], wd[0, 1])

    kv32, kv16 = kv_finish(matmul(rmsnorm_rows(h, kv_norm), w_kv.astype(bf), tn=512), k_norm)
    kv5 = kv32.reshape(N_ROWS, 3, 2, NSA_KV_HEADS, HEAD_DIM)
    cmp_p, cmp_s = _split_rows(kv5[:, 0])
    sel_p, sel_s = _split_rows(kv5[:, 1])
    win_rows_p, win_rows_s = _split_rows(kv5[:, 2])

    h = ffn_half(h, ffn_norm[1, 0], wg[1, 0], wu[1, 0], wd[1, 0])
    xn = rmsnorm_rows(h, mix_norm[1])
    w_q = nsa_w_q[0]
    n_q = NSA_HEADS * HEAD_DIM
    q = matmul(xn, w_q[:, :n_q].astype(bf), tn=512)
    gl = matmul(xn, nsa_gate_cols(w_q[:, n_q:]).astype(bf), tn=LANES)
    ck, cv = compress_prompt(kv32, cmp_pos_w, w_cmp, k_norm[0], batch=BATCH, seq=SEQ)
    o_p = nsa_prompt(q, gl, ck, cv, kv16, nsa_q_norm[0], batch=BATCH, seq=SEQ)
    o_s = nsa_sample(q, gl, kv32, cache_cmp_kv, cache_sel_kv, cache_win_kv, page_table, cmp_pos_w, w_cmp, k_norm[0],
                     nsa_q_norm[0])
    o = jnp.concatenate([o_p, o_s.astype(bf)], axis=0)
    h = matmul_residual(o, nsa_w_o[0].astype(bf), h, 1.0, tn=512, tm=MM_ROWS)
    h = ffn_half(h, ffn_norm[1, 1], wg[1, 1], wu[1, 1], wd[1, 1])

    y_p, y_s = _split_rows(h)
    win_p = win_rows_p[:, -min(WINDOW, SEQ):]
    win_s = jnp.concatenate([cache_win_kv, win_rows_s], axis=1)[:, DEC_SEQ:]
    return (y_p, y_s, conv_p[None], conv_s[None], delta_p[None], delta_s[None],
            cmp_p, cmp_s, sel_p, sel_s, win_p, win_s)
```

```python
import functools
import math

import jax
import jax.numpy as jnp
from jax import lax
from jax.experimental import pallas as pl
from jax.experimental.pallas import tpu as pltpu

D_MODEL = 4096
BATCH = 4
SEQ = 2048
DEPTH = 2
DEC_BATCH = 8
DEC_SEQ = 8
PAST_LEN = 16384
PAGE_SIZE = 128
D_FF = 11008
EPS = 1e-6
GDN_HEADS = 16
GDN_DK = 128
GDN_DV = 256
CONV_W = 4
GDN_CHUNK = 64
GDN_QK = GDN_HEADS * GDN_DK
GDN_V = GDN_HEADS * GDN_DV
GDN_CONV_CH = 2 * GDN_QK + GDN_V
NSA_HEADS = 32
NSA_KV_HEADS = 4
HEAD_DIM = 128
GROUP = NSA_HEADS // NSA_KV_HEADS
BLOCK = 64
N_SELECT = 16
N_LOCAL = 2
WINDOW = 512
SCALE = HEAD_DIM ** -0.5
NEG = -1e30
FORCE = 1e4

N_PROMPT = BATCH * SEQ
N_SAMPLE = DEC_BATCH * DEC_SEQ
N_ROWS = N_PROMPT + N_SAMPLE

VMEM_LIMIT_BYTES = 56 * 1024 * 1024
LANES = 128
SUBLANES = 8

NORM_ROWS = 192
MM_ROWS = 1376
DOWN_ROWS = 688


def _params(*sem):
    return pltpu.CompilerParams(dimension_semantics=sem, vmem_limit_bytes=VMEM_LIMIT_BYTES)


def _rmsnorm_kernel(x_ref, g_ref, o_ref):
    x = x_ref[...]
    y = x * lax.rsqrt(jnp.mean(x * x, axis=-1, keepdims=True) + EPS)
    o_ref[...] = (y * g_ref[...]).astype(o_ref.dtype)


def rmsnorm_rows(x, g, out_dtype=jnp.bfloat16):
    n, d = x.shape
    return pl.pallas_call(
        _rmsnorm_kernel,
        grid=(n // NORM_ROWS,),
        in_specs=[pl.BlockSpec((NORM_ROWS, d), lambda i: (i, 0)),
                  pl.BlockSpec((1, d), lambda i: (0, 0))],
        out_specs=pl.BlockSpec((NORM_ROWS, d), lambda i: (i, 0)),
        out_shape=jax.ShapeDtypeStruct((n, d), out_dtype),
        compiler_params=_params("parallel"),
        name="rmsnorm_rows",
    )(x, g.reshape(1, d))


def _weight_spec(w, lead, tn):
    k = w.shape[-2]
    return pl.BlockSpec((None,) * len(lead) + (k, tn), lambda i, j: tuple(lead) + (0, j))


def _mm_kernel(x_ref, w_ref, o_ref):
    w = w_ref[...].astype(jnp.bfloat16)
    o_ref[...] = jnp.dot(x_ref[...], w, preferred_element_type=jnp.float32).astype(o_ref.dtype)


def matmul(x, w, lead=(), *, tn, n_cols=None, out_dtype=jnp.float32, tm=MM_ROWS):
    m, k = x.shape
    n = n_cols or w.shape[-1]
    return pl.pallas_call(
        _mm_kernel,
        grid=(m // tm, n // tn),
        in_specs=[pl.BlockSpec((tm, k), lambda i, j: (i, 0)), _weight_spec(w, lead, tn)],
        out_specs=pl.BlockSpec((tm, tn), lambda i, j: (i, j)),
        out_shape=jax.ShapeDtypeStruct((m, n), out_dtype),
        compiler_params=_params("parallel", "arbitrary"),
        name="matmul",
    )(x, w)


def _mm_res_kernel(x_ref, w_ref, r_ref, o_ref, *, scale):
    w = w_ref[...].astype(jnp.bfloat16)
    acc = jnp.dot(x_ref[...], w, preferred_element_type=jnp.float32)
    o_ref[...] = r_ref[...] + scale * acc


def matmul_residual(x, w, lead, res, scale, *, tn, tm):
    m, k = x.shape
    n = w.shape[-1]
    return pl.pallas_call(
        functools.partial(_mm_res_kernel, scale=scale),
        grid=(m // tm, n // tn),
        in_specs=[pl.BlockSpec((tm, k), lambda i, j: (i, 0)), _weight_spec(w, lead, tn),
                  pl.BlockSpec((tm, tn), lambda i, j: (i, j))],
        out_specs=pl.BlockSpec((tm, tn), lambda i, j: (i, j)),
        out_shape=jax.ShapeDtypeStruct((m, n), jnp.float32),
        compiler_params=_params("parallel", "arbitrary"),
        name="matmul_residual",
    )(x, w, res)


def _swiglu_up_kernel(x_ref, wg_ref, wu_ref, o_ref):
    x = x_ref[...]
    g = jnp.dot(x, wg_ref[...].astype(jnp.bfloat16), preferred_element_type=jnp.float32)
    u = jnp.dot(x, wu_ref[...].astype(jnp.bfloat16), preferred_element_type=jnp.float32)
    o_ref[...] = (g * jax.nn.sigmoid(g) * u).astype(o_ref.dtype)


def swiglu_up(x, wg, wu, lead, tn=256, tm=MM_ROWS):
    m, k = x.shape
    n = wg.shape[-1]
    return pl.pallas_call(
        _swiglu_up_kernel,
        grid=(m // tm, n // tn),
        in_specs=[pl.BlockSpec((tm, k), lambda i, j: (i, 0)), _weight_spec(wg, lead, tn), _weight_spec(wu, lead, tn)],
        out_specs=pl.BlockSpec((tm, tn), lambda i, j: (i, j)),
        out_shape=jax.ShapeDtypeStruct((m, n), jnp.bfloat16),
        compiler_params=_params("parallel", "arbitrary"),
        name="swiglu_up",
    )(x, wg, wu)


_HI = lax.Precision.HIGHEST
_NT = (((1,), (1,)), ((), ()))
_TN = (((0,), (0,)), ((), ()))


def _bdot(a, b, dims=None):
    a = a.astype(jnp.bfloat16)
    b = b.astype(jnp.bfloat16)
    if dims is None:
        return jnp.dot(a, b, preferred_element_type=jnp.float32)
    return lax.dot_general(a, b, dims, preferred_element_type=jnp.float32)


def _hdot(a, b, dims=None):
    if dims is None:
        return jnp.dot(a, b, precision=_HI, preferred_element_type=jnp.float32)
    return lax.dot_general(a, b, dims, precision=_HI, preferred_element_type=jnp.float32)


def _iota2(shape, axis):
    return lax.broadcasted_iota(jnp.int32, shape, axis)


def _sigmoid(x):
    return 1.0 / (1.0 + jnp.exp(-x))


def _softmax_rows(s):
    m = jnp.max(s, axis=-1, keepdims=True)
    e = jnp.exp(s - m)
    return e, jnp.sum(e, axis=-1, keepdims=True)


GDN_HB = 16
CONV_PAD = SUBLANES


def _gdn_kernel(uq_ref, uk_ref, uv_ref, gate_ref, ab_ref, cq_ref, ck_ref, cv_ref, s0_ref,
                wq_ref, wk_ref, wv_ref, hp_ref, onorm_ref,
                o_ref, s_ref, extq, extk, extv, *, hb, chunk, valid_rows):
    C = chunk
    c = pl.program_id(2)

    @pl.when(c == 0)
    def _():
        s_ref[...] = s0_ref[...]
        extq[0:CONV_PAD, :] = cq_ref[0]
        extk[0:CONV_PAD, :] = ck_ref[0]
        extv[0:CONV_PAD, :] = cv_ref[0]

    def conv(ext, u_ref, w_ref):
        ext[CONV_PAD:CONV_PAD + C, :] = u_ref[...]
        base = CONV_PAD - (CONV_W - 1)
        acc = ext[base:base + C, :] * w_ref[0:1, :]
        for i in range(1, CONV_W):
            acc = acc + ext[base + i:base + i + C, :] * w_ref[i:i + 1, :]
        ext[0:CONV_PAD, :] = ext[C:C + CONV_PAD, :]
        return acc * _sigmoid(acc)

    qc = conv(extq, uq_ref, wq_ref)
    kc = conv(extk, uk_ref, wk_ref)
    vc = conv(extv, uv_ref, wv_ref)

    row = _iota2((C, C), 0)
    col = _iota2((C, C), 1)
    tri_incl = row >= col
    tri_strict = row > col
    row_ok = None
    if valid_rows < C:
        row_ok = _iota2((C, 1), 0) < valid_rows

    ab = ab_ref[...]
    x = ab + hp_ref[0, 1:2, :]
    softplus = jnp.maximum(x, 0.0) + jnp.log(1.0 + jnp.exp(-jnp.abs(x)))
    gmat = -jnp.exp(hp_ref[0, 0:1, :]) * softplus
    if row_ok is not None:
        gmat = jnp.where(row_ok, gmat, 0.0)
    beta = _sigmoid(ab)
    gam = _hdot(tri_incl.astype(jnp.float32), gmat)
    eye_l = (_iota2((LANES, LANES), 0) == _iota2((LANES, LANES), 1)).astype(jnp.float32)
    gam_t = _hdot(eye_l, gam, _NT)

    heads = range(hb)
    gc = [gam[:, h:h + 1] for h in heads]
    bc = [beta[:, hb + h:hb + h + 1] for h in heads]
    glast = [gam[C - 1:C, h:h + 1] for h in heads]
    q, k, v, decay = [], [], [], []
    for h in heads:
        qh = qc[:, h * GDN_DK:(h + 1) * GDN_DK]
        kh = kc[:, h * GDN_DK:(h + 1) * GDN_DK]
        vh = vc[:, h * GDN_DV:(h + 1) * GDN_DV]
        qh = qh * lax.rsqrt(jnp.sum(qh * qh, axis=-1, keepdims=True) + EPS) * GDN_DK ** -0.5
        kh = kh * lax.rsqrt(jnp.sum(kh * kh, axis=-1, keepdims=True) + EPS)
        if row_ok is not None:
            qh = jnp.where(row_ok, qh, 0.0)
            kh = jnp.where(row_ok, kh, 0.0)
            vh = jnp.where(row_ok, vh, 0.0)
        q.append(qh)
        k.append(kh)
        v.append(vh)
        decay.append(jnp.exp(jnp.where(tri_incl, gc[h] - gam_t[h:h + 1, :], -jnp.inf)))
    kk = [_bdot(k[h], k[h], _NT) for h in heads]
    qk = [_bdot(q[h], k[h], _NT) for h in heads]
    s_old = [s_ref[0, h] for h in heads]
    kq_s = [_bdot(jnp.concatenate([k[h], q[h]], axis=0), s_old[h]) for h in heads]
    pw = [jnp.where(tri_strict, decay[h] * kk[h], 0.0) * bc[h] for h in heads]
    nil = [-pw[h] for h in heads]
    for _ in range(int(math.log2(C)) - 1):
        pw = [_bdot(pw[h], pw[h]) for h in heads]
        nil = [nil[h] + pw[h] + _bdot(nil[h], pw[h]) for h in heads]
    gt = [jnp.exp(gc[h]) for h in heads]
    rhs = [bc[h] * (v[h] - gt[h] * kq_s[h][:C]) for h in heads]
    u = [rhs[h] + _bdot(nil[h], rhs[h]) for h in heads]
    o = [gt[h] * kq_s[h][C:] + _bdot(decay[h] * qk[h], u[h]) for h in heads]
    for h in heads:
        s_ref[0, h] = jnp.exp(glast[h]) * s_old[h] + _bdot(k[h] * jnp.exp(glast[h] - gc[h]), u[h], _TN)
    for h in heads:
        on = o[h] * lax.rsqrt(jnp.mean(o[h] * o[h], axis=-1, keepdims=True) + EPS) * onorm_ref[...]
        gate = gate_ref[:, h * GDN_DV:(h + 1) * GDN_DV]
        o_ref[:, h * GDN_DV:(h + 1) * GDN_DV] = (on * gate * _sigmoid(gate)).astype(o_ref.dtype)


def gdn_mixer(proj, ab, conv_init, s0, conv_w8, hp, o_norm, *, batch, n_chunks, valid_rows=GDN_CHUNK, hb=GDN_HB):
    C = GDN_CHUNK
    ng = GDN_HEADS // hb
    rows = batch * n_chunks * C
    qw, vw = hb * GDN_DK, hb * GDN_DV

    def rowblk(off):
        return lambda b, g, c: (b * n_chunks + c, off + g)

    def fixed3(off):
        return lambda b, g, c: (b, 0, off + g)

    def wblk(off):
        return lambda b, g, c: (0, off + g)

    in_specs = [
        pl.BlockSpec((C, qw), rowblk(0)),
        pl.BlockSpec((C, qw), rowblk(ng)),
        pl.BlockSpec((C, vw), rowblk(ng)),
        pl.BlockSpec((C, vw), rowblk(2 * ng)),
        pl.BlockSpec((C, LANES), rowblk(0)),
        pl.BlockSpec((1, CONV_PAD, qw), fixed3(0)),
        pl.BlockSpec((1, CONV_PAD, qw), fixed3(ng)),
        pl.BlockSpec((1, CONV_PAD, vw), fixed3(ng)),
        pl.BlockSpec((1, hb, GDN_DK, GDN_DV), lambda b, g, c: (b, g, 0, 0)),
        pl.BlockSpec((CONV_PAD, qw), wblk(0)),
        pl.BlockSpec((CONV_PAD, qw), wblk(ng)),
        pl.BlockSpec((CONV_PAD, vw), wblk(ng)),
        pl.BlockSpec((1, CONV_PAD, LANES), lambda b, g, c: (g, 0, 0)),
        pl.BlockSpec((1, GDN_DV), lambda b, g, c: (0, 0)),
    ]
    out_specs = [
        pl.BlockSpec((C, vw), rowblk(0)),
        pl.BlockSpec((1, hb, GDN_DK, GDN_DV), lambda b, g, c: (b, g, 0, 0)),
    ]
    return pl.pallas_call(
        functools.partial(_gdn_kernel, hb=hb, chunk=C, valid_rows=valid_rows),
        grid=(batch, ng, n_chunks),
        in_specs=in_specs,
        out_specs=out_specs,
        out_shape=[jax.ShapeDtypeStruct((rows, GDN_V), jnp.bfloat16),
                   jax.ShapeDtypeStruct((batch, GDN_HEADS, GDN_DK, GDN_DV), jnp.float32)],
        scratch_shapes=[pltpu.VMEM((CONV_PAD + C, qw), jnp.float32),
                        pltpu.VMEM((CONV_PAD + C, qw), jnp.float32),
                        pltpu.VMEM((CONV_PAD + C, vw), jnp.float32)],
        compiler_params=_params("parallel", "parallel", "arbitrary"),
        name="gdn_mixer",
    )(proj, proj, proj, proj, ab, conv_init, conv_init, conv_init, s0,
      conv_w8, conv_w8, conv_w8, hp, o_norm.reshape(1, GDN_DV))


def gdn_group_cols(w_ab, hb=GDN_HB):
    d = w_ab.shape[0]
    ng = GDN_HEADS // hb
    a = w_ab[:, :GDN_HEADS].reshape(d, ng, hb)
    b = w_ab[:, GDN_HEADS:].reshape(d, ng, hb)
    blk = jnp.concatenate([a, b, jnp.zeros((d, ng, LANES - 2 * hb), w_ab.dtype)], axis=-1)
    return blk.reshape(d, ng * LANES)


def gdn_head_params(a_log, dt_bias, hb=GDN_HB):
    ng = GDN_HEADS // hb
    rows = jnp.stack([a_log.reshape(ng, hb), dt_bias.reshape(ng, hb)], axis=1)
    return jnp.pad(rows, ((0, 0), (0, CONV_PAD - 2), (0, LANES - hb)))


KV_W = 3 * 2 * NSA_KV_HEADS * HEAD_DIM
KV_HALF = NSA_KV_HEADS * HEAD_DIM
N_KV_COLS = KV_W // HEAD_DIM
NB_PAD = LANES
_NORMED_KV_COLS = tuple(range(2 * NSA_KV_HEADS, 3 * NSA_KV_HEADS)) + tuple(range(4 * NSA_KV_HEADS, 5 * NSA_KV_HEADS))


def _kv_finish_kernel(kv_ref, gain_ref, o32_ref, o16_ref):
    for j in range(N_KV_COLS):
        sl = slice(j * HEAD_DIM, (j + 1) * HEAD_DIM)
        x = kv_ref[:, sl]
        if j in _NORMED_KV_COLS:
            x = x * lax.rsqrt(jnp.mean(x * x, axis=-1, keepdims=True) + EPS) * gain_ref[0:1, sl]
        o32_ref[:, sl] = x
        o16_ref[:, sl] = x.astype(o16_ref.dtype)


def kv_finish(kv, k_norm):
    n = kv.shape[0]
    ones = jnp.ones((KV_HALF,), jnp.float32)
    gain = jnp.concatenate([ones, ones, jnp.tile(k_norm[1], NSA_KV_HEADS), ones,
                            jnp.tile(k_norm[2], NSA_KV_HEADS), ones])
    gain = jnp.broadcast_to(gain[None], (SUBLANES, KV_W))
    return pl.pallas_call(
        _kv_finish_kernel,
        grid=(n // NORM_ROWS,),
        in_specs=[pl.BlockSpec((NORM_ROWS, KV_W), lambda i: (i, 0)),
                  pl.BlockSpec((SUBLANES, KV_W), lambda i: (0, 0))],
        out_specs=[pl.BlockSpec((NORM_ROWS, KV_W), lambda i: (i, 0)),
                   pl.BlockSpec((NORM_ROWS, KV_W), lambda i: (i, 0))],
        out_shape=[jax.ShapeDtypeStruct((n, KV_W), jnp.float32),
                   jax.ShapeDtypeStruct((n, KV_W), jnp.bfloat16)],
        compiler_params=_params("parallel"),
        name="kv_finish",
    )(kv, gain)


def _compress_prompt_kernel(rows_ref, pw_ref, wcmp_ref, kn_ref, ck_ref, cv_ref, *, nb):
    pooled_k = _hdot(pw_ref[0], rows_ref[:, 0:KV_HALF])
    pooled_v = _hdot(pw_ref[1], rows_ref[:, KV_HALF:2 * KV_HALF])
    ck_ref[...] = jnp.zeros_like(ck_ref)
    cv_ref[...] = jnp.zeros_like(cv_ref)
    for g in range(NSA_KV_HEADS):
        sl = slice(g * HEAD_DIM, (g + 1) * HEAD_DIM)
        k = _hdot(pooled_k[:, sl], wcmp_ref[0, g])
        k = k * lax.rsqrt(jnp.mean(k * k, axis=-1, keepdims=True) + EPS) * kn_ref[...]
        v = _hdot(pooled_v[:, sl], wcmp_ref[1, g])
        ck_ref[0, 0:nb, sl] = k.astype(ck_ref.dtype)
        cv_ref[0, 0:nb, sl] = v.astype(cv_ref.dtype)


def compress_prompt(kv32, cmp_pos_w, w_cmp, kn_cmp, *, batch, seq):
    nb = seq // BLOCK
    pw = jnp.einsum('nm,jc->cnmj', jnp.eye(nb, dtype=jnp.float32), cmp_pos_w).reshape(2, nb, seq)
    shape = jax.ShapeDtypeStruct((batch, NB_PAD, KV_HALF), jnp.bfloat16)
    return pl.pallas_call(
        functools.partial(_compress_prompt_kernel, nb=nb),
        grid=(batch,),
        in_specs=[pl.BlockSpec((seq, 2 * KV_HALF), lambda b: (b, 0)),
                  pl.BlockSpec((2, nb, seq), lambda b: (0, 0, 0)),
                  pl.BlockSpec((2, NSA_KV_HEADS, HEAD_DIM, HEAD_DIM), lambda b: (0, 0, 0, 0)),
                  pl.BlockSpec((1, HEAD_DIM), lambda b: (0, 0))],
        out_specs=[pl.BlockSpec((1, NB_PAD, KV_HALF), lambda b: (b, 0, 0)),
                   pl.BlockSpec((1, NB_PAD, KV_HALF), lambda b: (b, 0, 0))],
        out_shape=[shape, shape],
        compiler_params=_params("parallel"),
        name="compress_prompt",
    )(kv32, pw, w_cmp, kn_cmp.reshape(1, HEAD_DIM))


NSA_TQ = 256


def _select_mask(imp, qpos, nb):
    lane = _iota2(imp.shape, 1)
    cur = qpos // BLOCK
    causal = lane <= cur
    forced = (lane == 0) | (causal & (lane > cur - N_LOCAL))
    score = jnp.where(forced, FORCE, jnp.where(causal, imp, -1.0))
    score = jnp.where(lane < nb, score, -2.0)
    rank = jnp.zeros(imp.shape, jnp.float32)
    for j in range(nb):
        cj = score[:, j:j + 1]
        rank = rank + jnp.where((cj > score) | ((cj == score) & (lane > j)), 1.0, 0.0)
    return (rank < float(min(N_SELECT, nb))) & (lane < nb)


def _nsa_prompt_kernel(q_ref, gl_ref, ck_ref, cv_ref, ksel_ref, vsel_ref, kwin_ref, vwin_ref,
                       qn_ref, sl_ref, ex_ref, o_ref, *, tq, seq):
    nb = seq // BLOCK
    t0 = pl.program_id(2) * tq
    qpos = t0 + _iota2((tq, 1), 0)
    gates = _sigmoid(gl_ref[...])
    qs = []
    for r in range(GROUP):
        x = q_ref[:, r * HEAD_DIM:(r + 1) * HEAD_DIM]
        x = x * lax.rsqrt(jnp.mean(x * x, axis=-1, keepdims=True) + EPS) * qn_ref[...]
        qs.append((x * SCALE).astype(jnp.bfloat16))
    slopes = [sl_ref[0, r:r + 1, 0:1] for r in range(GROUP)]

    lane = _iota2((1, NB_PAD), 1)
    blk_end = (lane + 1) * BLOCK - 1
    valid_c = (qpos >= blk_end) & (lane < nb)
    off_c = jnp.where(lane < nb, NEG, -jnp.inf)
    any_c = (qpos >= BLOCK - 1).astype(jnp.float32)
    rel_c = (blk_end - t0).astype(jnp.float32)
    ck = ck_ref[0]
    cv = cv_ref[0]
    imp = jnp.zeros((tq, NB_PAD), jnp.float32)
    outs = []
    for r in range(GROUP):
        s = _bdot(qs[r], ck, _NT) + slopes[r] * rel_c
        e, l = _softmax_rows(jnp.where(valid_c, s, off_c))
        p = e / l * any_c
        imp = imp + p
        outs.append(gates[:, r:r + 1] * _bdot(p, cv))

    sel = _select_mask(imp, qpos, nb)
    kpos = _iota2((1, seq), 1)
    allowed = (_bdot(sel.astype(jnp.bfloat16), ex_ref[...]) > 0.5) & (kpos <= qpos)
    mask_s = jnp.where(allowed, 0.0, NEG)
    rel_s = (kpos - t0).astype(jnp.float32)
    ksel = ksel_ref[...]
    vsel = vsel_ref[...]
    for r in range(GROUP):
        s = _bdot(qs[r], ksel, _NT) + slopes[r] * rel_s + mask_s
        e, l = _softmax_rows(s)
        outs[r] = outs[r] + gates[:, GROUP + r:GROUP + r + 1] * (_bdot(e, vsel) / l)

    nw = WINDOW + tq
    start = pl.multiple_of(jnp.maximum(t0 - WINDOW, 0), LANES)
    kwin = kwin_ref[pl.ds(start, nw), :]
    vwin = vwin_ref[pl.ds(start, nw), :]
    kpos_w = start + _iota2((1, nw), 1)
    dist = qpos - kpos_w
    mask_w = jnp.where((dist >= 0) & (dist < WINDOW), 0.0, NEG)
    rel_w = (kpos_w - t0).astype(jnp.float32)
    for r in range(GROUP):
        s = _bdot(qs[r], kwin, _NT) + slopes[r] * rel_w + mask_w
        e, l = _softmax_rows(s)
        o = outs[r] + gates[:, 2 * GROUP + r:2 * GROUP + r + 1] * (_bdot(e, vwin) / l)
        o_ref[:, r * HEAD_DIM:(r + 1) * HEAD_DIM] = o.astype(o_ref.dtype)


def nsa_slopes():
    h = jnp.arange(1, NSA_HEADS + 1, dtype=jnp.float32)
    s = (2.0 ** (-8.0 * h / NSA_HEADS)).reshape(NSA_KV_HEADS, GROUP, 1)
    return jnp.broadcast_to(s, (NSA_KV_HEADS, GROUP, LANES))


def nsa_gate_cols(w_g):
    d = w_g.shape[0]
    w = w_g.reshape(d, 3, NSA_KV_HEADS, GROUP).transpose(0, 2, 1, 3).reshape(d, NSA_KV_HEADS, 3 * GROUP)
    return jnp.pad(w, ((0, 0), (0, 0), (0, LANES - 3 * GROUP))).reshape(d, NSA_KV_HEADS * LANES)


def nsa_prompt(q, gl, ck, cv, kv16, q_norm, *, batch, seq, tq=NSA_TQ):
    nt = seq // tq
    expand = (jnp.arange(NB_PAD)[:, None] == (jnp.arange(seq)[None, :] // BLOCK)).astype(jnp.bfloat16)
    kvw = 2 * NSA_KV_HEADS

    def kvspec(col0):
        return pl.BlockSpec((seq, HEAD_DIM), lambda b, g, i: (b, col0 + g))

    return pl.pallas_call(
        functools.partial(_nsa_prompt_kernel, tq=tq, seq=seq),
        grid=(batch, NSA_KV_HEADS, nt),
        in_specs=[pl.BlockSpec((tq, GROUP * HEAD_DIM), lambda b, g, i: (b * nt + i, g)),
                  pl.BlockSpec((tq, LANES), lambda b, g, i: (b * nt + i, g)),
                  pl.BlockSpec((1, NB_PAD, HEAD_DIM), lambda b, g, i: (b, 0, g)),
                  pl.BlockSpec((1, NB_PAD, HEAD_DIM), lambda b, g, i: (b, 0, g)),
                  kvspec(kvw), kvspec(kvw + NSA_KV_HEADS), kvspec(2 * kvw), kvspec(2 * kvw + NSA_KV_HEADS),
                  pl.BlockSpec((1, HEAD_DIM), lambda b, g, i: (0, 0)),
                  pl.BlockSpec((1, GROUP, LANES), lambda b, g, i: (g, 0, 0)),
                  pl.BlockSpec((NB_PAD, seq), lambda b, g, i: (0, 0))],
        out_specs=pl.BlockSpec((tq, GROUP * HEAD_DIM), lambda b, g, i: (b * nt + i, g)),
        out_shape=jax.ShapeDtypeStruct((batch * seq, NSA_HEADS * HEAD_DIM), jnp.bfloat16),
        compiler_params=_params("parallel", "parallel", "arbitrary"),
        name="nsa_prompt",
    )(q, gl, ck, cv, kv16, kv16, kv16, kv16, q_norm.reshape(1, HEAD_DIM), nsa_slopes(), expand)


KV_SLOTS = 2 * NSA_KV_HEADS
N_PAGES = PAST_LEN // PAGE_SIZE
PAGES_PER_STEP = 4
PAGE_ROWS = PAGE_SIZE * KV_SLOTS
BLOCK_ROWS = BLOCK * KV_SLOTS
BLOCKS_PER_PAGE = PAGE_SIZE // BLOCK
N_PAST_BLOCKS = PAST_LEN // BLOCK
N_SAMPLE_BLOCKS = -(-(PAST_LEN + DEC_SEQ) // BLOCK)
SEL_LANES = -(-N_SAMPLE_BLOCKS // LANES) * LANES
QROWS = GROUP * DEC_SEQ


def _slot_rows(ref, lead, slot, n):
    return ref[lead, pl.ds(slot, n, stride=KV_SLOTS), :]


def _page_specs():
    def spec(j):
        return pl.BlockSpec((1, PAGE_ROWS, HEAD_DIM),
                            lambda b, i, pt: (pt[b * N_PAGES + PAGES_PER_STEP * i + j], 0, 0))
    return [spec(j) for j in range(PAGES_PER_STEP)]


def _paged(cache):
    return [cache.reshape(-1, PAGE_ROWS, HEAD_DIM)] * PAGES_PER_STEP


def _cmp_pool_kernel(pt_ref, p0, p1, p2, p3, w_ref, o_ref):
    i = pl.program_id(1)
    tiles = []
    for p in (p0, p1, p2, p3):
        prod = p[0] * w_ref[...]
        for h in range(BLOCKS_PER_PAGE):
            blk = prod[h * BLOCK_ROWS:(h + 1) * BLOCK_ROWS].reshape(BLOCK, KV_SLOTS, HEAD_DIM)
            tiles.append(jnp.sum(blk, axis=0))
    n = PAGES_PER_STEP * BLOCKS_PER_PAGE * KV_SLOTS
    o_ref[0, pl.ds(pl.multiple_of(i * n, n), n), :] = jnp.concatenate(tiles, axis=0)


def cmp_pool_pages(cache, page_table, cmp_pos_w):
    w = jnp.repeat(jnp.tile(cmp_pos_w, (BLOCKS_PER_PAGE, 1)), NSA_KV_HEADS, axis=1)
    wt = jnp.broadcast_to(w.reshape(PAGE_ROWS, 1), (PAGE_ROWS, HEAD_DIM))
    return pl.pallas_call(
        _cmp_pool_kernel,
        grid_spec=pltpu.PrefetchScalarGridSpec(
            num_scalar_prefetch=1,
            grid=(DEC_BATCH, N_PAGES // PAGES_PER_STEP),
            in_specs=_page_specs() + [pl.BlockSpec((PAGE_ROWS, HEAD_DIM), lambda b, i, pt: (0, 0))],
            out_specs=pl.BlockSpec((1, N_PAST_BLOCKS * KV_SLOTS, HEAD_DIM), lambda b, i, pt: (b, 0, 0))),
        out_shape=jax.ShapeDtypeStruct((DEC_BATCH, N_PAST_BLOCKS * KV_SLOTS, HEAD_DIM), jnp.float32),
        compiler_params=_params("parallel", "arbitrary"),
        name="cmp_pool_pages",
    )(page_table.reshape(-1), *_paged(cache), wt)


def _sample_queries(q_ref, qn_ref, g):
    parts = []
    for r in range(GROUP):
        c0 = (g * GROUP + r) * HEAD_DIM
        x = q_ref[:, c0:c0 + HEAD_DIM]
        x = x * lax.rsqrt(jnp.mean(x * x, axis=-1, keepdims=True) + EPS) * qn_ref[...]
        parts.append(x * SCALE)
    return jnp.concatenate(parts, axis=0).astype(jnp.bfloat16)


def _head_major_col(x, lane0):
    return jnp.concatenate([x[:, lane0 + r:lane0 + r + 1] for r in range(GROUP)], axis=0)


def _slope_col(sl_ref, g):
    return jnp.concatenate([jnp.broadcast_to(sl_ref[g, r:r + 1, 0:1], (DEC_SEQ, 1)) for r in range(GROUP)], axis=0)


def _sample_select_kernel(pooled_ref, wcmp_ref, kn_ref, q_ref, gl_ref, qn_ref, sl_ref, wcache_ref, new_ref, ex_ref,
                          ocw_ref, mask_ref, newmask_ref):
    srow = _iota2((QROWS, 1), 0) % DEC_SEQ
    qpos = PAST_LEN + srow
    gl = gl_ref[...]
    nbp = N_PAST_BLOCKS
    lane_c = _iota2((1, nbp), 1)
    blk_end = (lane_c + 1) * BLOCK - 1
    rel_c = (blk_end - PAST_LEN).astype(jnp.float32)
    nw = wcache_ref.shape[1] // KV_SLOTS
    nwk = nw + LANES
    jw = _iota2((1, nwk), 1)
    kpos_w = jnp.where(jw < nw, PAST_LEN - nw + jw, PAST_LEN + jw - nw)
    dist_w = qpos - kpos_w
    ok_w = (dist_w >= 0) & (dist_w < WINDOW) & (jw < nw + DEC_SEQ)
    mask_w = jnp.where(ok_w, 0.0, NEG)
    rel_w = (kpos_w - PAST_LEN).astype(jnp.float32)
    pad_rows = jnp.zeros((LANES - DEC_SEQ, HEAD_DIM), jnp.float32)
    for g in range(NSA_KV_HEADS):
        ck = _hdot(_slot_rows(pooled_ref, 0, g, nbp), wcmp_ref[0, g])
        ck = ck * lax.rsqrt(jnp.mean(ck * ck, axis=-1, keepdims=True) + EPS) * kn_ref[...]
        cv = _hdot(_slot_rows(pooled_ref, 0, NSA_KV_HEADS + g, nbp), wcmp_ref[1, g])
        qg = _sample_queries(q_ref, qn_ref, g)
        slope = _slope_col(sl_ref, g)
        gates = _sigmoid(gl[:, g * LANES:(g + 1) * LANES])
        s = _bdot(qg, ck, _NT) + slope * rel_c
        e, l = _softmax_rows(jnp.where(qpos >= blk_end, s, NEG))
        p = e / l
        o = _head_major_col(gates, 0) * _bdot(p, cv)
        imp = p[0:DEC_SEQ]
        for r in range(1, GROUP):
            imp = imp + p[r * DEC_SEQ:(r + 1) * DEC_SEQ]
        imp = jnp.concatenate([imp, jnp.zeros((DEC_SEQ, SEL_LANES - nbp), jnp.float32)], axis=1)
        sel = _select_mask(imp, PAST_LEN + _iota2((DEC_SEQ, 1), 0), N_SAMPLE_BLOCKS).astype(jnp.bfloat16)
        for half in range(nbp // LANES):
            keys = _bdot(sel[:, half * LANES:(half + 1) * LANES], ex_ref[...])
            mask_ref[0, g * DEC_SEQ:(g + 1) * DEC_SEQ, half * LANES * BLOCK:(half + 1) * LANES * BLOCK] = keys
        newmask_ref[0, g * DEC_SEQ:(g + 1) * DEC_SEQ, :] = jnp.broadcast_to(
            sel[:, nbp:nbp + 1].astype(jnp.float32), (DEC_SEQ, LANES))
        c0 = 4 * KV_HALF + g * HEAD_DIM
        kw = jnp.concatenate([_slot_rows(wcache_ref, 0, g, nw), new_ref[:, c0:c0 + HEAD_DIM], pad_rows], axis=0)
        vw = jnp.concatenate([_slot_rows(wcache_ref, 0, NSA_KV_HEADS + g, nw),
                              new_ref[:, c0 + KV_HALF:c0 + KV_HALF + HEAD_DIM], pad_rows], axis=0)
        s = _bdot(qg, kw, _NT) + slope * rel_w + mask_w
        e, l = _softmax_rows(s)
        o = o + _head_major_col(gates, 2 * GROUP) * (_bdot(e, vw) / l)
        for r in range(GROUP):
            c0 = (g * GROUP + r) * HEAD_DIM
            ocw_ref[:, c0:c0 + HEAD_DIM] = o[r * DEC_SEQ:(r + 1) * DEC_SEQ]


def nsa_sample_select(pooled, w_cmp, kn_cmp, q, gl, q_norm, cache_win, kv32):
    expand = (jnp.arange(LANES)[:, None] == (jnp.arange(LANES * BLOCK)[None, :] // BLOCK)).astype(jnp.bfloat16)
    row0 = N_PROMPT // DEC_SEQ
    nw = cache_win.shape[1]
    return pl.pallas_call(
        _sample_select_kernel,
        grid=(DEC_BATCH,),
        in_specs=[pl.BlockSpec((1, N_PAST_BLOCKS * KV_SLOTS, HEAD_DIM), lambda b: (b, 0, 0)),
                  pl.BlockSpec((2, NSA_KV_HEADS, HEAD_DIM, HEAD_DIM), lambda b: (0, 0, 0, 0)),
                  pl.BlockSpec((1, HEAD_DIM), lambda b: (0, 0)),
                  pl.BlockSpec((DEC_SEQ, NSA_HEADS * HEAD_DIM), lambda b: (row0 + b, 0)),
                  pl.BlockSpec((DEC_SEQ, NSA_KV_HEADS * LANES), lambda b: (row0 + b, 0)),
                  pl.BlockSpec((1, HEAD_DIM), lambda b: (0, 0)),
                  pl.BlockSpec((NSA_KV_HEADS, GROUP, LANES), lambda b: (0, 0, 0)),
                  pl.BlockSpec((1, nw * KV_SLOTS, HEAD_DIM), lambda b: (b, 0, 0)),
                  pl.BlockSpec((DEC_SEQ, KV_W), lambda b: (row0 + b, 0)),
                  pl.BlockSpec((LANES, LANES * BLOCK), lambda b: (0, 0))],
        out_specs=[pl.BlockSpec((DEC_SEQ, NSA_HEADS * HEAD_DIM), lambda b: (b, 0)),
                   pl.BlockSpec((1, NSA_KV_HEADS * DEC_SEQ, PAST_LEN), lambda b: (b, 0, 0)),
                   pl.BlockSpec((1, NSA_KV_HEADS * DEC_SEQ, LANES), lambda b: (b, 0, 0))],
        out_shape=[jax.ShapeDtypeStruct((N_SAMPLE, NSA_HEADS * HEAD_DIM), jnp.float32),
                   jax.ShapeDtypeStruct((DEC_BATCH, NSA_KV_HEADS * DEC_SEQ, PAST_LEN), jnp.float32),
                   jax.ShapeDtypeStruct((DEC_BATCH, NSA_KV_HEADS * DEC_SEQ, LANES), jnp.float32)],
        compiler_params=_params("parallel"),
        name="nsa_sample_select",
    )(pooled, w_cmp, kn_cmp.reshape(1, HEAD_DIM), q, gl, q_norm.reshape(1, HEAD_DIM), nsa_slopes(),
      cache_win.reshape(DEC_BATCH, nw * KV_SLOTS, HEAD_DIM), kv32, expand)


def _sample_sel_kernel(pt_ref, p0, p1, p2, p3, mask_ref, newmask_ref, q_ref, gl_ref, qn_ref, sl_ref, new_ref, ocw_ref,
                       o_ref, qs, m_sc, l_sc, acc_sc):
    i = pl.program_id(1)
    groups = range(NSA_KV_HEADS)

    @pl.when(i == 0)
    def _():
        for g in groups:
            qs[g] = _sample_queries(q_ref, qn_ref, g)
        m_sc[...] = jnp.full(m_sc.shape, NEG, jnp.float32)
        l_sc[...] = jnp.zeros(l_sc.shape, jnp.float32)
        acc_sc[...] = jnp.zeros(acc_sc.shape, jnp.float32)

    def accumulate(keys, vals, rel, keep8):
        keep = [jnp.concatenate([keep8[g]] * GROUP, axis=0) for g in groups]
        s = [_bdot(qs[g], keys[g], _NT) + _slope_col(sl_ref, g) * rel + jnp.where(keep[g] > 0.5, 0.0, NEG)
             for g in groups]
        m_old = [m_sc[g] for g in groups]
        m_new = [jnp.maximum(m_old[g], jnp.max(s[g], axis=-1, keepdims=True)) for g in groups]
        p = [jnp.exp(s[g] - m_new[g]) * keep[g] for g in groups]
        pv = [_bdot(p[g], vals[g]) for g in groups]
        for g in groups:
            alpha = jnp.exp(m_old[g] - m_new[g])
            l_sc[g] = alpha * l_sc[g] + jnp.sum(p[g], axis=-1, keepdims=True)
            acc_sc[g] = alpha * acc_sc[g] + pv[g]
            m_sc[g] = m_new[g]

    lane = _iota2((1, PAGE_SIZE), 1)
    for j, page in enumerate((p0, p1, p2, p3)):
        keep_all = mask_ref[0, :, j * PAGE_SIZE:(j + 1) * PAGE_SIZE]

        @pl.when(jnp.max(keep_all) > 0.5)
        def _(j=j, page=page, keep_all=keep_all):
            rel = ((i * PAGES_PER_STEP + j) * PAGE_SIZE - PAST_LEN + lane).astype(jnp.float32)
            accumulate([_slot_rows(page, 0, g, PAGE_SIZE) for g in groups],
                       [_slot_rows(page, 0, NSA_KV_HEADS + g, PAGE_SIZE) for g in groups],
                       rel, [keep_all[g * DEC_SEQ:(g + 1) * DEC_SEQ] for g in groups])

    @pl.when(i == pl.num_programs(1) - 1)
    def _():
        gl = gl_ref[...]
        pad_rows = jnp.zeros((PAGE_SIZE - DEC_SEQ, HEAD_DIM), jnp.float32)
        causal = ((lane <= _iota2((DEC_SEQ, 1), 0)) & (lane < DEC_SEQ)).astype(jnp.float32)
        c0 = 2 * KV_HALF
        accumulate([jnp.concatenate([new_ref[:, c0 + g * HEAD_DIM:c0 + (g + 1) * HEAD_DIM], pad_rows], axis=0)
                    for g in groups],
                   [jnp.concatenate([new_ref[:, c0 + KV_HALF + g * HEAD_DIM:c0 + KV_HALF + (g + 1) * HEAD_DIM], pad_rows],
                                    axis=0) for g in groups],
                   lane.astype(jnp.float32),
                   [newmask_ref[0, g * DEC_SEQ:(g + 1) * DEC_SEQ, :] * causal for g in groups])
        for g in groups:
            gates = _sigmoid(gl[:, g * LANES:(g + 1) * LANES])
            o = _head_major_col(gates, GROUP) * (acc_sc[g] / l_sc[g])
            for r in range(GROUP):
                c0 = (g * GROUP + r) * HEAD_DIM
                o_ref[:, c0:c0 + HEAD_DIM] = ocw_ref[:, c0:c0 + HEAD_DIM] + o[r * DEC_SEQ:(r + 1) * DEC_SEQ]


def nsa_sample_sel(cache, page_table, mask, newmask, q, gl, q_norm, kv32, ocw):
    row0 = N_PROMPT // DEC_SEQ
    nrow = NSA_KV_HEADS * DEC_SEQ
    step_keys = PAGES_PER_STEP * PAGE_SIZE
    return pl.pallas_call(
        _sample_sel_kernel,
        grid_spec=pltpu.PrefetchScalarGridSpec(
            num_scalar_prefetch=1,
            grid=(DEC_BATCH, N_PAGES // PAGES_PER_STEP),
            in_specs=_page_specs() + [
                pl.BlockSpec((1, nrow, step_keys), lambda b, i, pt: (b, 0, i)),
                pl.BlockSpec((1, nrow, LANES), lambda b, i, pt: (b, 0, 0)),
                pl.BlockSpec((DEC_SEQ, NSA_HEADS * HEAD_DIM), lambda b, i, pt: (row0 + b, 0)),
                pl.BlockSpec((DEC_SEQ, NSA_KV_HEADS * LANES), lambda b, i, pt: (row0 + b, 0)),
                pl.BlockSpec((1, HEAD_DIM), lambda b, i, pt: (0, 0)),
                pl.BlockSpec((NSA_KV_HEADS, GROUP, LANES), lambda b, i, pt: (0, 0, 0)),
                pl.BlockSpec((DEC_SEQ, KV_W), lambda b, i, pt: (row0 + b, 0)),
                pl.BlockSpec((DEC_SEQ, NSA_HEADS * HEAD_DIM), lambda b, i, pt: (b, 0))],
            out_specs=pl.BlockSpec((DEC_SEQ, NSA_HEADS * HEAD_DIM), lambda b, i, pt: (b, 0)),
            scratch_shapes=[pltpu.VMEM((NSA_KV_HEADS, QROWS, HEAD_DIM), jnp.bfloat16),
                            pltpu.VMEM((NSA_KV_HEADS, QROWS, 1), jnp.float32),
                            pltpu.VMEM((NSA_KV_HEADS, QROWS, 1), jnp.float32),
                            pltpu.VMEM((NSA_KV_HEADS, QROWS, HEAD_DIM), jnp.float32)]),
        out_shape=jax.ShapeDtypeStruct((N_SAMPLE, NSA_HEADS * HEAD_DIM), jnp.float32),
        compiler_params=_params("parallel", "arbitrary"),
        name="nsa_sample_sel",
    )(page_table.reshape(-1), *_paged(cache), mask, newmask,
      q, gl, q_norm.reshape(1, HEAD_DIM), nsa_slopes(), kv32, ocw)


def nsa_sample(q, gl, kv32, cache_cmp_kv, cache_sel_kv, cache_win_kv, page_table, cmp_pos_w, w_cmp, kn_cmp, q_norm):
    pooled = cmp_pool_pages(cache_cmp_kv, page_table, cmp_pos_w)
    ocw, mask, newmask = nsa_sample_select(pooled, w_cmp, kn_cmp, q, gl, q_norm, cache_win_kv, kv32)
    return nsa_sample_sel(cache_sel_kv, page_table, mask, newmask, q, gl, q_norm, kv32, ocw)


def _split_rows(x):
    return (x[:N_PROMPT].reshape((BATCH, SEQ) + x.shape[1:]),
            x[N_PROMPT:].reshape((DEC_BATCH, DEC_SEQ) + x.shape[1:]))


def kernel(x_prompt, x_sample, state_conv, state_delta, cache_cmp_kv, cache_sel_kv, cache_win_kv, page_table,
           ffn_norm, ffn_w_gate, ffn_w_up, ffn_w_down, mix_norm,
           gdn_w_in, gdn_conv_w, gdn_a_log, gdn_dt_bias, gdn_o_norm, gdn_w_out,
           kv_norm, w_kv, cmp_pos_w, w_cmp, k_norm, nsa_w_q, nsa_q_norm, nsa_w_o):
    bf = jnp.bfloat16
    h = jnp.concatenate([x_prompt.reshape(N_PROMPT, D_MODEL), x_sample.reshape(N_SAMPLE, D_MODEL)], axis=0)
    n_main = GDN_CONV_CH + GDN_V
    n_q = NSA_HEADS * HEAD_DIM
    hist = CONV_W - 1

    def ffn_half(h, layer, i):
        xn = rmsnorm_rows(h, ffn_norm[layer, i])
        act = swiglu_up(xn, ffn_w_gate, ffn_w_up, (layer, i))
        return matmul_residual(act, ffn_w_down[layer, i].astype(bf), (), h, 0.5, tn=256, tm=DOWN_ROWS)

    h = ffn_half(h, 0, 0)
    xn = rmsnorm_rows(h, mix_norm[0])
    proj = matmul(xn, gdn_w_in, (0,), n_cols=n_main, tn=512)
    ab = matmul(xn, gdn_group_cols(gdn_w_in[0, :, n_main:]), tn=LANES)
    conv_w8 = jnp.pad(gdn_conv_w[0], ((0, CONV_PAD - CONV_W), (0, 0)))
    hp = gdn_head_params(gdn_a_log[0], gdn_dt_bias[0])

    def pad_sample(x):
        x = x[N_PROMPT:].reshape(DEC_BATCH, DEC_SEQ, x.shape[1])
        return jnp.pad(x, ((0, 0), (0, GDN_CHUNK - DEC_SEQ), (0, 0))).reshape(DEC_BATCH * GDN_CHUNK, x.shape[2])

    o_p, delta_p = gdn_mixer(proj, ab, jnp.zeros((BATCH, CONV_PAD, GDN_CONV_CH), jnp.float32),
                             jnp.zeros((BATCH, GDN_HEADS, GDN_DK, GDN_DV), jnp.float32),
                             conv_w8, hp, gdn_o_norm[0], batch=BATCH, n_chunks=SEQ // GDN_CHUNK)
    o_s, delta_s = gdn_mixer(pad_sample(proj), pad_sample(ab),
                             jnp.pad(state_conv[0], ((0, 0), (CONV_PAD - hist, 0), (0, 0))), state_delta[0],
                             conv_w8, hp, gdn_o_norm[0], batch=DEC_BATCH, n_chunks=1, valid_rows=DEC_SEQ)
    o_s = o_s.reshape(DEC_BATCH, GDN_CHUNK, GDN_V)[:, :DEC_SEQ].reshape(N_SAMPLE, GDN_V)
    conv_p = jnp.stack([lax.slice(proj, ((b + 1) * SEQ - hist, 0), ((b + 1) * SEQ, GDN_CONV_CH)) for b in range(BATCH)])
    u_s = lax.slice(proj, (N_PROMPT, 0), (N_ROWS, GDN_CONV_CH)).reshape(DEC_BATCH, DEC_SEQ, GDN_CONV_CH)
    conv_s = jnp.concatenate([state_conv[0], u_s], axis=1)[:, DEC_SEQ:]
    h = matmul_residual(jnp.concatenate([o_p, o_s], axis=0), gdn_w_out, (0,), h, 1.0, tn=256, tm=MM_ROWS)
    h = ffn_half(h, 0, 1)

    kv32, kv16 = kv_finish(matmul(rmsnorm_rows(h, kv_norm), w_kv, tn=512), k_norm)
    kv5 = kv32.reshape(N_ROWS, 3, 2, NSA_KV_HEADS, HEAD_DIM)
    cmp_p, cmp_s = _split_rows(kv5[:, 0])
    sel_p, sel_s = _split_rows(kv5[:, 1])
    win_rows_p, win_rows_s = _split_rows(kv5[:, 2])

    h = ffn_half(h, 1, 0)
    xn = rmsnorm_rows(h, mix_norm[1])
    q = matmul(xn, nsa_w_q, (0,), n_cols=n_q, tn=512)
    gl = matmul(xn, nsa_gate_cols(nsa_w_q[0, :, n_q:]), tn=LANES)
    ck, cv = compress_prompt(kv32, cmp_pos_w, w_cmp, k_norm[0], batch=BATCH, seq=SEQ)
    o_p = nsa_prompt(q, gl, ck, cv, kv16, nsa_q_norm[0], batch=BATCH, seq=SEQ)
    o_s = nsa_sample(q, gl, kv32, cache_cmp_kv, cache_sel_kv, cache_win_kv, page_table, cmp_pos_w, w_cmp, k_norm[0],
                     nsa_q_norm[0])
    h = matmul_residual(jnp.concatenate([o_p, o_s.astype(bf)], axis=0), nsa_w_o, (0,), h, 1.0, tn=256, tm=MM_ROWS)
    h = ffn_half(h, 1, 1)

    y_p, y_s = _split_rows(h)
    win_p = win_rows_p[:, -min(WINDOW, SEQ):]
    win_s = jnp.concatenate([cache_win_kv, win_rows_s], axis=1)[:, DEC_SEQ:]
    return (y_p, y_s, conv_p[None], conv_s[None], delta_p[None], delta_s[None],
            cmp_p, cmp_s, sel_p, sel_s, win_p, win_s)
```

```python
import functools
import math

import jax
import jax.numpy as jnp
from jax import lax
from jax.experimental import pallas as pl
from jax.experimental.pallas import tpu as pltpu

D_MODEL = 4096
BATCH = 4
SEQ = 2048
DEPTH = 2
DEC_BATCH = 8
DEC_SEQ = 8
PAST_LEN = 16384
PAGE_SIZE = 128
D_FF = 11008
EPS = 1e-6
GDN_HEADS = 16
GDN_DK = 128
GDN_DV = 256
CONV_W = 4
GDN_CHUNK = 64
GDN_QK = GDN_HEADS * GDN_DK
GDN_V = GDN_HEADS * GDN_DV
GDN_CONV_CH = 2 * GDN_QK + GDN_V
NSA_HEADS = 32
NSA_KV_HEADS = 4
HEAD_DIM = 128
GROUP = NSA_HEADS // NSA_KV_HEADS
BLOCK = 64
N_SELECT = 16
N_LOCAL = 2
WINDOW = 512
SCALE = HEAD_DIM ** -0.5
NEG = -1e30
FORCE = 1e4

N_PROMPT = BATCH * SEQ
N_SAMPLE = DEC_BATCH * DEC_SEQ
N_ROWS = N_PROMPT + N_SAMPLE

VMEM_LIMIT_BYTES = 56 * 1024 * 1024
LANES = 128
SUBLANES = 8

NORM_ROWS = 192
MM_ROWS = 1376
DOWN_ROWS = 688


def _params(*sem):
    return pltpu.CompilerParams(dimension_semantics=sem, vmem_limit_bytes=VMEM_LIMIT_BYTES)


PREP_ROWS = 64


def _lane_fold(x):
    acc = x[:, 0:LANES]
    for c in range(1, x.shape[1] // LANES):
        acc = acc + x[:, c * LANES:(c + 1) * LANES]
    return acc


def _rows_prep_kernel(xp_ref, xs_ref, h_ref, h16_ref, ssq_ref):
    n_prompt_tiles = N_PROMPT // PREP_ROWS

    def emit(x):
        h_ref[...] = x
        h16_ref[...] = x.astype(h16_ref.dtype)
        ssq_ref[...] = _lane_fold(x * x)

    @pl.when(pl.program_id(0) < n_prompt_tiles)
    def _():
        emit(xp_ref[...])

    @pl.when(pl.program_id(0) >= n_prompt_tiles)
    def _():
        emit(xs_ref[...])


def rows_prep(x_prompt, x_sample):
    d = x_prompt.shape[1]
    n_p = N_PROMPT // PREP_ROWS
    n = N_ROWS // PREP_ROWS
    return pl.pallas_call(
        _rows_prep_kernel,
        grid=(n,),
        in_specs=[pl.BlockSpec((PREP_ROWS, d), lambda i: (jnp.minimum(i, n_p - 1), 0)),
                  pl.BlockSpec((PREP_ROWS, d), lambda i: (jnp.maximum(i - n_p, 0), 0))],
        out_specs=[pl.BlockSpec((PREP_ROWS, d), lambda i: (i, 0)),
                   pl.BlockSpec((PREP_ROWS, d), lambda i: (i, 0)),
                   pl.BlockSpec((PREP_ROWS, LANES), lambda i: (i, 0))],
        out_shape=[jax.ShapeDtypeStruct((N_ROWS, d), jnp.float32),
                   jax.ShapeDtypeStruct((N_ROWS, d), jnp.bfloat16),
                   jax.ShapeDtypeStruct((N_ROWS, LANES), jnp.float32)],
        compiler_params=_params("parallel"),
        name="rows_prep",
    )(x_prompt, x_sample)


def _row_scale(ssq_ref, k):
    return lax.rsqrt(jnp.sum(ssq_ref[...], axis=-1, keepdims=True) / k + EPS)


def _gained_weight(w_ref, g_ref):
    w = w_ref[...]
    g = g_ref[...]
    cols = [w[:, c * LANES:(c + 1) * LANES] * g for c in range(w.shape[1] // LANES)]
    return jnp.concatenate(cols, axis=1).astype(jnp.bfloat16)


def _weight_spec(w, lead, tn):
    k = w.shape[-2]
    return pl.BlockSpec((None,) * len(lead) + (k, tn), lambda i, j: tuple(lead) + (0, j))


def _normed_specs(tm, k):
    return [pl.BlockSpec((tm, k), lambda i, j: (i, 0)),
            pl.BlockSpec((tm, LANES), lambda i, j: (i, 0)),
            pl.BlockSpec((k, LANES), lambda i, j: (0, 0))]


def _gain_lanes(g):
    return jnp.broadcast_to(g[:, None], (g.shape[0], LANES))


def _mm_kernel(x_ref, ssq_ref, g_ref, w_ref, o_ref):
    acc = jnp.dot(x_ref[...], _gained_weight(w_ref, g_ref), preferred_element_type=jnp.float32)
    o_ref[...] = (_row_scale(ssq_ref, x_ref.shape[1]) * acc).astype(o_ref.dtype)


def matmul_normed(x16, ssq, gain, w, lead=(), *, tn, n_cols=None, tm=MM_ROWS):
    m, k = x16.shape
    n = n_cols or w.shape[-1]
    return pl.pallas_call(
        _mm_kernel,
        grid=(m // tm, n // tn),
        in_specs=_normed_specs(tm, k) + [_weight_spec(w, lead, tn)],
        out_specs=pl.BlockSpec((tm, tn), lambda i, j: (i, j)),
        out_shape=jax.ShapeDtypeStruct((m, n), jnp.float32),
        compiler_params=_params("parallel", "arbitrary"),
        name="matmul_normed",
    )(x16, ssq, _gain_lanes(gain), w)


def _mm_res_kernel(x_ref, w_ref, r_ref, o_ref, o16_ref, ssq_ref, *, scale):
    w = w_ref[...].astype(jnp.bfloat16)
    h = r_ref[...] + scale * jnp.dot(x_ref[...], w, preferred_element_type=jnp.float32)
    o_ref[...] = h
    o16_ref[...] = h.astype(o16_ref.dtype)
    part = _lane_fold(h * h)

    @pl.when(pl.program_id(1) == 0)
    def _():
        ssq_ref[...] = part

    @pl.when(pl.program_id(1) > 0)
    def _():
        ssq_ref[...] += part


def matmul_residual(x, w, lead, res, scale, *, tn, tm):
    m, k = x.shape
    n = w.shape[-1]
    return pl.pallas_call(
        functools.partial(_mm_res_kernel, scale=scale),
        grid=(m // tm, n // tn),
        in_specs=[pl.BlockSpec((tm, k), lambda i, j: (i, 0)), _weight_spec(w, lead, tn),
                  pl.BlockSpec((tm, tn), lambda i, j: (i, j))],
        out_specs=[pl.BlockSpec((tm, tn), lambda i, j: (i, j)),
                   pl.BlockSpec((tm, tn), lambda i, j: (i, j)),
                   pl.BlockSpec((tm, LANES), lambda i, j: (i, 0))],
        out_shape=[jax.ShapeDtypeStruct((m, n), jnp.float32),
                   jax.ShapeDtypeStruct((m, n), jnp.bfloat16),
                   jax.ShapeDtypeStruct((m, LANES), jnp.float32)],
        compiler_params=_params("parallel", "arbitrary"),
        name="matmul_residual",
    )(x, w, res)


def _swiglu_up_kernel(x_ref, ssq_ref, g_ref, wg_ref, wu_ref, o_ref):
    x = x_ref[...]
    r = _row_scale(ssq_ref, x.shape[1])
    g = r * jnp.dot(x, _gained_weight(wg_ref, g_ref), preferred_element_type=jnp.float32)
    u = r * jnp.dot(x, _gained_weight(wu_ref, g_ref), preferred_element_type=jnp.float32)
    o_ref[...] = (g * jax.nn.sigmoid(g) * u).astype(o_ref.dtype)


def swiglu_up(x16, ssq, gain, wg, wu, lead, tn=256, tm=MM_ROWS):
    m, k = x16.shape
    n = wg.shape[-1]
    return pl.pallas_call(
        _swiglu_up_kernel,
        grid=(m // tm, n // tn),
        in_specs=_normed_specs(tm, k) + [_weight_spec(wg, lead, tn), _weight_spec(wu, lead, tn)],
        out_specs=pl.BlockSpec((tm, tn), lambda i, j: (i, j)),
        out_shape=jax.ShapeDtypeStruct((m, n), jnp.bfloat16),
        compiler_params=_params("parallel", "arbitrary"),
        name="swiglu_up",
    )(x16, ssq, _gain_lanes(gain), wg, wu)


_HI = lax.Precision.HIGHEST
_NT = (((1,), (1,)), ((), ()))
_TN = (((0,), (0,)), ((), ()))


def _bdot(a, b, dims=None):
    a = a.astype(jnp.bfloat16)
    b = b.astype(jnp.bfloat16)
    if dims is None:
        return jnp.dot(a, b, preferred_element_type=jnp.float32)
    return lax.dot_general(a, b, dims, preferred_element_type=jnp.float32)


def _hdot(a, b, dims=None):
    if dims is None:
        return jnp.dot(a, b, precision=_HI, preferred_element_type=jnp.float32)
    return lax.dot_general(a, b, dims, precision=_HI, preferred_element_type=jnp.float32)


def _iota2(shape, axis):
    return lax.broadcasted_iota(jnp.int32, shape, axis)


def _sigmoid(x):
    return 1.0 / (1.0 + jnp.exp(-x))


def _softmax_rows(s):
    m = jnp.max(s, axis=-1, keepdims=True)
    e = jnp.exp(s - m)
    return e, jnp.sum(e, axis=-1, keepdims=True)


GDN_HB = 16
CONV_PAD = SUBLANES


def _gdn_kernel(uq_ref, uk_ref, uv_ref, gate_ref, ab_ref, cq_ref, ck_ref, cv_ref, s0_ref,
                wq_ref, wk_ref, wv_ref, hp_ref, onorm_ref,
                o_ref, s_ref, extq, extk, extv, *, hb, chunk, valid_rows):
    C = chunk
    c = pl.program_id(2)

    @pl.when(c == 0)
    def _():
        s_ref[...] = s0_ref[...]
        extq[0:CONV_PAD, :] = cq_ref[0]
        extk[0:CONV_PAD, :] = ck_ref[0]
        extv[0:CONV_PAD, :] = cv_ref[0]

    def conv(ext, u_ref, w_ref):
        ext[CONV_PAD:CONV_PAD + C, :] = u_ref[...]
        base = CONV_PAD - (CONV_W - 1)
        acc = ext[base:base + C, :] * w_ref[0:1, :]
        for i in range(1, CONV_W):
            acc = acc + ext[base + i:base + i + C, :] * w_ref[i:i + 1, :]
        ext[0:CONV_PAD, :] = ext[C:C + CONV_PAD, :]
        return acc * _sigmoid(acc)

    qc = conv(extq, uq_ref, wq_ref)
    kc = conv(extk, uk_ref, wk_ref)
    vc = conv(extv, uv_ref, wv_ref)

    row = _iota2((C, C), 0)
    col = _iota2((C, C), 1)
    tri_incl = row >= col
    tri_strict = row > col
    row_ok = None
    if valid_rows < C:
        row_ok = _iota2((C, 1), 0) < valid_rows

    ab = ab_ref[...]
    x = ab + hp_ref[0, 1:2, :]
    softplus = jnp.maximum(x, 0.0) + jnp.log(1.0 + jnp.exp(-jnp.abs(x)))
    gmat = -jnp.exp(hp_ref[0, 0:1, :]) * softplus
    if row_ok is not None:
        gmat = jnp.where(row_ok, gmat, 0.0)
    beta = _sigmoid(ab)
    gam = _hdot(tri_incl.astype(jnp.float32), gmat)
    eye_l = (_iota2((LANES, LANES), 0) == _iota2((LANES, LANES), 1)).astype(jnp.float32)
    gam_t = _hdot(eye_l, gam, _NT)

    heads = range(hb)
    gc = [gam[:, h:h + 1] for h in heads]
    bc = [beta[:, hb + h:hb + h + 1] for h in heads]
    glast = [gam[C - 1:C, h:h + 1] for h in heads]
    q, k, v, decay = [], [], [], []
    for h in heads:
        qh = qc[:, h * GDN_DK:(h + 1) * GDN_DK]
        kh = kc[:, h * GDN_DK:(h + 1) * GDN_DK]
        vh = vc[:, h * GDN_DV:(h + 1) * GDN_DV]
        qh = qh * lax.rsqrt(jnp.sum(qh * qh, axis=-1, keepdims=True) + EPS) * GDN_DK ** -0.5
        kh = kh * lax.rsqrt(jnp.sum(kh * kh, axis=-1, keepdims=True) + EPS)
        if row_ok is not None:
            qh = jnp.where(row_ok, qh, 0.0)
            kh = jnp.where(row_ok, kh, 0.0)
            vh = jnp.where(row_ok, vh, 0.0)
        q.append(qh)
        k.append(kh)
        v.append(vh)
        decay.append(jnp.exp(jnp.where(tri_incl, gc[h] - gam_t[h:h + 1, :], -jnp.inf)))
    kk = [_bdot(k[h], k[h], _NT) for h in heads]
    qk = [_bdot(q[h], k[h], _NT) for h in heads]
    s_old = [s_ref[0, h] for h in heads]
    kq_s = [_bdot(jnp.concatenate([k[h], q[h]], axis=0), s_old[h]) for h in heads]
    pw = [jnp.where(tri_strict, decay[h] * kk[h], 0.0) * bc[h] for h in heads]
    nil = [-pw[h] for h in heads]
    for _ in range(int(math.log2(C)) - 1):
        pw = [_bdot(pw[h], pw[h]) for h in heads]
        nil = [nil[h] + pw[h] + _bdot(nil[h], pw[h]) for h in heads]
    gt = [jnp.exp(gc[h]) for h in heads]
    rhs = [bc[h] * (v[h] - gt[h] * kq_s[h][:C]) for h in heads]
    u = [rhs[h] + _bdot(nil[h], rhs[h]) for h in heads]
    o = [gt[h] * kq_s[h][C:] + _bdot(decay[h] * qk[h], u[h]) for h in heads]
    for h in heads:
        s_ref[0, h] = jnp.exp(glast[h]) * s_old[h] + _bdot(k[h] * jnp.exp(glast[h] - gc[h]), u[h], _TN)
    for h in heads:
        on = o[h] * lax.rsqrt(jnp.mean(o[h] * o[h], axis=-1, keepdims=True) + EPS) * onorm_ref[...]
        gate = gate_ref[:, h * GDN_DV:(h + 1) * GDN_DV]
        o_ref[:, h * GDN_DV:(h + 1) * GDN_DV] = (on * gate * _sigmoid(gate)).astype(o_ref.dtype)


def gdn_mixer(proj, ab, conv_init, s0, conv_w8, hp, o_norm, *, batch, n_chunks, valid_rows=GDN_CHUNK, hb=GDN_HB):
    C = GDN_CHUNK
    ng = GDN_HEADS // hb
    rows = batch * n_chunks * C
    qw, vw = hb * GDN_DK, hb * GDN_DV

    def rowblk(off):
        return lambda b, g, c: (b * n_chunks + c, off + g)

    def fixed3(off):
        return lambda b, g, c: (b, 0, off + g)

    def wblk(off):
        return lambda b, g, c: (0, off + g)

    in_specs = [
        pl.BlockSpec((C, qw), rowblk(0)),
        pl.BlockSpec((C, qw), rowblk(ng)),
        pl.BlockSpec((C, vw), rowblk(ng)),
        pl.BlockSpec((C, vw), rowblk(2 * ng)),
        pl.BlockSpec((C, LANES), rowblk(0)),
        pl.BlockSpec((1, CONV_PAD, qw), fixed3(0)),
        pl.BlockSpec((1, CONV_PAD, qw), fixed3(ng)),
        pl.BlockSpec((1, CONV_PAD, vw), fixed3(ng)),
        pl.BlockSpec((1, hb, GDN_DK, GDN_DV), lambda b, g, c: (b, g, 0, 0)),
        pl.BlockSpec((CONV_PAD, qw), wblk(0)),
        pl.BlockSpec((CONV_PAD, qw), wblk(ng)),
        pl.BlockSpec((CONV_PAD, vw), wblk(ng)),
        pl.BlockSpec((1, CONV_PAD, LANES), lambda b, g, c: (g, 0, 0)),
        pl.BlockSpec((1, GDN_DV), lambda b, g, c: (0, 0)),
    ]
    out_specs = [
        pl.BlockSpec((C, vw), rowblk(0)),
        pl.BlockSpec((1, hb, GDN_DK, GDN_DV), lambda b, g, c: (b, g, 0, 0)),
    ]
    return pl.pallas_call(
        functools.partial(_gdn_kernel, hb=hb, chunk=C, valid_rows=valid_rows),
        grid=(batch, ng, n_chunks),
        in_specs=in_specs,
        out_specs=out_specs,
        out_shape=[jax.ShapeDtypeStruct((rows, GDN_V), jnp.bfloat16),
                   jax.ShapeDtypeStruct((batch, GDN_HEADS, GDN_DK, GDN_DV), jnp.float32)],
        scratch_shapes=[pltpu.VMEM((CONV_PAD + C, qw), jnp.float32),
                        pltpu.VMEM((CONV_PAD + C, qw), jnp.float32),
                        pltpu.VMEM((CONV_PAD + C, vw), jnp.float32)],
        compiler_params=_params("parallel", "parallel", "arbitrary"),
        name="gdn_mixer",
    )(proj, proj, proj, proj, ab, conv_init, conv_init, conv_init, s0,
      conv_w8, conv_w8, conv_w8, hp, o_norm.reshape(1, GDN_DV))


def gdn_group_cols(w_ab, hb=GDN_HB):
    d = w_ab.shape[0]
    ng = GDN_HEADS // hb
    a = w_ab[:, :GDN_HEADS].reshape(d, ng, hb)
    b = w_ab[:, GDN_HEADS:].reshape(d, ng, hb)
    blk = jnp.concatenate([a, b, jnp.zeros((d, ng, LANES - 2 * hb), w_ab.dtype)], axis=-1)
    return blk.reshape(d, ng * LANES)


def gdn_head_params(a_log, dt_bias, hb=GDN_HB):
    ng = GDN_HEADS // hb
    rows = jnp.stack([a_log.reshape(ng, hb), dt_bias.reshape(ng, hb)], axis=1)
    return jnp.pad(rows, ((0, 0), (0, CONV_PAD - 2), (0, LANES - hb)))


KV_W = 3 * 2 * NSA_KV_HEADS * HEAD_DIM
KV_HALF = NSA_KV_HEADS * HEAD_DIM
N_KV_COLS = KV_W // HEAD_DIM
NB_PAD = LANES
_NORMED_KV_COLS = tuple(range(2 * NSA_KV_HEADS, 3 * NSA_KV_HEADS)) + tuple(range(4 * NSA_KV_HEADS, 5 * NSA_KV_HEADS))


def _kv_finish_kernel(kv_ref, gain_ref, o32_ref, o16_ref):
    for j in range(N_KV_COLS):
        sl = slice(j * HEAD_DIM, (j + 1) * HEAD_DIM)
        x = kv_ref[:, sl]
        if j in _NORMED_KV_COLS:
            x = x * lax.rsqrt(jnp.mean(x * x, axis=-1, keepdims=True) + EPS) * gain_ref[0:1, sl]
        o32_ref[:, sl] = x
        o16_ref[:, sl] = x.astype(o16_ref.dtype)


def kv_finish(kv, k_norm):
    n = kv.shape[0]
    ones = jnp.ones((KV_HALF,), jnp.float32)
    gain = jnp.concatenate([ones, ones, jnp.tile(k_norm[1], NSA_KV_HEADS), ones,
                            jnp.tile(k_norm[2], NSA_KV_HEADS), ones])
    gain = jnp.broadcast_to(gain[None], (SUBLANES, KV_W))
    return pl.pallas_call(
        _kv_finish_kernel,
        grid=(n // NORM_ROWS,),
        in_specs=[pl.BlockSpec((NORM_ROWS, KV_W), lambda i: (i, 0)),
                  pl.BlockSpec((SUBLANES, KV_W), lambda i: (0, 0))],
        out_specs=[pl.BlockSpec((NORM_ROWS, KV_W), lambda i: (i, 0)),
                   pl.BlockSpec((NORM_ROWS, KV_W), lambda i: (i, 0))],
        out_shape=[jax.ShapeDtypeStruct((n, KV_W), jnp.float32),
                   jax.ShapeDtypeStruct((n, KV_W), jnp.bfloat16)],
        compiler_params=_params("parallel"),
        name="kv_finish",
    )(kv, gain)


def _compress_prompt_kernel(rows_ref, pw_ref, wcmp_ref, kn_ref, ck_ref, cv_ref, *, nb):
    pooled_k = _hdot(pw_ref[0], rows_ref[:, 0:KV_HALF])
    pooled_v = _hdot(pw_ref[1], rows_ref[:, KV_HALF:2 * KV_HALF])
    ck_ref[...] = jnp.zeros_like(ck_ref)
    cv_ref[...] = jnp.zeros_like(cv_ref)
    for g in range(NSA_KV_HEADS):
        sl = slice(g * HEAD_DIM, (g + 1) * HEAD_DIM)
        k = _hdot(pooled_k[:, sl], wcmp_ref[0, g])
        k = k * lax.rsqrt(jnp.mean(k * k, axis=-1, keepdims=True) + EPS) * kn_ref[...]
        v = _hdot(pooled_v[:, sl], wcmp_ref[1, g])
        ck_ref[0, 0:nb, sl] = k.astype(ck_ref.dtype)
        cv_ref[0, 0:nb, sl] = v.astype(cv_ref.dtype)


def compress_prompt(kv32, cmp_pos_w, w_cmp, kn_cmp, *, batch, seq):
    nb = seq // BLOCK
    pw = jnp.einsum('nm,jc->cnmj', jnp.eye(nb, dtype=jnp.float32), cmp_pos_w).reshape(2, nb, seq)
    shape = jax.ShapeDtypeStruct((batch, NB_PAD, KV_HALF), jnp.bfloat16)
    return pl.pallas_call(
        functools.partial(_compress_prompt_kernel, nb=nb),
        grid=(batch,),
        in_specs=[pl.BlockSpec((seq, 2 * KV_HALF), lambda b: (b, 0)),
                  pl.BlockSpec((2, nb, seq), lambda b: (0, 0, 0)),
                  pl.BlockSpec((2, NSA_KV_HEADS, HEAD_DIM, HEAD_DIM), lambda b: (0, 0, 0, 0)),
                  pl.BlockSpec((1, HEAD_DIM), lambda b: (0, 0))],
        out_specs=[pl.BlockSpec((1, NB_PAD, KV_HALF), lambda b: (b, 0, 0)),
                   pl.BlockSpec((1, NB_PAD, KV_HALF), lambda b: (b, 0, 0))],
        out_shape=[shape, shape],
        compiler_params=_params("parallel"),
        name="compress_prompt",
    )(kv32, pw, w_cmp, kn_cmp.reshape(1, HEAD_DIM))


NSA_TQ = 256


def _select_mask(imp, qpos, nb):
    lane = _iota2(imp.shape, 1)
    cur = qpos // BLOCK
    causal = lane <= cur
    forced = (lane == 0) | (causal & (lane > cur - N_LOCAL))
    score = jnp.where(forced, FORCE, jnp.where(causal, imp, -1.0))
    score = jnp.where(lane < nb, score, -2.0)
    rank = jnp.zeros(imp.shape, jnp.float32)
    for j in range(nb):
        cj = score[:, j:j + 1]
        rank = rank + jnp.where((cj > score) | ((cj == score) & (lane > j)), 1.0, 0.0)
    return (rank < float(min(N_SELECT, nb))) & (lane < nb)


def _nsa_prompt_kernel(q_ref, gl_ref, ck_ref, cv_ref, ksel_ref, vsel_ref, kwin_ref, vwin_ref,
                       qn_ref, sl_ref, ex_ref, o_ref, acc_ref, *, tq, seq):
    nb = seq // BLOCK
    t0 = pl.program_id(2) * tq
    qpos = t0 + _iota2((tq, 1), 0)
    gates = _sigmoid(gl_ref[...])
    qs = []
    for r in range(GROUP):
        x = q_ref[:, r * HEAD_DIM:(r + 1) * HEAD_DIM]
        x = x * lax.rsqrt(jnp.mean(x * x, axis=-1, keepdims=True) + EPS) * qn_ref[...]
        qs.append((x * SCALE).astype(jnp.bfloat16))
    slopes = [sl_ref[0, r:r + 1, 0:1] for r in range(GROUP)]

    lane = _iota2((1, NB_PAD), 1)
    blk_end = (lane + 1) * BLOCK - 1
    valid_c = (qpos >= blk_end) & (lane < nb)
    off_c = jnp.where(lane < nb, NEG, -jnp.inf)
    any_c = (qpos >= BLOCK - 1).astype(jnp.float32)
    rel_c = (blk_end - t0).astype(jnp.float32)
    ck = ck_ref[0]
    cv = cv_ref[0]
    imp = jnp.zeros((tq, NB_PAD), jnp.float32)
    outs = []
    for r in range(GROUP):
        s = _bdot(qs[r], ck, _NT) + slopes[r] * rel_c
        e, l = _softmax_rows(jnp.where(valid_c, s, off_c))
        p = e / l * any_c
        imp = imp + p
        outs.append(gates[:, r:r + 1] * _bdot(p, cv))

    sel = _select_mask(imp, qpos, nb).astype(jnp.bfloat16)
    for r in range(GROUP):
        acc_ref[:, r * HEAD_DIM:(r + 1) * HEAD_DIM] = outs[r]
    for kq in range(seq // tq):
        @pl.when(pl.program_id(2) == kq)
        def _(kq=kq):
            nk = (kq + 1) * tq
            kpos = _iota2((1, nk), 1)
            allowed = (_bdot(sel, ex_ref[:, 0:nk]) > 0.5) & (kpos <= qpos)
            mask_s = jnp.where(allowed, 0.0, NEG)
            rel_s = (kpos - kq * tq).astype(jnp.float32)
            ksel = ksel_ref[0:nk, :]
            vsel = vsel_ref[0:nk, :]
            for r in range(GROUP):
                s = _bdot(qs[r], ksel, _NT) + slopes[r] * rel_s + mask_s
                e, l = _softmax_rows(s)
                acc_ref[:, r * HEAD_DIM:(r + 1) * HEAD_DIM] += gates[:, GROUP + r:GROUP + r + 1] * (_bdot(e, vsel) / l)

    nw = WINDOW + tq
    start = pl.multiple_of(jnp.maximum(t0 - WINDOW, 0), LANES)
    kwin = kwin_ref[pl.ds(start, nw), :]
    vwin = vwin_ref[pl.ds(start, nw), :]
    kpos_w = start + _iota2((1, nw), 1)
    dist = qpos - kpos_w
    mask_w = jnp.where((dist >= 0) & (dist < WINDOW), 0.0, NEG)
    rel_w = (kpos_w - t0).astype(jnp.float32)
    for r in range(GROUP):
        s = _bdot(qs[r], kwin, _NT) + slopes[r] * rel_w + mask_w
        e, l = _softmax_rows(s)
        o = acc_ref[:, r * HEAD_DIM:(r + 1) * HEAD_DIM] + gates[:, 2 * GROUP + r:2 * GROUP + r + 1] * (_bdot(e, vwin) / l)
        o_ref[:, r * HEAD_DIM:(r + 1) * HEAD_DIM] = o.astype(o_ref.dtype)


def nsa_slopes():
    h = jnp.arange(1, NSA_HEADS + 1, dtype=jnp.float32)
    s = (2.0 ** (-8.0 * h / NSA_HEADS)).reshape(NSA_KV_HEADS, GROUP, 1)
    return jnp.broadcast_to(s, (NSA_KV_HEADS, GROUP, LANES))


def nsa_gate_cols(w_g):
    d = w_g.shape[0]
    w = w_g.reshape(d, 3, NSA_KV_HEADS, GROUP).transpose(0, 2, 1, 3).reshape(d, NSA_KV_HEADS, 3 * GROUP)
    return jnp.pad(w, ((0, 0), (0, 0), (0, LANES - 3 * GROUP))).reshape(d, NSA_KV_HEADS * LANES)


def nsa_prompt(q, gl, ck, cv, kv16, q_norm, *, batch, seq, tq=NSA_TQ):
    nt = seq // tq
    expand = (jnp.arange(NB_PAD)[:, None] == (jnp.arange(seq)[None, :] // BLOCK)).astype(jnp.bfloat16)
    kvw = 2 * NSA_KV_HEADS

    def kvspec(col0):
        return pl.BlockSpec((seq, HEAD_DIM), lambda b, g, i: (b, col0 + g))

    return pl.pallas_call(
        functools.partial(_nsa_prompt_kernel, tq=tq, seq=seq),
        grid=(batch, NSA_KV_HEADS, nt),
        in_specs=[pl.BlockSpec((tq, GROUP * HEAD_DIM), lambda b, g, i: (b * nt + i, g)),
                  pl.BlockSpec((tq, LANES), lambda b, g, i: (b * nt + i, g)),
                  pl.BlockSpec((1, NB_PAD, HEAD_DIM), lambda b, g, i: (b, 0, g)),
                  pl.BlockSpec((1, NB_PAD, HEAD_DIM), lambda b, g, i: (b, 0, g)),
                  kvspec(kvw), kvspec(kvw + NSA_KV_HEADS), kvspec(2 * kvw), kvspec(2 * kvw + NSA_KV_HEADS),
                  pl.BlockSpec((1, HEAD_DIM), lambda b, g, i: (0, 0)),
                  pl.BlockSpec((1, GROUP, LANES), lambda b, g, i: (g, 0, 0)),
                  pl.BlockSpec((NB_PAD, seq), lambda b, g, i: (0, 0))],
        out_specs=pl.BlockSpec((tq, GROUP * HEAD_DIM), lambda b, g, i: (b * nt + i, g)),
        out_shape=jax.ShapeDtypeStruct((batch * seq, NSA_HEADS * HEAD_DIM), jnp.bfloat16),
        scratch_shapes=[pltpu.VMEM((tq, GROUP * HEAD_DIM), jnp.float32)],
        compiler_params=_params("parallel", "parallel", "arbitrary"),
        name="nsa_prompt",
    )(q, gl, ck, cv, kv16, kv16, kv16, kv16, q_norm.reshape(1, HEAD_DIM), nsa_slopes(), expand)


KV_SLOTS = 2 * NSA_KV_HEADS
N_PAGES = PAST_LEN // PAGE_SIZE
PAGES_PER_STEP = 4
PAGE_ROWS = PAGE_SIZE * KV_SLOTS
BLOCK_ROWS = BLOCK * KV_SLOTS
BLOCKS_PER_PAGE = PAGE_SIZE // BLOCK
N_PAST_BLOCKS = PAST_LEN // BLOCK
N_SAMPLE_BLOCKS = -(-(PAST_LEN + DEC_SEQ) // BLOCK)
SEL_LANES = -(-N_SAMPLE_BLOCKS // LANES) * LANES
QROWS = GROUP * DEC_SEQ


def _slot_rows(ref, lead, slot, n):
    return ref[lead, pl.ds(slot, n, stride=KV_SLOTS), :]


def _page_specs():
    def spec(j):
        return pl.BlockSpec((1, PAGE_ROWS, HEAD_DIM),
                            lambda b, i, pt: (pt[b * N_PAGES + PAGES_PER_STEP * i + j], 0, 0))
    return [spec(j) for j in range(PAGES_PER_STEP)]


def _paged(cache):
    return [cache.reshape(-1, PAGE_ROWS, HEAD_DIM)] * PAGES_PER_STEP


def _cmp_pool_kernel(pt_ref, p0, p1, p2, p3, w_ref, o_ref):
    i = pl.program_id(1)
    tiles = []
    for p in (p0, p1, p2, p3):
        prod = p[0] * w_ref[...]
        for h in range(BLOCKS_PER_PAGE):
            blk = prod[h * BLOCK_ROWS:(h + 1) * BLOCK_ROWS].reshape(BLOCK, KV_SLOTS, HEAD_DIM)
            tiles.append(jnp.sum(blk, axis=0))
    n = PAGES_PER_STEP * BLOCKS_PER_PAGE * KV_SLOTS
    o_ref[0, pl.ds(pl.multiple_of(i * n, n), n), :] = jnp.concatenate(tiles, axis=0)


def cmp_pool_pages(cache, page_table, cmp_pos_w):
    w = jnp.repeat(jnp.tile(cmp_pos_w, (BLOCKS_PER_PAGE, 1)), NSA_KV_HEADS, axis=1)
    wt = jnp.broadcast_to(w.reshape(PAGE_ROWS, 1), (PAGE_ROWS, HEAD_DIM))
    return pl.pallas_call(
        _cmp_pool_kernel,
        grid_spec=pltpu.PrefetchScalarGridSpec(
            num_scalar_prefetch=1,
            grid=(DEC_BATCH, N_PAGES // PAGES_PER_STEP),
            in_specs=_page_specs() + [pl.BlockSpec((PAGE_ROWS, HEAD_DIM), lambda b, i, pt: (0, 0))],
            out_specs=pl.BlockSpec((1, N_PAST_BLOCKS * KV_SLOTS, HEAD_DIM), lambda b, i, pt: (b, 0, 0))),
        out_shape=jax.ShapeDtypeStruct((DEC_BATCH, N_PAST_BLOCKS * KV_SLOTS, HEAD_DIM), jnp.float32),
        compiler_params=_params("parallel", "arbitrary"),
        name="cmp_pool_pages",
    )(page_table.reshape(-1), *_paged(cache), wt)


def _sample_queries(q_ref, qn_ref, g):
    parts = []
    for r in range(GROUP):
        c0 = (g * GROUP + r) * HEAD_DIM
        x = q_ref[:, c0:c0 + HEAD_DIM]
        x = x * lax.rsqrt(jnp.mean(x * x, axis=-1, keepdims=True) + EPS) * qn_ref[...]
        parts.append(x * SCALE)
    return jnp.concatenate(parts, axis=0).astype(jnp.bfloat16)


def _head_major_col(x, lane0):
    return jnp.concatenate([x[:, lane0 + r:lane0 + r + 1] for r in range(GROUP)], axis=0)


def _slope_col(sl_ref, g):
    return jnp.concatenate([jnp.broadcast_to(sl_ref[g, r:r + 1, 0:1], (DEC_SEQ, 1)) for r in range(GROUP)], axis=0)


def _sample_select_kernel(pooled_ref, wcmp_ref, kn_ref, q_ref, gl_ref, qn_ref, sl_ref, wcache_ref, new_ref, ex_ref,
                          ocw_ref, mask_ref, newmask_ref):
    srow = _iota2((QROWS, 1), 0) % DEC_SEQ
    qpos = PAST_LEN + srow
    gl = gl_ref[...]
    nbp = N_PAST_BLOCKS
    lane_c = _iota2((1, nbp), 1)
    blk_end = (lane_c + 1) * BLOCK - 1
    rel_c = (blk_end - PAST_LEN).astype(jnp.float32)
    nw = wcache_ref.shape[1] // KV_SLOTS
    nwk = nw + LANES
    jw = _iota2((1, nwk), 1)
    kpos_w = jnp.where(jw < nw, PAST_LEN - nw + jw, PAST_LEN + jw - nw)
    dist_w = qpos - kpos_w
    ok_w = (dist_w >= 0) & (dist_w < WINDOW) & (jw < nw + DEC_SEQ)
    mask_w = jnp.where(ok_w, 0.0, NEG)
    rel_w = (kpos_w - PAST_LEN).astype(jnp.float32)
    pad_rows = jnp.zeros((LANES - DEC_SEQ, HEAD_DIM), jnp.float32)
    for g in range(NSA_KV_HEADS):
        ck = _hdot(_slot_rows(pooled_ref, 0, g, nbp), wcmp_ref[0, g])
        ck = ck * lax.rsqrt(jnp.mean(ck * ck, axis=-1, keepdims=True) + EPS) * kn_ref[...]
        cv = _hdot(_slot_rows(pooled_ref, 0, NSA_KV_HEADS + g, nbp), wcmp_ref[1, g])
        qg = _sample_queries(q_ref, qn_ref, g)
        slope = _slope_col(sl_ref, g)
        gates = _sigmoid(gl[:, g * LANES:(g + 1) * LANES])
        s = _bdot(qg, ck, _NT) + slope * rel_c
        e, l = _softmax_rows(jnp.where(qpos >= blk_end, s, NEG))
        p = e / l
        o = _head_major_col(gates, 0) * _bdot(p, cv)
        imp = p[0:DEC_SEQ]
        for r in range(1, GROUP):
            imp = imp + p[r * DEC_SEQ:(r + 1) * DEC_SEQ]
        imp = jnp.concatenate([imp, jnp.zeros((DEC_SEQ, SEL_LANES - nbp), jnp.float32)], axis=1)
        sel = _select_mask(imp, PAST_LEN + _iota2((DEC_SEQ, 1), 0), N_SAMPLE_BLOCKS).astype(jnp.bfloat16)
        for half in range(nbp // LANES):
            keys = _bdot(sel[:, half * LANES:(half + 1) * LANES], ex_ref[...])
            mask_ref[0, g * DEC_SEQ:(g + 1) * DEC_SEQ, half * LANES * BLOCK:(half + 1) * LANES * BLOCK] = keys
        newmask_ref[0, g * DEC_SEQ:(g + 1) * DEC_SEQ, :] = jnp.broadcast_to(
            sel[:, nbp:nbp + 1].astype(jnp.float32), (DEC_SEQ, LANES))
        c0 = 4 * KV_HALF + g * HEAD_DIM
        kw = jnp.concatenate([_slot_rows(wcache_ref, 0, g, nw), new_ref[:, c0:c0 + HEAD_DIM], pad_rows], axis=0)
        vw = jnp.concatenate([_slot_rows(wcache_ref, 0, NSA_KV_HEADS + g, nw),
                              new_ref[:, c0 + KV_HALF:c0 + KV_HALF + HEAD_DIM], pad_rows], axis=0)
        s = _bdot(qg, kw, _NT) + slope * rel_w + mask_w
        e, l = _softmax_rows(s)
        o = o + _head_major_col(gates, 2 * GROUP) * (_bdot(e, vw) / l)
        for r in range(GROUP):
            c0 = (g * GROUP + r) * HEAD_DIM
            ocw_ref[:, c0:c0 + HEAD_DIM] = o[r * DEC_SEQ:(r + 1) * DEC_SEQ]


def nsa_sample_select(pooled, w_cmp, kn_cmp, q, gl, q_norm, cache_win, kv32):
    expand = (jnp.arange(LANES)[:, None] == (jnp.arange(LANES * BLOCK)[None, :] // BLOCK)).astype(jnp.bfloat16)
    row0 = N_PROMPT // DEC_SEQ
    nw = cache_win.shape[1]
    return pl.pallas_call(
        _sample_select_kernel,
        grid=(DEC_BATCH,),
        in_specs=[pl.BlockSpec((1, N_PAST_BLOCKS * KV_SLOTS, HEAD_DIM), lambda b: (b, 0, 0)),
                  pl.BlockSpec((2, NSA_KV_HEADS, HEAD_DIM, HEAD_DIM), lambda b: (0, 0, 0, 0)),
                  pl.BlockSpec((1, HEAD_DIM), lambda b: (0, 0)),
                  pl.BlockSpec((DEC_SEQ, NSA_HEADS * HEAD_DIM), lambda b: (row0 + b, 0)),
                  pl.BlockSpec((DEC_SEQ, NSA_KV_HEADS * LANES), lambda b: (row0 + b, 0)),
                  pl.BlockSpec((1, HEAD_DIM), lambda b: (0, 0)),
                  pl.BlockSpec((NSA_KV_HEADS, GROUP, LANES), lambda b: (0, 0, 0)),
                  pl.BlockSpec((1, nw * KV_SLOTS, HEAD_DIM), lambda b: (b, 0, 0)),
                  pl.BlockSpec((DEC_SEQ, KV_W), lambda b: (row0 + b, 0)),
                  pl.BlockSpec((LANES, LANES * BLOCK), lambda b: (0, 0))],
        out_specs=[pl.BlockSpec((DEC_SEQ, NSA_HEADS * HEAD_DIM), lambda b: (b, 0)),
                   pl.BlockSpec((1, NSA_KV_HEADS * DEC_SEQ, PAST_LEN), lambda b: (b, 0, 0)),
                   pl.BlockSpec((1, NSA_KV_HEADS * DEC_SEQ, LANES), lambda b: (b, 0, 0))],
        out_shape=[jax.ShapeDtypeStruct((N_SAMPLE, NSA_HEADS * HEAD_DIM), jnp.float32),
                   jax.ShapeDtypeStruct((DEC_BATCH, NSA_KV_HEADS * DEC_SEQ, PAST_LEN), jnp.float32),
                   jax.ShapeDtypeStruct((DEC_BATCH, NSA_KV_HEADS * DEC_SEQ, LANES), jnp.float32)],
        compiler_params=_params("parallel"),
        name="nsa_sample_select",
    )(pooled, w_cmp, kn_cmp.reshape(1, HEAD_DIM), q, gl, q_norm.reshape(1, HEAD_DIM), nsa_slopes(),
      cache_win.reshape(DEC_BATCH, nw * KV_SLOTS, HEAD_DIM), kv32, expand)


def _sample_sel_kernel(pt_ref, p0, p1, p2, p3, mask_ref, newmask_ref, q_ref, gl_ref, qn_ref, sl_ref, new_ref, ocw_ref,
                       o_ref, qs, m_sc, l_sc, acc_sc):
    i = pl.program_id(1)
    groups = range(NSA_KV_HEADS)

    @pl.when(i == 0)
    def _():
        for g in groups:
            qs[g] = _sample_queries(q_ref, qn_ref, g)
        m_sc[...] = jnp.full(m_sc.shape, NEG, jnp.float32)
        l_sc[...] = jnp.zeros(l_sc.shape, jnp.float32)
        acc_sc[...] = jnp.zeros(acc_sc.shape, jnp.float32)

    def accumulate(keys, vals, rel, keep8):
        keep = [jnp.concatenate([keep8[g]] * GROUP, axis=0) for g in groups]
        s = [_bdot(qs[g], keys[g], _NT) + _slope_col(sl_ref, g) * rel + jnp.where(keep[g] > 0.5, 0.0, NEG)
             for g in groups]
        m_old = [m_sc[g] for g in groups]
        m_new = [jnp.maximum(m_old[g], jnp.max(s[g], axis=-1, keepdims=True)) for g in groups]
        p = [jnp.exp(s[g] - m_new[g]) * keep[g] for g in groups]
        pv = [_bdot(p[g], vals[g]) for g in groups]
        for g in groups:
            alpha = jnp.exp(m_old[g] - m_new[g])
            l_sc[g] = alpha * l_sc[g] + jnp.sum(p[g], axis=-1, keepdims=True)
            acc_sc[g] = alpha * acc_sc[g] + pv[g]
            m_sc[g] = m_new[g]

    lane = _iota2((1, PAGE_SIZE), 1)
    for j, page in enumerate((p0, p1, p2, p3)):
        keep_all = mask_ref[0, :, j * PAGE_SIZE:(j + 1) * PAGE_SIZE]

        @pl.when(jnp.max(keep_all) > 0.5)
        def _(j=j, page=page, keep_all=keep_all):
            rel = ((i * PAGES_PER_STEP + j) * PAGE_SIZE - PAST_LEN + lane).astype(jnp.float32)
            accumulate([_slot_rows(page, 0, g, PAGE_SIZE) for g in groups],
                       [_slot_rows(page, 0, NSA_KV_HEADS + g, PAGE_SIZE) for g in groups],
                       rel, [keep_all[g * DEC_SEQ:(g + 1) * DEC_SEQ] for g in groups])

    @pl.when(i == pl.num_programs(1) - 1)
    def _():
        gl = gl_ref[...]
        pad_rows = jnp.zeros((PAGE_SIZE - DEC_SEQ, HEAD_DIM), jnp.float32)
        causal = ((lane <= _iota2((DEC_SEQ, 1), 0)) & (lane < DEC_SEQ)).astype(jnp.float32)
        c0 = 2 * KV_HALF
        accumulate([jnp.concatenate([new_ref[:, c0 + g * HEAD_DIM:c0 + (g + 1) * HEAD_DIM], pad_rows], axis=0)
                    for g in groups],
                   [jnp.concatenate([new_ref[:, c0 + KV_HALF + g * HEAD_DIM:c0 + KV_HALF + (g + 1) * HEAD_DIM], pad_rows],
                                    axis=0) for g in groups],
                   lane.astype(jnp.float32),
                   [newmask_ref[0, g * DEC_SEQ:(g + 1) * DEC_SEQ, :] * causal for g in groups])
        for g in groups:
            gates = _sigmoid(gl[:, g * LANES:(g + 1) * LANES])
            o = _head_major_col(gates, GROUP) * (acc_sc[g] / l_sc[g])
            for r in range(GROUP):
                c0 = (g * GROUP + r) * HEAD_DIM
                o_ref[:, c0:c0 + HEAD_DIM] = ocw_ref[:, c0:c0 + HEAD_DIM] + o[r * DEC_SEQ:(r + 1) * DEC_SEQ]


def nsa_sample_sel(cache, page_table, mask, newmask, q, gl, q_norm, kv32, ocw):
    row0 = N_PROMPT // DEC_SEQ
    nrow = NSA_KV_HEADS * DEC_SEQ
    step_keys = PAGES_PER_STEP * PAGE_SIZE
    return pl.pallas_call(
        _sample_sel_kernel,
        grid_spec=pltpu.PrefetchScalarGridSpec(
            num_scalar_prefetch=1,
            grid=(DEC_BATCH, N_PAGES // PAGES_PER_STEP),
            in_specs=_page_specs() + [
                pl.BlockSpec((1, nrow, step_keys), lambda b, i, pt: (b, 0, i)),
                pl.BlockSpec((1, nrow, LANES), lambda b, i, pt: (b, 0, 0)),
                pl.BlockSpec((DEC_SEQ, NSA_HEADS * HEAD_DIM), lambda b, i, pt: (row0 + b, 0)),
                pl.BlockSpec((DEC_SEQ, NSA_KV_HEADS * LANES), lambda b, i, pt: (row0 + b, 0)),
                pl.BlockSpec((1, HEAD_DIM), lambda b, i, pt: (0, 0)),
                pl.BlockSpec((NSA_KV_HEADS, GROUP, LANES), lambda b, i, pt: (0, 0, 0)),
                pl.BlockSpec((DEC_SEQ, KV_W), lambda b, i, pt: (row0 + b, 0)),
                pl.BlockSpec((DEC_SEQ, NSA_HEADS * HEAD_DIM), lambda b, i, pt: (b, 0))],
            out_specs=pl.BlockSpec((DEC_SEQ, NSA_HEADS * HEAD_DIM), lambda b, i, pt: (b, 0)),
            scratch_shapes=[pltpu.VMEM((NSA_KV_HEADS, QROWS, HEAD_DIM), jnp.bfloat16),
                            pltpu.VMEM((NSA_KV_HEADS, QROWS, 1), jnp.float32),
                            pltpu.VMEM((NSA_KV_HEADS, QROWS, 1), jnp.float32),
                            pltpu.VMEM((NSA_KV_HEADS, QROWS, HEAD_DIM), jnp.float32)]),
        out_shape=jax.ShapeDtypeStruct((N_SAMPLE, NSA_HEADS * HEAD_DIM), jnp.float32),
        compiler_params=_params("parallel", "arbitrary"),
        name="nsa_sample_sel",
    )(page_table.reshape(-1), *_paged(cache), mask, newmask,
      q, gl, q_norm.reshape(1, HEAD_DIM), nsa_slopes(), kv32, ocw)


def nsa_sample(q, gl, kv32, cache_cmp_kv, cache_sel_kv, cache_win_kv, page_table, cmp_pos_w, w_cmp, kn_cmp, q_norm):
    pooled = cmp_pool_pages(cache_cmp_kv, page_table, cmp_pos_w)
    ocw, mask, newmask = nsa_sample_select(pooled, w_cmp, kn_cmp, q, gl, q_norm, cache_win_kv, kv32)
    return nsa_sample_sel(cache_sel_kv, page_table, mask, newmask, q, gl, q_norm, kv32, ocw)


def _split_rows(x):
    return (x[:N_PROMPT].reshape((BATCH, SEQ) + x.shape[1:]),
            x[N_PROMPT:].reshape((DEC_BATCH, DEC_SEQ) + x.shape[1:]))


def kernel(x_prompt, x_sample, state_conv, state_delta, cache_cmp_kv, cache_sel_kv, cache_win_kv, page_table,
           ffn_norm, ffn_w_gate, ffn_w_up, ffn_w_down, mix_norm,
           gdn_w_in, gdn_conv_w, gdn_a_log, gdn_dt_bias, gdn_o_norm, gdn_w_out,
           kv_norm, w_kv, cmp_pos_w, w_cmp, k_norm, nsa_w_q, nsa_q_norm, nsa_w_o):
    bf = jnp.bfloat16
    n_main = GDN_CONV_CH + GDN_V
    n_q = NSA_HEADS * HEAD_DIM
    hist = CONV_W - 1
    wd16 = ffn_w_down.astype(bf)

    def ffn_half(rows, layer, i):
        h, h16, ssq = rows
        act = swiglu_up(h16, ssq, ffn_norm[layer, i], ffn_w_gate, ffn_w_up, (layer, i))
        return matmul_residual(act, wd16, (layer, i), h, 0.5, tn=256, tm=DOWN_ROWS)

    rows = ffn_half(rows_prep(x_prompt.reshape(N_PROMPT, D_MODEL), x_sample.reshape(N_SAMPLE, D_MODEL)), 0, 0)
    h, h16, ssq = rows
    proj = matmul_normed(h16, ssq, mix_norm[0], gdn_w_in, (0,), n_cols=n_main, tn=512)
    ab = matmul_normed(h16, ssq, mix_norm[0], gdn_group_cols(gdn_w_in[0, :, n_main:]), tn=LANES)
    conv_w8 = jnp.pad(gdn_conv_w[0], ((0, CONV_PAD - CONV_W), (0, 0)))
    hp = gdn_head_params(gdn_a_log[0], gdn_dt_bias[0])

    def pad_sample(x):
        x = x[N_PROMPT:].reshape(DEC_BATCH, DEC_SEQ, x.shape[1])
        return jnp.pad(x, ((0, 0), (0, GDN_CHUNK - DEC_SEQ), (0, 0))).reshape(DEC_BATCH * GDN_CHUNK, x.shape[2])

    o_p, delta_p = gdn_mixer(proj, ab, jnp.zeros((BATCH, CONV_PAD, GDN_CONV_CH), jnp.float32),
                             jnp.zeros((BATCH, GDN_HEADS, GDN_DK, GDN_DV), jnp.float32),
                             conv_w8, hp, gdn_o_norm[0], batch=BATCH, n_chunks=SEQ // GDN_CHUNK)
    o_s, delta_s = gdn_mixer(pad_sample(proj), pad_sample(ab),
                             jnp.pad(state_conv[0], ((0, 0), (CONV_PAD - hist, 0), (0, 0))), state_delta[0],
                             conv_w8, hp, gdn_o_norm[0], batch=DEC_BATCH, n_chunks=1, valid_rows=DEC_SEQ)
    o_s = o_s.reshape(DEC_BATCH, GDN_CHUNK, GDN_V)[:, :DEC_SEQ].reshape(N_SAMPLE, GDN_V)
    conv_p = jnp.stack([lax.slice(proj, ((b + 1) * SEQ - hist, 0), ((b + 1) * SEQ, GDN_CONV_CH)) for b in range(BATCH)])
    u_s = lax.slice(proj, (N_PROMPT, 0), (N_ROWS, GDN_CONV_CH)).reshape(DEC_BATCH, DEC_SEQ, GDN_CONV_CH)
    conv_s = jnp.concatenate([state_conv[0], u_s], axis=1)[:, DEC_SEQ:]
    rows = matmul_residual(jnp.concatenate([o_p, o_s], axis=0), gdn_w_out, (0,), h, 1.0, tn=256, tm=MM_ROWS)
    rows = ffn_half(rows, 0, 1)

    h, h16, ssq = rows
    kv32, kv16 = kv_finish(matmul_normed(h16, ssq, kv_norm, w_kv, tn=512), k_norm)
    kv5 = kv32.reshape(N_ROWS, 3, 2, NSA_KV_HEADS, HEAD_DIM)
    cmp_p, cmp_s = _split_rows(kv5[:, 0])
    sel_p, sel_s = _split_rows(kv5[:, 1])
    win_rows_p, win_rows_s = _split_rows(kv5[:, 2])

    h, h16, ssq = ffn_half(rows, 1, 0)
    q = matmul_normed(h16, ssq, mix_norm[1], nsa_w_q, (0,), n_cols=n_q, tn=512)
    gl = matmul_normed(h16, ssq, mix_norm[1], nsa_gate_cols(nsa_w_q[0, :, n_q:]), tn=LANES)
    ck, cv = compress_prompt(kv32, cmp_pos_w, w_cmp, k_norm[0], batch=BATCH, seq=SEQ)
    o_p = nsa_prompt(q, gl, ck, cv, kv16, nsa_q_norm[0], batch=BATCH, seq=SEQ)
    o_s = nsa_sample(q, gl, kv32, cache_cmp_kv, cache_sel_kv, cache_win_kv, page_table, cmp_pos_w, w_cmp, k_norm[0],
                     nsa_q_norm[0])
    rows = matmul_residual(jnp.concatenate([o_p, o_s.astype(bf)], axis=0), nsa_w_o, (0,), h, 1.0, tn=256, tm=MM_ROWS)
    h, _, _ = ffn_half(rows, 1, 1)

    y_p, y_s = _split_rows(h)
    win_p = win_rows_p[:, -min(WINDOW, SEQ):]
    win_s = jnp.concatenate([cache_win_kv, win_rows_s], axis=1)[:, DEC_SEQ:]
    return (y_p, y_s, conv_p[None], conv_s[None], delta_p[None], delta_s[None],
            cmp_p, cmp_s, sel_p, sel_s, win_p, win_s)
```

```python
import functools
import math

import jax
import jax.numpy as jnp
from jax import lax
from jax.experimental import pallas as pl
from jax.experimental.pallas import tpu as pltpu

D_MODEL = 4096
BATCH = 4
SEQ = 2048
DEPTH = 2
DEC_BATCH = 8
DEC_SEQ = 8
PAST_LEN = 16384
PAGE_SIZE = 128
D_FF = 11008
EPS = 1e-6
GDN_HEADS = 16
GDN_DK = 128
GDN_DV = 256
CONV_W = 4
GDN_CHUNK = 64
GDN_QK = GDN_HEADS * GDN_DK
GDN_V = GDN_HEADS * GDN_DV
GDN_CONV_CH = 2 * GDN_QK + GDN_V
NSA_HEADS = 32
NSA_KV_HEADS = 4
HEAD_DIM = 128
GROUP = NSA_HEADS // NSA_KV_HEADS
BLOCK = 64
N_SELECT = 16
N_LOCAL = 2
WINDOW = 512
SCALE = HEAD_DIM ** -0.5
NEG = -1e30
FORCE = 1e4

N_PROMPT = BATCH * SEQ
N_SAMPLE = DEC_BATCH * DEC_SEQ
N_ROWS = N_PROMPT + N_SAMPLE

VMEM_LIMIT_BYTES = 56 * 1024 * 1024
LANES = 128
SUBLANES = 8

NORM_ROWS = 192
MM_ROWS = 1376
DOWN_ROWS = 688


def _params(*sem):
    return pltpu.CompilerParams(dimension_semantics=sem, vmem_limit_bytes=VMEM_LIMIT_BYTES)


PREP_ROWS = 64


def _lane_fold(x):
    acc = x[:, 0:LANES]
    for c in range(1, x.shape[1] // LANES):
        acc = acc + x[:, c * LANES:(c + 1) * LANES]
    return acc


def _rows_prep_kernel(xp_ref, xs_ref, h_ref, h16_ref, ssq_ref):
    n_prompt_tiles = N_PROMPT // PREP_ROWS

    def emit(x):
        h_ref[...] = x
        h16_ref[...] = x.astype(h16_ref.dtype)
        ssq_ref[...] = _lane_fold(x * x)

    @pl.when(pl.program_id(0) < n_prompt_tiles)
    def _():
        emit(xp_ref[...])

    @pl.when(pl.program_id(0) >= n_prompt_tiles)
    def _():
        emit(xs_ref[...])


def rows_prep(x_prompt, x_sample):
    d = x_prompt.shape[1]
    n_p = N_PROMPT // PREP_ROWS
    n = N_ROWS // PREP_ROWS
    return pl.pallas_call(
        _rows_prep_kernel,
        grid=(n,),
        in_specs=[pl.BlockSpec((PREP_ROWS, d), lambda i: (jnp.minimum(i, n_p - 1), 0)),
                  pl.BlockSpec((PREP_ROWS, d), lambda i: (jnp.maximum(i - n_p, 0), 0))],
        out_specs=[pl.BlockSpec((PREP_ROWS, d), lambda i: (i, 0)),
                   pl.BlockSpec((PREP_ROWS, d), lambda i: (i, 0)),
                   pl.BlockSpec((PREP_ROWS, LANES), lambda i: (i, 0))],
        out_shape=[jax.ShapeDtypeStruct((N_ROWS, d), jnp.float32),
                   jax.ShapeDtypeStruct((N_ROWS, d), jnp.bfloat16),
                   jax.ShapeDtypeStruct((N_ROWS, LANES), jnp.float32)],
        compiler_params=_params("parallel"),
        name="rows_prep",
    )(x_prompt, x_sample)


def _row_scale(ssq_ref, k):
    return lax.rsqrt(jnp.sum(ssq_ref[...], axis=-1, keepdims=True) / k + EPS)


def _gained_weight(w_ref, g_ref):
    w = w_ref[...]
    g = g_ref[...]
    cols = [w[:, c * LANES:(c + 1) * LANES] * g for c in range(w.shape[1] // LANES)]
    return jnp.concatenate(cols, axis=1).astype(jnp.bfloat16)


def _weight_spec(w, lead, tn):
    k = w.shape[-2]
    return pl.BlockSpec((None,) * len(lead) + (k, tn), lambda i, j: tuple(lead) + (0, j))


def _normed_specs(tm, k):
    return [pl.BlockSpec((tm, k), lambda i, j: (i, 0)),
            pl.BlockSpec((tm, LANES), lambda i, j: (i, 0)),
            pl.BlockSpec((k, LANES), lambda i, j: (0, 0))]


def _gain_lanes(g):
    return jnp.broadcast_to(g[:, None], (g.shape[0], LANES))


def _mm_kernel(x_ref, ssq_ref, g_ref, w_ref, o_ref):
    acc = jnp.dot(x_ref[...], _gained_weight(w_ref, g_ref), preferred_element_type=jnp.float32)
    o_ref[...] = (_row_scale(ssq_ref, x_ref.shape[1]) * acc).astype(o_ref.dtype)


def matmul_normed(x16, ssq, gain, w, lead=(), *, tn, n_cols=None, tm=MM_ROWS):
    m, k = x16.shape
    n = n_cols or w.shape[-1]
    return pl.pallas_call(
        _mm_kernel,
        grid=(m // tm, n // tn),
        in_specs=_normed_specs(tm, k) + [_weight_spec(w, lead, tn)],
        out_specs=pl.BlockSpec((tm, tn), lambda i, j: (i, j)),
        out_shape=jax.ShapeDtypeStruct((m, n), jnp.float32),
        compiler_params=_params("parallel", "arbitrary"),
        name="matmul_normed",
    )(x16, ssq, _gain_lanes(gain), w)


def _mm_res_kernel(x_ref, w_ref, r_ref, o_ref, o16_ref, ssq_ref, *, scale):
    w = w_ref[...].astype(jnp.bfloat16)
    h = r_ref[...] + scale * jnp.dot(x_ref[...], w, preferred_element_type=jnp.float32)
    o_ref[...] = h
    o16_ref[...] = h.astype(o16_ref.dtype)
    part = _lane_fold(h * h)

    @pl.when(pl.program_id(1) == 0)
    def _():
        ssq_ref[...] = part

    @pl.when(pl.program_id(1) > 0)
    def _():
        ssq_ref[...] += part


def matmul_residual(x, w, lead, res, scale, *, tn, tm):
    m, k = x.shape
    n = w.shape[-1]
    return pl.pallas_call(
        functools.partial(_mm_res_kernel, scale=scale),
        grid=(m // tm, n // tn),
        in_specs=[pl.BlockSpec((tm, k), lambda i, j: (i, 0)), _weight_spec(w, lead, tn),
                  pl.BlockSpec((tm, tn), lambda i, j: (i, j))],
        out_specs=[pl.BlockSpec((tm, tn), lambda i, j: (i, j)),
                   pl.BlockSpec((tm, tn), lambda i, j: (i, j)),
                   pl.BlockSpec((tm, LANES), lambda i, j: (i, 0))],
        out_shape=[jax.ShapeDtypeStruct((m, n), jnp.float32),
                   jax.ShapeDtypeStruct((m, n), jnp.bfloat16),
                   jax.ShapeDtypeStruct((m, LANES), jnp.float32)],
        compiler_params=_params("parallel", "arbitrary"),
        name="matmul_residual",
    )(x, w, res)


def _swiglu_up_kernel(x_ref, ssq_ref, g_ref, wg_ref, wu_ref, o_ref):
    x = x_ref[...]
    r = _row_scale(ssq_ref, x.shape[1])
    g = r * jnp.dot(x, _gained_weight(wg_ref, g_ref), preferred_element_type=jnp.float32)
    u = r * jnp.dot(x, _gained_weight(wu_ref, g_ref), preferred_element_type=jnp.float32)
    o_ref[...] = (g * jax.nn.sigmoid(g) * u).astype(o_ref.dtype)


def swiglu_up(x16, ssq, gain, wg, wu, lead, tn=256, tm=MM_ROWS):
    m, k = x16.shape
    n = wg.shape[-1]
    return pl.pallas_call(
        _swiglu_up_kernel,
        grid=(m // tm, n // tn),
        in_specs=_normed_specs(tm, k) + [_weight_spec(wg, lead, tn), _weight_spec(wu, lead, tn)],
        out_specs=pl.BlockSpec((tm, tn), lambda i, j: (i, j)),
        out_shape=jax.ShapeDtypeStruct((m, n), jnp.bfloat16),
        compiler_params=_params("parallel", "arbitrary"),
        name="swiglu_up",
    )(x16, ssq, _gain_lanes(gain), wg, wu)


_HI = lax.Precision.HIGHEST
_NT = (((1,), (1,)), ((), ()))
_TN = (((0,), (0,)), ((), ()))


def _bdot(a, b, dims=None):
    a = a.astype(jnp.bfloat16)
    b = b.astype(jnp.bfloat16)
    if dims is None:
        return jnp.dot(a, b, preferred_element_type=jnp.float32)
    return lax.dot_general(a, b, dims, preferred_element_type=jnp.float32)


def _hdot(a, b, dims=None):
    if dims is None:
        return jnp.dot(a, b, precision=_HI, preferred_element_type=jnp.float32)
    return lax.dot_general(a, b, dims, precision=_HI, preferred_element_type=jnp.float32)


def _iota2(shape, axis):
    return lax.broadcasted_iota(jnp.int32, shape, axis)


def _sigmoid(x):
    return 1.0 / (1.0 + jnp.exp(-x))


def _softmax_rows(s):
    m = jnp.max(s, axis=-1, keepdims=True)
    e = jnp.exp(s - m)
    return e, jnp.sum(e, axis=-1, keepdims=True)


GDN_HB = 16
CONV_PAD = SUBLANES


def _gdn_kernel(uq_ref, uk_ref, uv_ref, gate_ref, ab_ref, cq_ref, ck_ref, cv_ref, s0_ref,
                wq_ref, wk_ref, wv_ref, hp_ref, onorm_ref,
                o_ref, s_ref, extq, extk, extv, *, hb, chunk, valid_rows):
    C = chunk
    c = pl.program_id(2)

    @pl.when(c == 0)
    def _():
        s_ref[...] = s0_ref[...]
        extq[0:CONV_PAD, :] = cq_ref[0]
        extk[0:CONV_PAD, :] = ck_ref[0]
        extv[0:CONV_PAD, :] = cv_ref[0]

    def conv(ext, u_ref, w_ref):
        ext[CONV_PAD:CONV_PAD + C, :] = u_ref[...]
        base = CONV_PAD - (CONV_W - 1)
        acc = ext[base:base + C, :] * w_ref[0:1, :]
        for i in range(1, CONV_W):
            acc = acc + ext[base + i:base + i + C, :] * w_ref[i:i + 1, :]
        ext[0:CONV_PAD, :] = ext[C:C + CONV_PAD, :]
        return acc * _sigmoid(acc)

    qc = conv(extq, uq_ref, wq_ref)
    kc = conv(extk, uk_ref, wk_ref)
    vc = conv(extv, uv_ref, wv_ref)

    row = _iota2((C, C), 0)
    col = _iota2((C, C), 1)
    tri_incl = row >= col
    tri_strict = row > col
    row_ok = None
    if valid_rows < C:
        row_ok = _iota2((C, 1), 0) < valid_rows

    ab = ab_ref[...]
    x = ab + hp_ref[0, 1:2, :]
    softplus = jnp.maximum(x, 0.0) + jnp.log(1.0 + jnp.exp(-jnp.abs(x)))
    gmat = -jnp.exp(hp_ref[0, 0:1, :]) * softplus
    if row_ok is not None:
        gmat = jnp.where(row_ok, gmat, 0.0)
    beta = _sigmoid(ab)
    gam = _hdot(tri_incl.astype(jnp.float32), gmat)
    eye_l = (_iota2((LANES, LANES), 0) == _iota2((LANES, LANES), 1)).astype(jnp.float32)
    gam_t = _hdot(eye_l, gam, _NT)

    heads = range(hb)
    gc = [gam[:, h:h + 1] for h in heads]
    bc = [beta[:, hb + h:hb + h + 1] for h in heads]
    glast = [gam[C - 1:C, h:h + 1] for h in heads]
    q, k, v, decay = [], [], [], []
    for h in heads:
        qh = qc[:, h * GDN_DK:(h + 1) * GDN_DK]
        kh = kc[:, h * GDN_DK:(h + 1) * GDN_DK]
        vh = vc[:, h * GDN_DV:(h + 1) * GDN_DV]
        qh = qh * lax.rsqrt(jnp.sum(qh * qh, axis=-1, keepdims=True) + EPS) * GDN_DK ** -0.5
        kh = kh * lax.rsqrt(jnp.sum(kh * kh, axis=-1, keepdims=True) + EPS)
        if row_ok is not None:
            qh = jnp.where(row_ok, qh, 0.0)
            kh = jnp.where(row_ok, kh, 0.0)
            vh = jnp.where(row_ok, vh, 0.0)
        q.append(qh)
        k.append(kh)
        v.append(vh)
        decay.append(jnp.exp(jnp.where(tri_incl, gc[h] - gam_t[h:h + 1, :], -jnp.inf)))
    kk = [_bdot(k[h], k[h], _NT) for h in heads]
    qk = [_bdot(q[h], k[h], _NT) for h in heads]
    s_old = [s_ref[0, h] for h in heads]
    kq_s = [_bdot(jnp.concatenate([k[h], q[h]], axis=0), s_old[h]) for h in heads]
    pw = [jnp.where(tri_strict, decay[h] * kk[h], 0.0) * bc[h] for h in heads]
    nil = [-pw[h] for h in heads]
    for _ in range(int(math.log2(C)) - 1):
        pw = [_bdot(pw[h], pw[h]) for h in heads]
        nil = [nil[h] + pw[h] + _bdot(nil[h], pw[h]) for h in heads]
    gt = [jnp.exp(gc[h]) for h in heads]
    rhs = [bc[h] * (v[h] - gt[h] * kq_s[h][:C]) for h in heads]
    u = [rhs[h] + _bdot(nil[h], rhs[h]) for h in heads]
    o = [gt[h] * kq_s[h][C:] + _bdot(decay[h] * qk[h], u[h]) for h in heads]
    for h in heads:
        s_ref[0, h] = jnp.exp(glast[h]) * s_old[h] + _bdot(k[h] * jnp.exp(glast[h] - gc[h]), u[h], _TN)
    for h in heads:
        on = o[h] * lax.rsqrt(jnp.mean(o[h] * o[h], axis=-1, keepdims=True) + EPS) * onorm_ref[...]
        gate = gate_ref[:, h * GDN_DV:(h + 1) * GDN_DV]
        o_ref[:, h * GDN_DV:(h + 1) * GDN_DV] = (on * gate * _sigmoid(gate)).astype(o_ref.dtype)


def gdn_mixer(proj, ab, conv_init, s0, conv_w8, hp, o_norm, *, batch, n_chunks, valid_rows=GDN_CHUNK, hb=GDN_HB):
    C = GDN_CHUNK
    ng = GDN_HEADS // hb
    rows = batch * n_chunks * C
    qw, vw = hb * GDN_DK, hb * GDN_DV

    def rowblk(off):
        return lambda b, g, c: (b * n_chunks + c, off + g)

    def fixed3(off):
        return lambda b, g, c: (b, 0, off + g)

    def wblk(off):
        return lambda b, g, c: (0, off + g)

    in_specs = [
        pl.BlockSpec((C, qw), rowblk(0)),
        pl.BlockSpec((C, qw), rowblk(ng)),
        pl.BlockSpec((C, vw), rowblk(ng)),
        pl.BlockSpec((C, vw), rowblk(2 * ng)),
        pl.BlockSpec((C, LANES), rowblk(0)),
        pl.BlockSpec((1, CONV_PAD, qw), fixed3(0)),
        pl.BlockSpec((1, CONV_PAD, qw), fixed3(ng)),
        pl.BlockSpec((1, CONV_PAD, vw), fixed3(ng)),
        pl.BlockSpec((1, hb, GDN_DK, GDN_DV), lambda b, g, c: (b, g, 0, 0)),
        pl.BlockSpec((CONV_PAD, qw), wblk(0)),
        pl.BlockSpec((CONV_PAD, qw), wblk(ng)),
        pl.BlockSpec((CONV_PAD, vw), wblk(ng)),
        pl.BlockSpec((1, CONV_PAD, LANES), lambda b, g, c: (g, 0, 0)),
        pl.BlockSpec((1, GDN_DV), lambda b, g, c: (0, 0)),
    ]
    out_specs = [
        pl.BlockSpec((C, vw), rowblk(0)),
        pl.BlockSpec((1, hb, GDN_DK, GDN_DV), lambda b, g, c: (b, g, 0, 0)),
    ]
    return pl.pallas_call(
        functools.partial(_gdn_kernel, hb=hb, chunk=C, valid_rows=valid_rows),
        grid=(batch, ng, n_chunks),
        in_specs=in_specs,
        out_specs=out_specs,
        out_shape=[jax.ShapeDtypeStruct((rows, GDN_V), jnp.bfloat16),
                   jax.ShapeDtypeStruct((batch, GDN_HEADS, GDN_DK, GDN_DV), jnp.float32)],
        scratch_shapes=[pltpu.VMEM((CONV_PAD + C, qw), jnp.float32),
                        pltpu.VMEM((CONV_PAD + C, qw), jnp.float32),
                        pltpu.VMEM((CONV_PAD + C, vw), jnp.float32)],
        compiler_params=_params("parallel", "parallel", "arbitrary"),
        name="gdn_mixer",
    )(proj, proj, proj, proj, ab, conv_init, conv_init, conv_init, s0,
      conv_w8, conv_w8, conv_w8, hp, o_norm.reshape(1, GDN_DV))


def gdn_group_cols(w_ab, hb=GDN_HB):
    d = w_ab.shape[0]
    ng = GDN_HEADS // hb
    a = w_ab[:, :GDN_HEADS].reshape(d, ng, hb)
    b = w_ab[:, GDN_HEADS:].reshape(d, ng, hb)
    blk = jnp.concatenate([a, b, jnp.zeros((d, ng, LANES - 2 * hb), w_ab.dtype)], axis=-1)
    return blk.reshape(d, ng * LANES)


def gdn_head_params(a_log, dt_bias, hb=GDN_HB):
    ng = GDN_HEADS // hb
    rows = jnp.stack([a_log.reshape(ng, hb), dt_bias.reshape(ng, hb)], axis=1)
    return jnp.pad(rows, ((0, 0), (0, CONV_PAD - 2), (0, LANES - hb)))


KV_W = 3 * 2 * NSA_KV_HEADS * HEAD_DIM
KV_HALF = NSA_KV_HEADS * HEAD_DIM
N_KV_COLS = KV_W // HEAD_DIM
NB_PAD = LANES
_NORMED_KV_COLS = tuple(range(2 * NSA_KV_HEADS, 3 * NSA_KV_HEADS)) + tuple(range(4 * NSA_KV_HEADS, 5 * NSA_KV_HEADS))


def _kv_finish_kernel(kv_ref, gain_ref, o32_ref, o16_ref):
    for j in range(N_KV_COLS):
        sl = slice(j * HEAD_DIM, (j + 1) * HEAD_DIM)
        x = kv_ref[:, sl]
        if j in _NORMED_KV_COLS:
            x = x * lax.rsqrt(jnp.mean(x * x, axis=-1, keepdims=True) + EPS) * gain_ref[0:1, sl]
        o32_ref[:, sl] = x
        o16_ref[:, sl] = x.astype(o16_ref.dtype)


def kv_finish(kv, k_norm):
    n = kv.shape[0]
    ones = jnp.ones((KV_HALF,), jnp.float32)
    gain = jnp.concatenate([ones, ones, jnp.tile(k_norm[1], NSA_KV_HEADS), ones,
                            jnp.tile(k_norm[2], NSA_KV_HEADS), ones])
    gain = jnp.broadcast_to(gain[None], (SUBLANES, KV_W))
    return pl.pallas_call(
        _kv_finish_kernel,
        grid=(n // NORM_ROWS,),
        in_specs=[pl.BlockSpec((NORM_ROWS, KV_W), lambda i: (i, 0)),
                  pl.BlockSpec((SUBLANES, KV_W), lambda i: (0, 0))],
        out_specs=[pl.BlockSpec((NORM_ROWS, KV_W), lambda i: (i, 0)),
                   pl.BlockSpec((NORM_ROWS, KV_W), lambda i: (i, 0))],
        out_shape=[jax.ShapeDtypeStruct((n, KV_W), jnp.float32),
                   jax.ShapeDtypeStruct((n, KV_W), jnp.bfloat16)],
        compiler_params=_params("parallel"),
        name="kv_finish",
    )(kv, gain)


def _compress_prompt_kernel(rows_ref, pw_ref, wcmp_ref, kn_ref, ck_ref, cv_ref, *, nb):
    pooled_k = _hdot(pw_ref[0], rows_ref[:, 0:KV_HALF])
    pooled_v = _hdot(pw_ref[1], rows_ref[:, KV_HALF:2 * KV_HALF])
    ck_ref[...] = jnp.zeros_like(ck_ref)
    cv_ref[...] = jnp.zeros_like(cv_ref)
    for g in range(NSA_KV_HEADS):
        sl = slice(g * HEAD_DIM, (g + 1) * HEAD_DIM)
        k = _hdot(pooled_k[:, sl], wcmp_ref[0, g])
        k = k * lax.rsqrt(jnp.mean(k * k, axis=-1, keepdims=True) + EPS) * kn_ref[...]
        v = _hdot(pooled_v[:, sl], wcmp_ref[1, g])
        ck_ref[0, 0:nb, sl] = k.astype(ck_ref.dtype)
        cv_ref[0, 0:nb, sl] = v.astype(cv_ref.dtype)


def compress_prompt(kv32, cmp_pos_w, w_cmp, kn_cmp, *, batch, seq):
    nb = seq // BLOCK
    pw = jnp.einsum('nm,jc->cnmj', jnp.eye(nb, dtype=jnp.float32), cmp_pos_w).reshape(2, nb, seq)
    shape = jax.ShapeDtypeStruct((batch, NB_PAD, KV_HALF), jnp.bfloat16)
    return pl.pallas_call(
        functools.partial(_compress_prompt_kernel, nb=nb),
        grid=(batch,),
        in_specs=[pl.BlockSpec((seq, 2 * KV_HALF), lambda b: (b, 0)),
                  pl.BlockSpec((2, nb, seq), lambda b: (0, 0, 0)),
                  pl.BlockSpec((2, NSA_KV_HEADS, HEAD_DIM, HEAD_DIM), lambda b: (0, 0, 0, 0)),
                  pl.BlockSpec((1, HEAD_DIM), lambda b: (0, 0))],
        out_specs=[pl.BlockSpec((1, NB_PAD, KV_HALF), lambda b: (b, 0, 0)),
                   pl.BlockSpec((1, NB_PAD, KV_HALF), lambda b: (b, 0, 0))],
        out_shape=[shape, shape],
        compiler_params=_params("parallel"),
        name="compress_prompt",
    )(kv32, pw, w_cmp, kn_cmp.reshape(1, HEAD_DIM))


NSA_TQ = 256


def _select_mask(imp, qpos, nb):
    lane = _iota2(imp.shape, 1)
    cur = qpos // BLOCK
    causal = lane <= cur
    forced = (lane == 0) | (causal & (lane > cur - N_LOCAL))
    score = jnp.where(forced, FORCE, jnp.where(causal, imp, -1.0))
    score = jnp.where(lane < nb, score, -2.0)
    rank = jnp.zeros(imp.shape, jnp.float32)
    for j in range(nb):
        cj = score[:, j:j + 1]
        rank = rank + jnp.where((cj > score) | ((cj == score) & (lane > j)), 1.0, 0.0)
    return (rank < float(min(N_SELECT, nb))) & (lane < nb)


def _nsa_prompt_kernel(q_ref, gl_ref, ck_ref, cv_ref, ksel_ref, vsel_ref, kwin_ref, vwin_ref,
                       qn_ref, sl_ref, ex_ref, o_ref, *, tq, seq):
    nb = seq // BLOCK
    t0 = pl.program_id(2) * tq
    qpos = t0 + _iota2((tq, 1), 0)
    gates = _sigmoid(gl_ref[...])
    qs = []
    for r in range(GROUP):
        x = q_ref[:, r * HEAD_DIM:(r + 1) * HEAD_DIM]
        x = x * lax.rsqrt(jnp.mean(x * x, axis=-1, keepdims=True) + EPS) * qn_ref[...]
        qs.append((x * SCALE).astype(jnp.bfloat16))
    slopes = [sl_ref[0, r:r + 1, 0:1] for r in range(GROUP)]

    lane = _iota2((1, NB_PAD), 1)
    blk_end = (lane + 1) * BLOCK - 1
    valid_c = (qpos >= blk_end) & (lane < nb)
    off_c = jnp.where(lane < nb, NEG, -jnp.inf)
    any_c = (qpos >= BLOCK - 1).astype(jnp.float32)
    rel_c = (blk_end - t0).astype(jnp.float32)
    ck = ck_ref[0]
    cv = cv_ref[0]
    imp = jnp.zeros((tq, NB_PAD), jnp.float32)
    outs = []
    for r in range(GROUP):
        s = _bdot(qs[r], ck, _NT) + slopes[r] * rel_c
        e, l = _softmax_rows(jnp.where(valid_c, s, off_c))
        p = e / l * any_c
        imp = imp + p
        outs.append(gates[:, r:r + 1] * _bdot(p, cv))

    sel = _select_mask(imp, qpos, nb).astype(jnp.bfloat16)
    kpos = _iota2((1, seq), 1)
    allowed = (_bdot(sel, ex_ref[...]) > 0.5) & (kpos <= qpos)
    mask_s = jnp.where(allowed, 0.0, NEG)
    rel_s = (kpos - t0).astype(jnp.float32)
    ksel = ksel_ref[...]
    vsel = vsel_ref[...]
    for r in range(GROUP):
        s = _bdot(qs[r], ksel, _NT) + slopes[r] * rel_s + mask_s
        e, l = _softmax_rows(s)
        outs[r] = outs[r] + gates[:, GROUP + r:GROUP + r + 1] * (_bdot(e, vsel) / l)

    nw = WINDOW + tq
    start = pl.multiple_of(jnp.maximum(t0 - WINDOW, 0), LANES)
    kwin = kwin_ref[pl.ds(start, nw), :]
    vwin = vwin_ref[pl.ds(start, nw), :]
    kpos_w = start + _iota2((1, nw), 1)
    dist = qpos - kpos_w
    mask_w = jnp.where((dist >= 0) & (dist < WINDOW), 0.0, NEG)
    rel_w = (kpos_w - t0).astype(jnp.float32)
    for r in range(GROUP):
        s = _bdot(qs[r], kwin, _NT) + slopes[r] * rel_w + mask_w
        e, l = _softmax_rows(s)
        o = outs[r] + gates[:, 2 * GROUP + r:2 * GROUP + r + 1] * (_bdot(e, vwin) / l)
        o_ref[:, r * HEAD_DIM:(r + 1) * HEAD_DIM] = o.astype(o_ref.dtype)


def nsa_slopes():
    h = jnp.arange(1, NSA_HEADS + 1, dtype=jnp.float32)
    s = (2.0 ** (-8.0 * h / NSA_HEADS)).reshape(NSA_KV_HEADS, GROUP, 1)
    return jnp.broadcast_to(s, (NSA_KV_HEADS, GROUP, LANES))


def nsa_gate_cols(w_g):
    d = w_g.shape[0]
    w = w_g.reshape(d, 3, NSA_KV_HEADS, GROUP).transpose(0, 2, 1, 3).reshape(d, NSA_KV_HEADS, 3 * GROUP)
    return jnp.pad(w, ((0, 0), (0, 0), (0, LANES - 3 * GROUP))).reshape(d, NSA_KV_HEADS * LANES)


def nsa_prompt(q, gl, ck, cv, kv16, q_norm, *, batch, seq, tq=NSA_TQ):
    nt = seq // tq
    expand = (jnp.arange(NB_PAD)[:, None] == (jnp.arange(seq)[None, :] // BLOCK)).astype(jnp.bfloat16)
    kvw = 2 * NSA_KV_HEADS

    def kvspec(col0):
        return pl.BlockSpec((seq, HEAD_DIM), lambda b, g, i: (b, col0 + g))

    return pl.pallas_call(
        functools.partial(_nsa_prompt_kernel, tq=tq, seq=seq),
        grid=(batch, NSA_KV_HEADS, nt),
        in_specs=[pl.BlockSpec((tq, GROUP * HEAD_DIM), lambda b, g, i: (b * nt + i, g)),
                  pl.BlockSpec((tq, LANES), lambda b, g, i: (b * nt + i, g)),
                  pl.BlockSpec((1, NB_PAD, HEAD_DIM), lambda b, g, i: (b, 0, g)),
                  pl.BlockSpec((1, NB_PAD, HEAD_DIM), lambda b, g, i: (b, 0, g)),
                  kvspec(kvw), kvspec(kvw + NSA_KV_HEADS), kvspec(2 * kvw), kvspec(2 * kvw + NSA_KV_HEADS),
                  pl.BlockSpec((1, HEAD_DIM), lambda b, g, i: (0, 0)),
                  pl.BlockSpec((1, GROUP, LANES), lambda b, g, i: (g, 0, 0)),
                  pl.BlockSpec((NB_PAD, seq), lambda b, g, i: (0, 0))],
        out_specs=pl.BlockSpec((tq, GROUP * HEAD_DIM), lambda b, g, i: (b * nt + i, g)),
        out_shape=jax.ShapeDtypeStruct((batch * seq, NSA_HEADS * HEAD_DIM), jnp.bfloat16),
        compiler_params=_params("parallel", "parallel", "arbitrary"),
        name="nsa_prompt",
    )(q, gl, ck, cv, kv16, kv16, kv16, kv16, q_norm.reshape(1, HEAD_DIM), nsa_slopes(), expand)


KV_SLOTS = 2 * NSA_KV_HEADS
N_PAGES = PAST_LEN // PAGE_SIZE
PAGES_PER_STEP = 4
PAGE_ROWS = PAGE_SIZE * KV_SLOTS
BLOCK_ROWS = BLOCK * KV_SLOTS
BLOCKS_PER_PAGE = PAGE_SIZE // BLOCK
N_PAST_BLOCKS = PAST_LEN // BLOCK
N_SAMPLE_BLOCKS = -(-(PAST_LEN + DEC_SEQ) // BLOCK)
SEL_LANES = -(-N_SAMPLE_BLOCKS // LANES) * LANES
QROWS = GROUP * DEC_SEQ


def _slot_rows(ref, lead, slot, n):
    return ref[lead, pl.ds(slot, n, stride=KV_SLOTS), :]


def _page_specs():
    def spec(j):
        return pl.BlockSpec((1, PAGE_ROWS, HEAD_DIM),
                            lambda b, i, pt: (pt[b * N_PAGES + PAGES_PER_STEP * i + j], 0, 0))
    return [spec(j) for j in range(PAGES_PER_STEP)]


def _paged(cache):
    return [cache.reshape(-1, PAGE_ROWS, HEAD_DIM)] * PAGES_PER_STEP


def _cmp_pool_kernel(pt_ref, p0, p1, p2, p3, w_ref, o_ref):
    i = pl.program_id(1)
    tiles = []
    for p in (p0, p1, p2, p3):
        prod = p[0] * w_ref[...]
        for h in range(BLOCKS_PER_PAGE):
            blk = prod[h * BLOCK_ROWS:(h + 1) * BLOCK_ROWS].reshape(BLOCK, KV_SLOTS, HEAD_DIM)
            tiles.append(jnp.sum(blk, axis=0))
    n = PAGES_PER_STEP * BLOCKS_PER_PAGE * KV_SLOTS
    o_ref[0, pl.ds(pl.multiple_of(i * n, n), n), :] = jnp.concatenate(tiles, axis=0)


def cmp_pool_pages(cache, page_table, cmp_pos_w):
    w = jnp.repeat(jnp.tile(cmp_pos_w, (BLOCKS_PER_PAGE, 1)), NSA_KV_HEADS, axis=1)
    wt = jnp.broadcast_to(w.reshape(PAGE_ROWS, 1), (PAGE_ROWS, HEAD_DIM))
    return pl.pallas_call(
        _cmp_pool_kernel,
        grid_spec=pltpu.PrefetchScalarGridSpec(
            num_scalar_prefetch=1,
            grid=(DEC_BATCH, N_PAGES // PAGES_PER_STEP),
            in_specs=_page_specs() + [pl.BlockSpec((PAGE_ROWS, HEAD_DIM), lambda b, i, pt: (0, 0))],
            out_specs=pl.BlockSpec((1, N_PAST_BLOCKS * KV_SLOTS, HEAD_DIM), lambda b, i, pt: (b, 0, 0))),
        out_shape=jax.ShapeDtypeStruct((DEC_BATCH, N_PAST_BLOCKS * KV_SLOTS, HEAD_DIM), jnp.float32),
        compiler_params=_params("parallel", "arbitrary"),
        name="cmp_pool_pages",
    )(page_table.reshape(-1), *_paged(cache), wt)


def _sample_queries(q_ref, qn_ref, g):
    parts = []
    for r in range(GROUP):
        c0 = (g * GROUP + r) * HEAD_DIM
        x = q_ref[:, c0:c0 + HEAD_DIM]
        x = x * lax.rsqrt(jnp.mean(x * x, axis=-1, keepdims=True) + EPS) * qn_ref[...]
        parts.append(x * SCALE)
    return jnp.concatenate(parts, axis=0).astype(jnp.bfloat16)


def _head_major_col(x, lane0):
    return jnp.concatenate([x[:, lane0 + r:lane0 + r + 1] for r in range(GROUP)], axis=0)


def _slope_col(sl_ref, g):
    return jnp.concatenate([jnp.broadcast_to(sl_ref[g, r:r + 1, 0:1], (DEC_SEQ, 1)) for r in range(GROUP)], axis=0)


def _sample_select_kernel(pooled_ref, wcmp_ref, kn_ref, q_ref, gl_ref, qn_ref, sl_ref, wcache_ref, new_ref, ex_ref,
                          ocw_ref, mask_ref, newmask_ref):
    srow = _iota2((QROWS, 1), 0) % DEC_SEQ
    qpos = PAST_LEN + srow
    gl = gl_ref[...]
    nbp = N_PAST_BLOCKS
    lane_c = _iota2((1, nbp), 1)
    blk_end = (lane_c + 1) * BLOCK - 1
    rel_c = (blk_end - PAST_LEN).astype(jnp.float32)
    nw = wcache_ref.shape[1] // KV_SLOTS
    nwk = nw + LANES
    jw = _iota2((1, nwk), 1)
    kpos_w = jnp.where(jw < nw, PAST_LEN - nw + jw, PAST_LEN + jw - nw)
    dist_w = qpos - kpos_w
    ok_w = (dist_w >= 0) & (dist_w < WINDOW) & (jw < nw + DEC_SEQ)
    mask_w = jnp.where(ok_w, 0.0, NEG)
    rel_w = (kpos_w - PAST_LEN).astype(jnp.float32)
    pad_rows = jnp.zeros((LANES - DEC_SEQ, HEAD_DIM), jnp.float32)
    for g in range(NSA_KV_HEADS):
        ck = _hdot(_slot_rows(pooled_ref, 0, g, nbp), wcmp_ref[0, g])
        ck = ck * lax.rsqrt(jnp.mean(ck * ck, axis=-1, keepdims=True) + EPS) * kn_ref[...]
        cv = _hdot(_slot_rows(pooled_ref, 0, NSA_KV_HEADS + g, nbp), wcmp_ref[1, g])
        qg = _sample_queries(q_ref, qn_ref, g)
        slope = _slope_col(sl_ref, g)
        gates = _sigmoid(gl[:, g * LANES:(g + 1) * LANES])
        s = _bdot(qg, ck, _NT) + slope * rel_c
        e, l = _softmax_rows(jnp.where(qpos >= blk_end, s, NEG))
        p = e / l
        o = _head_major_col(gates, 0) * _bdot(p, cv)
        imp = p[0:DEC_SEQ]
        for r in range(1, GROUP):
            imp = imp + p[r * DEC_SEQ:(r + 1) * DEC_SEQ]
        imp = jnp.concatenate([imp, jnp.zeros((DEC_SEQ, SEL_LANES - nbp), jnp.float32)], axis=1)
        sel = _select_mask(imp, PAST_LEN + _iota2((DEC_SEQ, 1), 0), N_SAMPLE_BLOCKS).astype(jnp.bfloat16)
        for half in range(nbp // LANES):
            keys = _bdot(sel[:, half * LANES:(half + 1) * LANES], ex_ref[...])
            mask_ref[0, g * DEC_SEQ:(g + 1) * DEC_SEQ, half * LANES * BLOCK:(half + 1) * LANES * BLOCK] = keys
        newmask_ref[0, g * DEC_SEQ:(g + 1) * DEC_SEQ, :] = jnp.broadcast_to(
            sel[:, nbp:nbp + 1].astype(jnp.float32), (DEC_SEQ, LANES))
        c0 = 4 * KV_HALF + g * HEAD_DIM
        kw = jnp.concatenate([_slot_rows(wcache_ref, 0, g, nw), new_ref[:, c0:c0 + HEAD_DIM], pad_rows], axis=0)
        vw = jnp.concatenate([_slot_rows(wcache_ref, 0, NSA_KV_HEADS + g, nw),
                              new_ref[:, c0 + KV_HALF:c0 + KV_HALF + HEAD_DIM], pad_rows], axis=0)
        s = _bdot(qg, kw, _NT) + slope * rel_w + mask_w
        e, l = _softmax_rows(s)
        o = o + _head_major_col(gates, 2 * GROUP) * (_bdot(e, vw) / l)
        for r in range(GROUP):
            c0 = (g * GROUP + r) * HEAD_DIM
            ocw_ref[:, c0:c0 + HEAD_DIM] = o[r * DEC_SEQ:(r + 1) * DEC_SEQ]


def nsa_sample_select(pooled, w_cmp, kn_cmp, q, gl, q_norm, cache_win, kv32):
    expand = (jnp.arange(LANES)[:, None] == (jnp.arange(LANES * BLOCK)[None, :] // BLOCK)).astype(jnp.bfloat16)
    row0 = N_PROMPT // DEC_SEQ
    nw = cache_win.shape[1]
    return pl.pallas_call(
        _sample_select_kernel,
        grid=(DEC_BATCH,),
        in_specs=[pl.BlockSpec((1, N_PAST_BLOCKS * KV_SLOTS, HEAD_DIM), lambda b: (b, 0, 0)),
                  pl.BlockSpec((2, NSA_KV_HEADS, HEAD_DIM, HEAD_DIM), lambda b: (0, 0, 0, 0)),
                  pl.BlockSpec((1, HEAD_DIM), lambda b: (0, 0)),
                  pl.BlockSpec((DEC_SEQ, NSA_HEADS * HEAD_DIM), lambda b: (row0 + b, 0)),
                  pl.BlockSpec((DEC_SEQ, NSA_KV_HEADS * LANES), lambda b: (row0 + b, 0)),
                  pl.BlockSpec((1, HEAD_DIM), lambda b: (0, 0)),
                  pl.BlockSpec((NSA_KV_HEADS, GROUP, LANES), lambda b: (0, 0, 0)),
                  pl.BlockSpec((1, nw * KV_SLOTS, HEAD_DIM), lambda b: (b, 0, 0)),
                  pl.BlockSpec((DEC_SEQ, KV_W), lambda b: (row0 + b, 0)),
                  pl.BlockSpec((LANES, LANES * BLOCK), lambda b: (0, 0))],
        out_specs=[pl.BlockSpec((DEC_SEQ, NSA_HEADS * HEAD_DIM), lambda b: (b, 0)),
                   pl.BlockSpec((1, NSA_KV_HEADS * DEC_SEQ, PAST_LEN), lambda b: (b, 0, 0)),
                   pl.BlockSpec((1, NSA_KV_HEADS * DEC_SEQ, LANES), lambda b: (b, 0, 0))],
        out_shape=[jax.ShapeDtypeStruct((N_SAMPLE, NSA_HEADS * HEAD_DIM), jnp.float32),
                   jax.ShapeDtypeStruct((DEC_BATCH, NSA_KV_HEADS * DEC_SEQ, PAST_LEN), jnp.float32),
                   jax.ShapeDtypeStruct((DEC_BATCH, NSA_KV_HEADS * DEC_SEQ, LANES), jnp.float32)],
        compiler_params=_params("parallel"),
        name="nsa_sample_select",
    )(pooled, w_cmp, kn_cmp.reshape(1, HEAD_DIM), q, gl, q_norm.reshape(1, HEAD_DIM), nsa_slopes(),
      cache_win.reshape(DEC_BATCH, nw * KV_SLOTS, HEAD_DIM), kv32, expand)


def _sample_sel_kernel(pt_ref, p0, p1, p2, p3, mask_ref, newmask_ref, q_ref, gl_ref, qn_ref, sl_ref, new_ref, ocw_ref,
                       o_ref, qs, m_sc, l_sc, acc_sc):
    i = pl.program_id(1)
    groups = range(NSA_KV_HEADS)

    @pl.when(i == 0)
    def _():
        for g in groups:
            qs[g] = _sample_queries(q_ref, qn_ref, g)
        m_sc[...] = jnp.full(m_sc.shape, NEG, jnp.float32)
        l_sc[...] = jnp.zeros(l_sc.shape, jnp.float32)
        acc_sc[...] = jnp.zeros(acc_sc.shape, jnp.float32)

    def accumulate(keys, vals, rel, keep8):
        keep = [jnp.concatenate([keep8[g]] * GROUP, axis=0) for g in groups]
        s = [_bdot(qs[g], keys[g], _NT) + _slope_col(sl_ref, g) * rel + jnp.where(keep[g] > 0.5, 0.0, NEG)
             for g in groups]
        m_old = [m_sc[g] for g in groups]
        m_new = [jnp.maximum(m_old[g], jnp.max(s[g], axis=-1, keepdims=True)) for g in groups]
        p = [jnp.exp(s[g] - m_new[g]) * keep[g] for g in groups]
        pv = [_bdot(p[g], vals[g]) for g in groups]
        for g in groups:
            alpha = jnp.exp(m_old[g] - m_new[g])
            l_sc[g] = alpha * l_sc[g] + jnp.sum(p[g], axis=-1, keepdims=True)
            acc_sc[g] = alpha * acc_sc[g] + pv[g]
            m_sc[g] = m_new[g]

    lane = _iota2((1, PAGE_SIZE), 1)
    for j, page in enumerate((p0, p1, p2, p3)):
        keep_all = mask_ref[0, :, j * PAGE_SIZE:(j + 1) * PAGE_SIZE]

        @pl.when(jnp.max(keep_all) > 0.5)
        def _(j=j, page=page, keep_all=keep_all):
            rel = ((i * PAGES_PER_STEP + j) * PAGE_SIZE - PAST_LEN + lane).astype(jnp.float32)
            accumulate([_slot_rows(page, 0, g, PAGE_SIZE) for g in groups],
                       [_slot_rows(page, 0, NSA_KV_HEADS + g, PAGE_SIZE) for g in groups],
                       rel, [keep_all[g * DEC_SEQ:(g + 1) * DEC_SEQ] for g in groups])

    @pl.when(i == pl.num_programs(1) - 1)
    def _():
        gl = gl_ref[...]
        pad_rows = jnp.zeros((PAGE_SIZE - DEC_SEQ, HEAD_DIM), jnp.float32)
        causal = ((lane <= _iota2((DEC_SEQ, 1), 0)) & (lane < DEC_SEQ)).astype(jnp.float32)
        c0 = 2 * KV_HALF
        accumulate([jnp.concatenate([new_ref[:, c0 + g * HEAD_DIM:c0 + (g + 1) * HEAD_DIM], pad_rows], axis=0)
                    for g in groups],
                   [jnp.concatenate([new_ref[:, c0 + KV_HALF + g * HEAD_DIM:c0 + KV_HALF + (g + 1) * HEAD_DIM], pad_rows],
                                    axis=0) for g in groups],
                   lane.astype(jnp.float32),
                   [newmask_ref[0, g * DEC_SEQ:(g + 1) * DEC_SEQ, :] * causal for g in groups])
        for g in groups:
            gates = _sigmoid(gl[:, g * LANES:(g + 1) * LANES])
            o = _head_major_col(gates, GROUP) * (acc_sc[g] / l_sc[g])
            for r in range(GROUP):
                c0 = (g * GROUP + r) * HEAD_DIM
                o_ref[:, c0:c0 + HEAD_DIM] = ocw_ref[:, c0:c0 + HEAD_DIM] + o[r * DEC_SEQ:(r + 1) * DEC_SEQ]


def nsa_sample_sel(cache, page_table, mask, newmask, q, gl, q_norm, kv32, ocw):
    row0 = N_PROMPT // DEC_SEQ
    nrow = NSA_KV_HEADS * DEC_SEQ
    step_keys = PAGES_PER_STEP * PAGE_SIZE
    return pl.pallas_call(
        _sample_sel_kernel,
        grid_spec=pltpu.PrefetchScalarGridSpec(
            num_scalar_prefetch=1,
            grid=(DEC_BATCH, N_PAGES // PAGES_PER_STEP),
            in_specs=_page_specs() + [
                pl.BlockSpec((1, nrow, step_keys), lambda b, i, pt: (b, 0, i)),
                pl.BlockSpec((1, nrow, LANES), lambda b, i, pt: (b, 0, 0)),
                pl.BlockSpec((DEC_SEQ, NSA_HEADS * HEAD_DIM), lambda b, i, pt: (row0 + b, 0)),
                pl.BlockSpec((DEC_SEQ, NSA_KV_HEADS * LANES), lambda b, i, pt: (row0 + b, 0)),
                pl.BlockSpec((1, HEAD_DIM), lambda b, i, pt: (0, 0)),
                pl.BlockSpec((NSA_KV_HEADS, GROUP, LANES), lambda b, i, pt: (0, 0, 0)),
                pl.BlockSpec((DEC_SEQ, KV_W), lambda b, i, pt: (row0 + b, 0)),
                pl.BlockSpec((DEC_SEQ, NSA_HEADS * HEAD_DIM), lambda b, i, pt: (b, 0))],
            out_specs=pl.BlockSpec((DEC_SEQ, NSA_HEADS * HEAD_DIM), lambda b, i, pt: (b, 0)),
            scratch_shapes=[pltpu.VMEM((NSA_KV_HEADS, QROWS, HEAD_DIM), jnp.bfloat16),
                            pltpu.VMEM((NSA_KV_HEADS, QROWS, 1), jnp.float32),
                            pltpu.VMEM((NSA_KV_HEADS, QROWS, 1), jnp.float32),
                            pltpu.VMEM((NSA_KV_HEADS, QROWS, HEAD_DIM), jnp.float32)]),
        out_shape=jax.ShapeDtypeStruct((N_SAMPLE, NSA_HEADS * HEAD_DIM), jnp.float32),
        compiler_params=_params("parallel", "arbitrary"),
        name="nsa_sample_sel",
    )(page_table.reshape(-1), *_paged(cache), mask, newmask,
      q, gl, q_norm.reshape(1, HEAD_DIM), nsa_slopes(), kv32, ocw)


def nsa_sample(q, gl, kv32, cache_cmp_kv, cache_sel_kv, cache_win_kv, page_table, cmp_pos_w, w_cmp, kn_cmp, q_norm):
    pooled = cmp_pool_pages(cache_cmp_kv, page_table, cmp_pos_w)
    ocw, mask, newmask = nsa_sample_select(pooled, w_cmp, kn_cmp, q, gl, q_norm, cache_win_kv, kv32)
    return nsa_sample_sel(cache_sel_kv, page_table, mask, newmask, q, gl, q_norm, kv32, ocw)


def _split_rows(x):
    return (x[:N_PROMPT].reshape((BATCH, SEQ) + x.shape[1:]),
            x[N_PROMPT:].reshape((DEC_BATCH, DEC_SEQ) + x.shape[1:]))


def kernel(x_prompt, x_sample, state_conv, state_delta, cache_cmp_kv, cache_sel_kv, cache_win_kv, page_table,
           ffn_norm, ffn_w_gate, ffn_w_up, ffn_w_down, mix_norm,
           gdn_w_in, gdn_conv_w, gdn_a_log, gdn_dt_bias, gdn_o_norm, gdn_w_out,
           kv_norm, w_kv, cmp_pos_w, w_cmp, k_norm, nsa_w_q, nsa_q_norm, nsa_w_o):
    bf = jnp.bfloat16
    n_main = GDN_CONV_CH + GDN_V
    n_q = NSA_HEADS * HEAD_DIM
    hist = CONV_W - 1
    wd16 = ffn_w_down.astype(bf)

    def ffn_half(rows, layer, i):
        h, h16, ssq = rows
        act = swiglu_up(h16, ssq, ffn_norm[layer, i], ffn_w_gate, ffn_w_up, (layer, i))
        return matmul_residual(act, wd16, (layer, i), h, 0.5, tn=256, tm=DOWN_ROWS)

    rows = ffn_half(rows_prep(x_prompt.reshape(N_PROMPT, D_MODEL), x_sample.reshape(N_SAMPLE, D_MODEL)), 0, 0)
    h, h16, ssq = rows
    proj = matmul_normed(h16, ssq, mix_norm[0], gdn_w_in, (0,), n_cols=n_main, tn=512)
    ab = matmul_normed(h16, ssq, mix_norm[0], gdn_group_cols(gdn_w_in[0, :, n_main:]), tn=LANES)
    conv_w8 = jnp.pad(gdn_conv_w[0], ((0, CONV_PAD - CONV_W), (0, 0)))
    hp = gdn_head_params(gdn_a_log[0], gdn_dt_bias[0])

    def pad_sample(x):
        x = x[N_PROMPT:].reshape(DEC_BATCH, DEC_SEQ, x.shape[1])
        return jnp.pad(x, ((0, 0), (0, GDN_CHUNK - DEC_SEQ), (0, 0))).reshape(DEC_BATCH * GDN_CHUNK, x.shape[2])

    o_p, delta_p = gdn_mixer(proj, ab, jnp.zeros((BATCH, CONV_PAD, GDN_CONV_CH), jnp.float32),
                             jnp.zeros((BATCH, GDN_HEADS, GDN_DK, GDN_DV), jnp.float32),
                             conv_w8, hp, gdn_o_norm[0], batch=BATCH, n_chunks=SEQ // GDN_CHUNK)
    o_s, delta_s = gdn_mixer(pad_sample(proj), pad_sample(ab),
                             jnp.pad(state_conv[0], ((0, 0), (CONV_PAD - hist, 0), (0, 0))), state_delta[0],
                             conv_w8, hp, gdn_o_norm[0], batch=DEC_BATCH, n_chunks=1, valid_rows=DEC_SEQ)
    o_s = o_s.reshape(DEC_BATCH, GDN_CHUNK, GDN_V)[:, :DEC_SEQ].reshape(N_SAMPLE, GDN_V)
    conv_p = jnp.stack([lax.slice(proj, ((b + 1) * SEQ - hist, 0), ((b + 1) * SEQ, GDN_CONV_CH)) for b in range(BATCH)])
    u_s = lax.slice(proj, (N_PROMPT, 0), (N_ROWS, GDN_CONV_CH)).reshape(DEC_BATCH, DEC_SEQ, GDN_CONV_CH)
    conv_s = jnp.concatenate([state_conv[0], u_s], axis=1)[:, DEC_SEQ:]
    rows = matmul_residual(jnp.concatenate([o_p, o_s], axis=0), gdn_w_out, (0,), h, 1.0, tn=256, tm=MM_ROWS)
    rows = ffn_half(rows, 0, 1)

    h, h16, ssq = rows
    kv32, kv16 = kv_finish(matmul_normed(h16, ssq, kv_norm, w_kv, tn=512), k_norm)
    kv5 = kv32.reshape(N_ROWS, 3, 2, NSA_KV_HEADS, HEAD_DIM)
    cmp_p, cmp_s = _split_rows(kv5[:, 0])
    sel_p, sel_s = _split_rows(kv5[:, 1])
    win_rows_p, win_rows_s = _split_rows(kv5[:, 2])

    h, h16, ssq = ffn_half(rows, 1, 0)
    q = matmul_normed(h16, ssq, mix_norm[1], nsa_w_q, (0,), n_cols=n_q, tn=512)
    gl = matmul_normed(h16, ssq, mix_norm[1], nsa_gate_cols(nsa_w_q[0, :, n_q:]), tn=LANES)
    ck, cv = compress_prompt(kv32, cmp_pos_w, w_cmp, k_norm[0], batch=BATCH, seq=SEQ)
    o_p = nsa_prompt(q, gl, ck, cv, kv16, nsa_q_norm[0], batch=BATCH, seq=SEQ)
    o_s = nsa_sample(q, gl, kv32, cache_cmp_kv, cache_sel_kv, cache_win_kv, page_table, cmp_pos_w, w_cmp, k_norm[0],
                     nsa_q_norm[0])
    rows = matmul_residual(jnp.concatenate([o_p, o_s.astype(bf)], axis=0), nsa_w_o, (0,), h, 1.0, tn=256, tm=MM_ROWS)
    h, _, _ = ffn_half(rows, 1, 1)

    y_p, y_s = _split_rows(h)
    win_p = win_rows_p[:, -min(WINDOW, SEQ):]
    win_s = jnp.concatenate([cache_win_kv, win_rows_s], axis=1)[:, DEC_SEQ:]
    return (y_p, y_s, conv_p[None], conv_s[None], delta_p[None], delta_s[None],
            cmp_p, cmp_s, sel_p, sel_s, win_p, win_s)
```

```python
import functools
import math

import jax
import jax.numpy as jnp
from jax import lax
from jax.experimental import pallas as pl
from jax.experimental.pallas import tpu as pltpu

D_MODEL = 4096
BATCH = 4
SEQ = 2048
DEPTH = 2
DEC_BATCH = 8
DEC_SEQ = 8
PAST_LEN = 16384
PAGE_SIZE = 128
D_FF = 11008
EPS = 1e-6
GDN_HEADS = 16
GDN_DK = 128
GDN_DV = 256
CONV_W = 4
GDN_CHUNK = 64
GDN_QK = GDN_HEADS * GDN_DK
GDN_V = GDN_HEADS * GDN_DV
GDN_CONV_CH = 2 * GDN_QK + GDN_V
NSA_HEADS = 32
NSA_KV_HEADS = 4
HEAD_DIM = 128
GROUP = NSA_HEADS // NSA_KV_HEADS
BLOCK = 64
N_SELECT = 16
N_LOCAL = 2
WINDOW = 512
SCALE = HEAD_DIM ** -0.5
NEG = -1e30
LOG2E = 1.0 / math.log(2.0)
FORCE = 1e4

N_PROMPT = BATCH * SEQ
N_SAMPLE = DEC_BATCH * DEC_SEQ
N_ROWS = N_PROMPT + N_SAMPLE

VMEM_LIMIT_BYTES = 56 * 1024 * 1024
LANES = 128
SUBLANES = 8

NORM_ROWS = 192
MM_ROWS = 1376
DOWN_ROWS = 688


def _params(*sem):
    return pltpu.CompilerParams(dimension_semantics=sem, vmem_limit_bytes=VMEM_LIMIT_BYTES)


PREP_ROWS = 64


def _lane_fold(x):
    acc = x[:, 0:LANES]
    for c in range(1, x.shape[1] // LANES):
        acc = acc + x[:, c * LANES:(c + 1) * LANES]
    return acc


def _rows_prep_kernel(xp_ref, xs_ref, g_ref, h_ref, h16_ref, ssq_ref):
    n_prompt_tiles = N_PROMPT // PREP_ROWS

    def emit(x):
        h_ref[...] = x
        h16_ref[...] = (x * g_ref[...]).astype(h16_ref.dtype)
        ssq_ref[...] = _lane_fold(x * x)

    @pl.when(pl.program_id(0) < n_prompt_tiles)
    def _():
        emit(xp_ref[...])

    @pl.when(pl.program_id(0) >= n_prompt_tiles)
    def _():
        emit(xs_ref[...])


def rows_prep(x_prompt, x_sample, gain):
    d = x_prompt.shape[1]
    n_p = N_PROMPT // PREP_ROWS
    n = N_ROWS // PREP_ROWS
    return pl.pallas_call(
        _rows_prep_kernel,
        grid=(n,),
        in_specs=[pl.BlockSpec((PREP_ROWS, d), lambda i: (jnp.minimum(i, n_p - 1), 0)),
                  pl.BlockSpec((PREP_ROWS, d), lambda i: (jnp.maximum(i - n_p, 0), 0)),
                  pl.BlockSpec((1, d), lambda i: (0, 0))],
        out_specs=[pl.BlockSpec((PREP_ROWS, d), lambda i: (i, 0)),
                   pl.BlockSpec((PREP_ROWS, d), lambda i: (i, 0)),
                   pl.BlockSpec((PREP_ROWS, LANES), lambda i: (i, 0))],
        out_shape=[jax.ShapeDtypeStruct((N_ROWS, d), jnp.float32),
                   jax.ShapeDtypeStruct((N_ROWS, d), jnp.bfloat16),
                   jax.ShapeDtypeStruct((N_ROWS, LANES), jnp.float32)],
        compiler_params=_params("parallel"),
        name="rows_prep",
    )(x_prompt, x_sample, gain.reshape(1, d))


def _row_scale(ssq_ref, k):
    return lax.rsqrt(jnp.sum(ssq_ref[...], axis=-1, keepdims=True) / k + EPS)


def _weight_spec(w, lead, tn):
    k = w.shape[-2]
    return pl.BlockSpec((None,) * len(lead) + (k, tn), lambda i, j: tuple(lead) + (0, j))


def _normed_specs(tm, k):
    return [pl.BlockSpec((tm, k), lambda i, j: (i, 0)),
            pl.BlockSpec((tm, LANES), lambda i, j: (i, 0))]


def _mm_kernel(x_ref, ssq_ref, w_ref, o_ref):
    acc = jnp.dot(x_ref[...], w_ref[...].astype(jnp.bfloat16), preferred_element_type=jnp.float32)
    o_ref[...] = (_row_scale(ssq_ref, x_ref.shape[1]) * acc).astype(o_ref.dtype)


def matmul_normed(xg16, ssq, w, lead=(), *, tn, n_cols=None, tm=MM_ROWS):
    m, k = xg16.shape
    n = n_cols or w.shape[-1]
    return pl.pallas_call(
        _mm_kernel,
        grid=(m // tm, n // tn),
        in_specs=_normed_specs(tm, k) + [_weight_spec(w, lead, tn)],
        out_specs=pl.BlockSpec((tm, tn), lambda i, j: (i, j)),
        out_shape=jax.ShapeDtypeStruct((m, n), jnp.float32),
        compiler_params=_params("parallel", "arbitrary"),
        name="matmul_normed",
    )(xg16, ssq, w)


def _mm_res_kernel(x_ref, w_ref, r_ref, *refs, scale, n_gains):
    gain_refs, o_ref, o16_refs = refs[:n_gains], refs[n_gains], refs[n_gains + 1:2 * n_gains + 1]
    w = w_ref[...].astype(jnp.bfloat16)
    h = r_ref[...] + scale * jnp.dot(x_ref[...], w, preferred_element_type=jnp.float32)
    o_ref[...] = h
    for g_ref, o16_ref in zip(gain_refs, o16_refs):
        o16_ref[...] = (h * g_ref[...]).astype(o16_ref.dtype)
    if n_gains:
        ssq_ref = refs[2 * n_gains + 1]
        part = _lane_fold(h * h)

        @pl.when(pl.program_id(1) == 0)
        def _():
            ssq_ref[...] = part

        @pl.when(pl.program_id(1) > 0)
        def _():
            ssq_ref[...] += part


def matmul_residual(x, w, lead, res, scale, gains, *, tn, tm):
    m, k = x.shape
    n = w.shape[-1]
    ng = len(gains)
    tile = pl.BlockSpec((tm, tn), lambda i, j: (i, j))
    out = pl.pallas_call(
        functools.partial(_mm_res_kernel, scale=scale, n_gains=ng),
        grid=(m // tm, n // tn),
        in_specs=[pl.BlockSpec((tm, k), lambda i, j: (i, 0)), _weight_spec(w, lead, tn), tile]
                 + [pl.BlockSpec((1, tn), lambda i, j: (0, j))] * ng,
        out_specs=[tile] * (1 + ng) + [pl.BlockSpec((tm, LANES), lambda i, j: (i, 0))] * min(ng, 1),
        out_shape=[jax.ShapeDtypeStruct((m, n), jnp.float32)] + [jax.ShapeDtypeStruct((m, n), jnp.bfloat16)] * ng
                  + [jax.ShapeDtypeStruct((m, LANES), jnp.float32)] * min(ng, 1),
        compiler_params=_params("parallel", "arbitrary"),
        name="matmul_residual",
    )(x, w, res, *[g.reshape(1, n) for g in gains])
    return out[0], list(out[1:1 + ng]), (out[1 + ng] if ng else None)


def _swiglu_up_kernel(x_ref, ssq_ref, wg_ref, wu_ref, o_ref):
    x = x_ref[...]
    r = _row_scale(ssq_ref, x.shape[1])
    g = r * jnp.dot(x, wg_ref[...].astype(jnp.bfloat16), preferred_element_type=jnp.float32)
    u = r * jnp.dot(x, wu_ref[...].astype(jnp.bfloat16), preferred_element_type=jnp.float32)
    o_ref[...] = (g * jax.nn.sigmoid(g) * u).astype(o_ref.dtype)


def swiglu_up(xg16, ssq, wg, wu, lead, tn=256, tm=MM_ROWS):
    m, k = xg16.shape
    n = wg.shape[-1]
    return pl.pallas_call(
        _swiglu_up_kernel,
        grid=(m // tm, n // tn),
        in_specs=_normed_specs(tm, k) + [_weight_spec(wg, lead, tn), _weight_spec(wu, lead, tn)],
        out_specs=pl.BlockSpec((tm, tn), lambda i, j: (i, j)),
        out_shape=jax.ShapeDtypeStruct((m, n), jnp.bfloat16),
        compiler_params=_params("parallel", "arbitrary"),
        name="swiglu_up",
    )(xg16, ssq, wg, wu)


_HI = lax.Precision.HIGHEST
_NT = (((1,), (1,)), ((), ()))
_TN = (((0,), (0,)), ((), ()))


def _bdot(a, b, dims=None):
    a = a.astype(jnp.bfloat16)
    b = b.astype(jnp.bfloat16)
    if dims is None:
        return jnp.dot(a, b, preferred_element_type=jnp.float32)
    return lax.dot_general(a, b, dims, preferred_element_type=jnp.float32)


def _hdot(a, b, dims=None):
    if dims is None:
        return jnp.dot(a, b, precision=_HI, preferred_element_type=jnp.float32)
    return lax.dot_general(a, b, dims, precision=_HI, preferred_element_type=jnp.float32)


def _iota2(shape, axis):
    return lax.broadcasted_iota(jnp.int32, shape, axis)


def _sigmoid(x):
    return 1.0 / (1.0 + jnp.exp(-x))


def _softmax_rows(s, exp=jnp.exp):
    m = jnp.max(s, axis=-1, keepdims=True)
    e = exp(s - m)
    return e, jnp.sum(e, axis=-1, keepdims=True)


GDN_HB = 16
CONV_PAD = SUBLANES


def _gdn_kernel(uq_ref, uk_ref, uv_ref, gate_ref, ab_ref, cq_ref, ck_ref, cv_ref, s0_ref,
                wq_ref, wk_ref, wv_ref, hp_ref, onorm_ref,
                o_ref, s_ref, extq, extk, extv, *, hb, chunk, valid_rows):
    C = chunk
    c = pl.program_id(2)

    @pl.when(c == 0)
    def _():
        s_ref[...] = s0_ref[...]
        extq[0:CONV_PAD, :] = cq_ref[0]
        extk[0:CONV_PAD, :] = ck_ref[0]
        extv[0:CONV_PAD, :] = cv_ref[0]

    def conv(ext, u_ref, w_ref):
        ext[CONV_PAD:CONV_PAD + C, :] = u_ref[...]
        base = CONV_PAD - (CONV_W - 1)
        acc = ext[base:base + C, :] * w_ref[0:1, :]
        for i in range(1, CONV_W):
            acc = acc + ext[base + i:base + i + C, :] * w_ref[i:i + 1, :]
        ext[0:CONV_PAD, :] = ext[C:C + CONV_PAD, :]
        return acc * _sigmoid(acc)

    qc = conv(extq, uq_ref, wq_ref)
    kc = conv(extk, uk_ref, wk_ref)
    vc = conv(extv, uv_ref, wv_ref)

    row = _iota2((C, C), 0)
    col = _iota2((C, C), 1)
    tri_incl = row >= col
    tri_strict = row > col
    row_ok = None
    if valid_rows < C:
        row_ok = _iota2((C, 1), 0) < valid_rows

    ab = ab_ref[...]
    x = ab + hp_ref[0, 1:2, :]
    softplus = jnp.maximum(x, 0.0) + jnp.log(1.0 + jnp.exp(-jnp.abs(x)))
    gmat = -jnp.exp(hp_ref[0, 0:1, :]) * softplus
    if row_ok is not None:
        gmat = jnp.where(row_ok, gmat, 0.0)
    beta = _sigmoid(ab)
    gam = _hdot(tri_incl.astype(jnp.float32), gmat)
    eye_l = (_iota2((LANES, LANES), 0) == _iota2((LANES, LANES), 1)).astype(jnp.float32)
    gam_t = _hdot(eye_l, gam, _NT)

    heads = range(hb)
    gc = [gam[:, h:h + 1] for h in heads]
    bc = [beta[:, hb + h:hb + h + 1] for h in heads]
    glast = [gam[C - 1:C, h:h + 1] for h in heads]
    q, k, v, decay = [], [], [], []
    for h in heads:
        qh = qc[:, h * GDN_DK:(h + 1) * GDN_DK]
        kh = kc[:, h * GDN_DK:(h + 1) * GDN_DK]
        vh = vc[:, h * GDN_DV:(h + 1) * GDN_DV]
        qh = qh * lax.rsqrt(jnp.sum(qh * qh, axis=-1, keepdims=True) + EPS) * GDN_DK ** -0.5
        kh = kh * lax.rsqrt(jnp.sum(kh * kh, axis=-1, keepdims=True) + EPS)
        if row_ok is not None:
            qh = jnp.where(row_ok, qh, 0.0)
            kh = jnp.where(row_ok, kh, 0.0)
            vh = jnp.where(row_ok, vh, 0.0)
        q.append(qh)
        k.append(kh)
        v.append(vh)
        decay.append(jnp.exp(jnp.where(tri_incl, gc[h] - gam_t[h:h + 1, :], -jnp.inf)))
    kk = [_bdot(k[h], k[h], _NT) for h in heads]
    qk = [_bdot(q[h], k[h], _NT) for h in heads]
    s_old = [s_ref[0, h] for h in heads]
    kq_s = [_bdot(jnp.concatenate([k[h], q[h]], axis=0), s_old[h]) for h in heads]
    pw = [jnp.where(tri_strict, decay[h] * kk[h], 0.0) * bc[h] for h in heads]
    nil = [-pw[h] for h in heads]
    for _ in range(int(math.log2(C)) - 1):
        pw = [_bdot(pw[h], pw[h]) for h in heads]
        nil = [nil[h] + pw[h] + _bdot(nil[h], pw[h]) for h in heads]
    gt = [jnp.exp(gc[h]) for h in heads]
    rhs = [bc[h] * (v[h] - gt[h] * kq_s[h][:C]) for h in heads]
    u = [rhs[h] + _bdot(nil[h], rhs[h]) for h in heads]
    o = [gt[h] * kq_s[h][C:] + _bdot(decay[h] * qk[h], u[h]) for h in heads]
    for h in heads:
        s_ref[0, h] = jnp.exp(glast[h]) * s_old[h] + _bdot(k[h] * jnp.exp(glast[h] - gc[h]), u[h], _TN)
    for h in heads:
        on = o[h] * lax.rsqrt(jnp.mean(o[h] * o[h], axis=-1, keepdims=True) + EPS) * onorm_ref[...]
        gate = gate_ref[:, h * GDN_DV:(h + 1) * GDN_DV]
        o_ref[:, h * GDN_DV:(h + 1) * GDN_DV] = (on * gate * _sigmoid(gate)).astype(o_ref.dtype)


def gdn_mixer(proj, ab, conv_init, s0, conv_w8, hp, o_norm, *, batch, n_chunks, valid_rows=GDN_CHUNK, hb=GDN_HB):
    C = GDN_CHUNK
    ng = GDN_HEADS // hb
    rows = batch * n_chunks * C
    qw, vw = hb * GDN_DK, hb * GDN_DV

    def rowblk(off):
        return lambda b, g, c: (b * n_chunks + c, off + g)

    def fixed3(off):
        return lambda b, g, c: (b, 0, off + g)

    def wblk(off):
        return lambda b, g, c: (0, off + g)

    in_specs = [
        pl.BlockSpec((C, qw), rowblk(0)),
        pl.BlockSpec((C, qw), rowblk(ng)),
        pl.BlockSpec((C, vw), rowblk(ng)),
        pl.BlockSpec((C, vw), rowblk(2 * ng)),
        pl.BlockSpec((C, LANES), rowblk(0)),
        pl.BlockSpec((1, CONV_PAD, qw), fixed3(0)),
        pl.BlockSpec((1, CONV_PAD, qw), fixed3(ng)),
        pl.BlockSpec((1, CONV_PAD, vw), fixed3(ng)),
        pl.BlockSpec((1, hb, GDN_DK, GDN_DV), lambda b, g, c: (b, g, 0, 0)),
        pl.BlockSpec((CONV_PAD, qw), wblk(0)),
        pl.BlockSpec((CONV_PAD, qw), wblk(ng)),
        pl.BlockSpec((CONV_PAD, vw), wblk(ng)),
        pl.BlockSpec((1, CONV_PAD, LANES), lambda b, g, c: (g, 0, 0)),
        pl.BlockSpec((1, GDN_DV), lambda b, g, c: (0, 0)),
    ]
    out_specs = [
        pl.BlockSpec((C, vw), rowblk(0)),
        pl.BlockSpec((1, hb, GDN_DK, GDN_DV), lambda b, g, c: (b, g, 0, 0)),
    ]
    return pl.pallas_call(
        functools.partial(_gdn_kernel, hb=hb, chunk=C, valid_rows=valid_rows),
        grid=(batch, ng, n_chunks),
        in_specs=in_specs,
        out_specs=out_specs,
        out_shape=[jax.ShapeDtypeStruct((rows, GDN_V), jnp.bfloat16),
                   jax.ShapeDtypeStruct((batch, GDN_HEADS, GDN_DK, GDN_DV), jnp.float32)],
        scratch_shapes=[pltpu.VMEM((CONV_PAD + C, qw), jnp.float32),
                        pltpu.VMEM((CONV_PAD + C, qw), jnp.float32),
                        pltpu.VMEM((CONV_PAD + C, vw), jnp.float32)],
        compiler_params=_params("parallel", "parallel", "arbitrary"),
        name="gdn_mixer",
    )(proj, proj, proj, proj, ab, conv_init, conv_init, conv_init, s0,
      conv_w8, conv_w8, conv_w8, hp, o_norm.reshape(1, GDN_DV))


def gdn_group_cols(w_ab, hb=GDN_HB):
    d = w_ab.shape[0]
    ng = GDN_HEADS // hb
    a = w_ab[:, :GDN_HEADS].reshape(d, ng, hb)
    b = w_ab[:, GDN_HEADS:].reshape(d, ng, hb)
    blk = jnp.concatenate([a, b, jnp.zeros((d, ng, LANES - 2 * hb), w_ab.dtype)], axis=-1)
    return blk.reshape(d, ng * LANES)


def gdn_head_params(a_log, dt_bias, hb=GDN_HB):
    ng = GDN_HEADS // hb
    rows = jnp.stack([a_log.reshape(ng, hb), dt_bias.reshape(ng, hb)], axis=1)
    return jnp.pad(rows, ((0, 0), (0, CONV_PAD - 2), (0, LANES - hb)))


KV_W = 3 * 2 * NSA_KV_HEADS * HEAD_DIM
KV_HALF = NSA_KV_HEADS * HEAD_DIM
N_KV_COLS = KV_W // HEAD_DIM
NB_PAD = LANES
_NORMED_KV_COLS = tuple(range(2 * NSA_KV_HEADS, 3 * NSA_KV_HEADS)) + tuple(range(4 * NSA_KV_HEADS, 5 * NSA_KV_HEADS))


def _kv_finish_kernel(kv_ref, gain_ref, o32_ref, o16_ref):
    for j in range(N_KV_COLS):
        sl = slice(j * HEAD_DIM, (j + 1) * HEAD_DIM)
        x = kv_ref[:, sl]
        if j in _NORMED_KV_COLS:
            x = x * lax.rsqrt(jnp.mean(x * x, axis=-1, keepdims=True) + EPS) * gain_ref[0:1, sl]
        o32_ref[:, sl] = x
        o16_ref[:, sl] = x.astype(o16_ref.dtype)


def kv_finish(kv, k_norm):
    n = kv.shape[0]
    ones = jnp.ones((KV_HALF,), jnp.float32)
    gain = jnp.concatenate([ones, ones, jnp.tile(k_norm[1], NSA_KV_HEADS), ones,
                            jnp.tile(k_norm[2], NSA_KV_HEADS), ones])
    gain = jnp.broadcast_to(gain[None], (SUBLANES, KV_W))
    return pl.pallas_call(
        _kv_finish_kernel,
        grid=(n // NORM_ROWS,),
        in_specs=[pl.BlockSpec((NORM_ROWS, KV_W), lambda i: (i, 0)),
                  pl.BlockSpec((SUBLANES, KV_W), lambda i: (0, 0))],
        out_specs=[pl.BlockSpec((NORM_ROWS, KV_W), lambda i: (i, 0)),
                   pl.BlockSpec((NORM_ROWS, KV_W), lambda i: (i, 0))],
        out_shape=[jax.ShapeDtypeStruct((n, KV_W), jnp.float32),
                   jax.ShapeDtypeStruct((n, KV_W), jnp.bfloat16)],
        compiler_params=_params("parallel"),
        name="kv_finish",
    )(kv, gain)


def _compress_prompt_kernel(rows_ref, pw_ref, wcmp_ref, kn_ref, ck_ref, cv_ref, *, nb):
    pooled_k = _hdot(pw_ref[0], rows_ref[:, 0:KV_HALF])
    pooled_v = _hdot(pw_ref[1], rows_ref[:, KV_HALF:2 * KV_HALF])
    ck_ref[...] = jnp.zeros_like(ck_ref)
    cv_ref[...] = jnp.zeros_like(cv_ref)
    for g in range(NSA_KV_HEADS):
        sl = slice(g * HEAD_DIM, (g + 1) * HEAD_DIM)
        k = _hdot(pooled_k[:, sl], wcmp_ref[0, g])
        k = k * lax.rsqrt(jnp.mean(k * k, axis=-1, keepdims=True) + EPS) * kn_ref[...]
        v = _hdot(pooled_v[:, sl], wcmp_ref[1, g])
        ck_ref[0, 0:nb, sl] = k.astype(ck_ref.dtype)
        cv_ref[0, 0:nb, sl] = v.astype(cv_ref.dtype)


def compress_prompt(kv32, cmp_pos_w, w_cmp, kn_cmp, *, batch, seq):
    nb = seq // BLOCK
    pw = jnp.einsum('nm,jc->cnmj', jnp.eye(nb, dtype=jnp.float32), cmp_pos_w).reshape(2, nb, seq)
    shape = jax.ShapeDtypeStruct((batch, NB_PAD, KV_HALF), jnp.bfloat16)
    return pl.pallas_call(
        functools.partial(_compress_prompt_kernel, nb=nb),
        grid=(batch,),
        in_specs=[pl.BlockSpec((seq, 2 * KV_HALF), lambda b: (b, 0)),
                  pl.BlockSpec((2, nb, seq), lambda b: (0, 0, 0)),
                  pl.BlockSpec((2, NSA_KV_HEADS, HEAD_DIM, HEAD_DIM), lambda b: (0, 0, 0, 0)),
                  pl.BlockSpec((1, HEAD_DIM), lambda b: (0, 0))],
        out_specs=[pl.BlockSpec((1, NB_PAD, KV_HALF), lambda b: (b, 0, 0)),
                   pl.BlockSpec((1, NB_PAD, KV_HALF), lambda b: (b, 0, 0))],
        out_shape=[shape, shape],
        compiler_params=_params("parallel"),
        name="compress_prompt",
    )(kv32, pw, w_cmp, kn_cmp.reshape(1, HEAD_DIM))


NSA_TQ = 256
NSA_PROMPT_CALLS = 4


def _select_mask(imp, qpos, nb):
    lane = _iota2(imp.shape, 1)
    cur = qpos // BLOCK
    causal = lane <= cur
    forced = (lane == 0) | (causal & (lane > cur - N_LOCAL))
    score = jnp.where(forced, FORCE, jnp.where(causal, imp, -1.0))
    score = jnp.where(lane < nb, score, -2.0)
    rank = jnp.zeros(imp.shape, jnp.float32)
    for j in range(nb):
        cj = score[:, j:j + 1]
        rank = rank + jnp.where((cj > score) | ((cj == score) & (lane > j)), 1.0, 0.0)
    return (rank < float(min(N_SELECT, nb))) & (lane < nb)


def _nsa_prompt_kernel(q_ref, gl_ref, ck_ref, cv_ref, ksel_ref, vsel_ref, kwin_ref, vwin_ref,
                       qn_ref, sl_ref, ex_ref, *rest, tq, seq, tile0, n_keys):
    o_ref = rest[-1]
    nb = seq // BLOCK
    t0 = (tile0 + pl.program_id(2)) * tq
    qpos = t0 + _iota2((tq, 1), 0)
    gates = _sigmoid(gl_ref[...])
    qs = []
    for r in range(GROUP):
        x = q_ref[:, r * HEAD_DIM:(r + 1) * HEAD_DIM]
        x = x * lax.rsqrt(jnp.mean(x * x, axis=-1, keepdims=True) + EPS) * qn_ref[...]
        qs.append((x * (SCALE * LOG2E)).astype(jnp.bfloat16))
    slopes = [sl_ref[0, r:r + 1, 0:1] * LOG2E for r in range(GROUP)]

    lane = _iota2((1, NB_PAD), 1)
    blk_end = (lane + 1) * BLOCK - 1
    valid_c = (qpos >= blk_end) & (lane < nb)
    off_c = jnp.where(lane < nb, NEG, -jnp.inf)
    any_c = (qpos >= BLOCK - 1).astype(jnp.float32)
    rel_c = (blk_end - t0).astype(jnp.float32)
    ck = ck_ref[0]
    cv = cv_ref[0]
    imp = jnp.zeros((tq, NB_PAD), jnp.float32)
    outs = []
    for r in range(GROUP):
        s = _bdot(qs[r], ck, _NT) + slopes[r] * rel_c
        e, l = _softmax_rows(jnp.where(valid_c, s, off_c), jnp.exp2)
        p = e / l * any_c
        imp = imp + p
        outs.append(gates[:, r:r + 1] * _bdot(p, cv))

    sel = _select_mask(imp, qpos, nb).astype(jnp.bfloat16)
    kpos = _iota2((1, n_keys), 1)
    allowed = (_bdot(sel, ex_ref[:, 0:n_keys]) > 0.5) & (kpos <= qpos)
    mask_s = jnp.where(allowed, 0.0, NEG)
    rel_s = (kpos - t0).astype(jnp.float32)
    ksel = ksel_ref[0:n_keys, :]
    vsel = vsel_ref[0:n_keys, :]
    for r in range(GROUP):
        s = _bdot(qs[r], ksel, _NT) + slopes[r] * rel_s + mask_s
        e, l = _softmax_rows(s, jnp.exp2)
        outs[r] = outs[r] + gates[:, GROUP + r:GROUP + r + 1] * (_bdot(e, vsel) / l)

    nw = WINDOW + tq
    start = pl.multiple_of(jnp.maximum(t0 - WINDOW, 0), LANES)
    kwin = kwin_ref[pl.ds(start, nw), :]
    vwin = vwin_ref[pl.ds(start, nw), :]
    kpos_w = start + _iota2((1, nw), 1)
    dist = qpos - kpos_w
    mask_w = jnp.where((dist >= 0) & (dist < WINDOW), 0.0, NEG)
    rel_w = (kpos_w - t0).astype(jnp.float32)
    for r in range(GROUP):
        s = _bdot(qs[r], kwin, _NT) + slopes[r] * rel_w + mask_w
        e, l = _softmax_rows(s, jnp.exp2)
        o = outs[r] + gates[:, 2 * GROUP + r:2 * GROUP + r + 1] * (_bdot(e, vwin) / l)
        o_ref[:, r * HEAD_DIM:(r + 1) * HEAD_DIM] = o.astype(o_ref.dtype)


def nsa_slopes():
    h = jnp.arange(1, NSA_HEADS + 1, dtype=jnp.float32)
    s = (2.0 ** (-8.0 * h / NSA_HEADS)).reshape(NSA_KV_HEADS, GROUP, 1)
    return jnp.broadcast_to(s, (NSA_KV_HEADS, GROUP, LANES))


def nsa_gate_cols(w_g):
    d = w_g.shape[0]
    w = w_g.reshape(d, 3, NSA_KV_HEADS, GROUP).transpose(0, 2, 1, 3).reshape(d, NSA_KV_HEADS, 3 * GROUP)
    return jnp.pad(w, ((0, 0), (0, 0), (0, LANES - 3 * GROUP))).reshape(d, NSA_KV_HEADS * LANES)


def nsa_prompt(q, gl, ck, cv, kv16, q_norm, *, batch, seq, tq=NSA_TQ):
    nt = seq // tq
    expand = (jnp.arange(NB_PAD)[:, None] == (jnp.arange(seq)[None, :] // BLOCK)).astype(jnp.bfloat16)
    kvw = 2 * NSA_KV_HEADS
    tiles_per_call = nt // NSA_PROMPT_CALLS

    def kvspec(col0):
        return pl.BlockSpec((seq, HEAD_DIM), lambda b, g, i: (b, col0 + g))

    out = None
    for part in range(NSA_PROMPT_CALLS):
        tile0 = part * tiles_per_call

        def rows(b, g, i, tile0=tile0):
            return (b * nt + tile0 + i, g)

        operands = [q, gl, ck, cv, kv16, kv16, kv16, kv16, q_norm.reshape(1, HEAD_DIM), nsa_slopes(), expand]
        in_specs = [pl.BlockSpec((tq, GROUP * HEAD_DIM), rows),
                    pl.BlockSpec((tq, LANES), rows),
                    pl.BlockSpec((1, NB_PAD, HEAD_DIM), lambda b, g, i: (b, 0, g)),
                    pl.BlockSpec((1, NB_PAD, HEAD_DIM), lambda b, g, i: (b, 0, g)),
                    kvspec(kvw), kvspec(kvw + NSA_KV_HEADS), kvspec(2 * kvw), kvspec(2 * kvw + NSA_KV_HEADS),
                    pl.BlockSpec((1, HEAD_DIM), lambda b, g, i: (0, 0)),
                    pl.BlockSpec((1, GROUP, LANES), lambda b, g, i: (g, 0, 0)),
                    pl.BlockSpec((NB_PAD, seq), lambda b, g, i: (0, 0))]
        aliases = {}
        if out is not None:
            aliases = {len(operands): 0}
            operands.append(out)
            in_specs.append(pl.BlockSpec(memory_space=pl.ANY))
        out = pl.pallas_call(
            functools.partial(_nsa_prompt_kernel, tq=tq, seq=seq, tile0=tile0, n_keys=(tile0 + tiles_per_call) * tq),
            grid=(batch, NSA_KV_HEADS, tiles_per_call),
            in_specs=in_specs,
            out_specs=pl.BlockSpec((tq, GROUP * HEAD_DIM), rows),
            out_shape=jax.ShapeDtypeStruct((batch * seq, NSA_HEADS * HEAD_DIM), jnp.bfloat16),
            input_output_aliases=aliases,
            compiler_params=_params("parallel", "parallel", "arbitrary"),
            name="nsa_prompt",
        )(*operands)
    return out


KV_SLOTS = 2 * NSA_KV_HEADS
N_PAGES = PAST_LEN // PAGE_SIZE
PAGES_PER_STEP = 8
PAGE_ROWS = PAGE_SIZE * KV_SLOTS
BLOCK_ROWS = BLOCK * KV_SLOTS
BLOCKS_PER_PAGE = PAGE_SIZE // BLOCK
N_PAST_BLOCKS = PAST_LEN // BLOCK
N_SAMPLE_BLOCKS = -(-(PAST_LEN + DEC_SEQ) // BLOCK)
SEL_LANES = -(-N_SAMPLE_BLOCKS // LANES) * LANES
QROWS = GROUP * DEC_SEQ


def _slot_rows(ref, lead, slot, n):
    return ref[lead, pl.ds(slot, n, stride=KV_SLOTS), :]


def _page_specs():
    def spec(j):
        return pl.BlockSpec((1, PAGE_ROWS, HEAD_DIM),
                            lambda b, i, pt: (pt[b * N_PAGES + PAGES_PER_STEP * i + j], 0, 0))
    return [spec(j) for j in range(PAGES_PER_STEP)]


def _paged(cache):
    return [cache.reshape(-1, PAGE_ROWS, HEAD_DIM)] * PAGES_PER_STEP


def _cmp_pool_kernel(pt_ref, *refs):
    pages, (w_ref, o_ref) = refs[:PAGES_PER_STEP], refs[PAGES_PER_STEP:]
    i = pl.program_id(1)
    tiles = []
    for p in pages:
        prod = p[0] * w_ref[...]
        for h in range(BLOCKS_PER_PAGE):
            blk = prod[h * BLOCK_ROWS:(h + 1) * BLOCK_ROWS].reshape(BLOCK, KV_SLOTS, HEAD_DIM)
            tiles.append(jnp.sum(blk, axis=0))
    n = PAGES_PER_STEP * BLOCKS_PER_PAGE * KV_SLOTS
    o_ref[0, pl.ds(pl.multiple_of(i * n, n), n), :] = jnp.concatenate(tiles, axis=0)


def cmp_pool_pages(cache, page_table, cmp_pos_w):
    w = jnp.repeat(jnp.tile(cmp_pos_w, (BLOCKS_PER_PAGE, 1)), NSA_KV_HEADS, axis=1)
    wt = jnp.broadcast_to(w.reshape(PAGE_ROWS, 1), (PAGE_ROWS, HEAD_DIM))
    return pl.pallas_call(
        _cmp_pool_kernel,
        grid_spec=pltpu.PrefetchScalarGridSpec(
            num_scalar_prefetch=1,
            grid=(DEC_BATCH, N_PAGES // PAGES_PER_STEP),
            in_specs=_page_specs() + [pl.BlockSpec((PAGE_ROWS, HEAD_DIM), lambda b, i, pt: (0, 0))],
            out_specs=pl.BlockSpec((1, N_PAST_BLOCKS * KV_SLOTS, HEAD_DIM), lambda b, i, pt: (b, 0, 0))),
        out_shape=jax.ShapeDtypeStruct((DEC_BATCH, N_PAST_BLOCKS * KV_SLOTS, HEAD_DIM), jnp.float32),
        compiler_params=_params("parallel", "arbitrary"),
        name="cmp_pool_pages",
    )(page_table.reshape(-1), *_paged(cache), wt)


def _sample_queries(q_ref, qn_ref, g):
    parts = []
    for r in range(GROUP):
        c0 = (g * GROUP + r) * HEAD_DIM
        x = q_ref[:, c0:c0 + HEAD_DIM]
        x = x * lax.rsqrt(jnp.mean(x * x, axis=-1, keepdims=True) + EPS) * qn_ref[...]
        parts.append(x * SCALE)
    return jnp.concatenate(parts, axis=0).astype(jnp.bfloat16)


def _head_major_col(x, lane0):
    return jnp.concatenate([x[:, lane0 + r:lane0 + r + 1] for r in range(GROUP)], axis=0)


def _slope_col(sl_ref, g):
    return jnp.concatenate([jnp.broadcast_to(sl_ref[g, r:r + 1, 0:1], (DEC_SEQ, 1)) for r in range(GROUP)], axis=0)


def _sample_select_kernel(pooled_ref, wcmp_ref, kn_ref, q_ref, gl_ref, qn_ref, sl_ref, wcache_ref, new_ref, ex_ref,
                          ocw_ref, mask_ref, newmask_ref):
    srow = _iota2((QROWS, 1), 0) % DEC_SEQ
    qpos = PAST_LEN + srow
    gl = gl_ref[...]
    nbp = N_PAST_BLOCKS
    lane_c = _iota2((1, nbp), 1)
    blk_end = (lane_c + 1) * BLOCK - 1
    rel_c = (blk_end - PAST_LEN).astype(jnp.float32)
    nw = wcache_ref.shape[1] // KV_SLOTS
    nwk = nw + LANES
    jw = _iota2((1, nwk), 1)
    kpos_w = jnp.where(jw < nw, PAST_LEN - nw + jw, PAST_LEN + jw - nw)
    dist_w = qpos - kpos_w
    ok_w = (dist_w >= 0) & (dist_w < WINDOW) & (jw < nw + DEC_SEQ)
    mask_w = jnp.where(ok_w, 0.0, NEG)
    rel_w = (kpos_w - PAST_LEN).astype(jnp.float32)
    pad_rows = jnp.zeros((LANES - DEC_SEQ, HEAD_DIM), jnp.float32)
    for g in range(NSA_KV_HEADS):
        ck = _hdot(_slot_rows(pooled_ref, 0, g, nbp), wcmp_ref[0, g])
        ck = ck * lax.rsqrt(jnp.mean(ck * ck, axis=-1, keepdims=True) + EPS) * kn_ref[...]
        cv = _hdot(_slot_rows(pooled_ref, 0, NSA_KV_HEADS + g, nbp), wcmp_ref[1, g])
        qg = _sample_queries(q_ref, qn_ref, g)
        slope = _slope_col(sl_ref, g)
        gates = _sigmoid(gl[:, g * LANES:(g + 1) * LANES])
        s = _bdot(qg, ck, _NT) + slope * rel_c
        e, l = _softmax_rows(jnp.where(qpos >= blk_end, s, NEG))
        p = e / l
        o = _head_major_col(gates, 0) * _bdot(p, cv)
        imp = p[0:DEC_SEQ]
        for r in range(1, GROUP):
            imp = imp + p[r * DEC_SEQ:(r + 1) * DEC_SEQ]
        imp = jnp.concatenate([imp, jnp.zeros((DEC_SEQ, SEL_LANES - nbp), jnp.float32)], axis=1)
        sel = _select_mask(imp, PAST_LEN + _iota2((DEC_SEQ, 1), 0), N_SAMPLE_BLOCKS).astype(jnp.bfloat16)
        for half in range(nbp // LANES):
            keys = _bdot(sel[:, half * LANES:(half + 1) * LANES], ex_ref[...])
            mask_ref[0, g * DEC_SEQ:(g + 1) * DEC_SEQ, half * LANES * BLOCK:(half + 1) * LANES * BLOCK] = keys
        newmask_ref[0, g * DEC_SEQ:(g + 1) * DEC_SEQ, :] = jnp.broadcast_to(
            sel[:, nbp:nbp + 1].astype(jnp.float32), (DEC_SEQ, LANES))
        c0 = 4 * KV_HALF + g * HEAD_DIM
        kw = jnp.concatenate([_slot_rows(wcache_ref, 0, g, nw), new_ref[:, c0:c0 + HEAD_DIM], pad_rows], axis=0)
        vw = jnp.concatenate([_slot_rows(wcache_ref, 0, NSA_KV_HEADS + g, nw),
                              new_ref[:, c0 + KV_HALF:c0 + KV_HALF + HEAD_DIM], pad_rows], axis=0)
        s = _bdot(qg, kw, _NT) + slope * rel_w + mask_w
        e, l = _softmax_rows(s)
        o = o + _head_major_col(gates, 2 * GROUP) * (_bdot(e, vw) / l)
        for r in range(GROUP):
            c0 = (g * GROUP + r) * HEAD_DIM
            ocw_ref[:, c0:c0 + HEAD_DIM] = o[r * DEC_SEQ:(r + 1) * DEC_SEQ]


def nsa_sample_select(pooled, w_cmp, kn_cmp, q, gl, q_norm, cache_win, kv32):
    expand = (jnp.arange(LANES)[:, None] == (jnp.arange(LANES * BLOCK)[None, :] // BLOCK)).astype(jnp.bfloat16)
    row0 = N_PROMPT // DEC_SEQ
    nw = cache_win.shape[1]
    return pl.pallas_call(
        _sample_select_kernel,
        grid=(DEC_BATCH,),
        in_specs=[pl.BlockSpec((1, N_PAST_BLOCKS * KV_SLOTS, HEAD_DIM), lambda b: (b, 0, 0)),
                  pl.BlockSpec((2, NSA_KV_HEADS, HEAD_DIM, HEAD_DIM), lambda b: (0, 0, 0, 0)),
                  pl.BlockSpec((1, HEAD_DIM), lambda b: (0, 0)),
                  pl.BlockSpec((DEC_SEQ, NSA_HEADS * HEAD_DIM), lambda b: (row0 + b, 0)),
                  pl.BlockSpec((DEC_SEQ, NSA_KV_HEADS * LANES), lambda b: (row0 + b, 0)),
                  pl.BlockSpec((1, HEAD_DIM), lambda b: (0, 0)),
                  pl.BlockSpec((NSA_KV_HEADS, GROUP, LANES), lambda b: (0, 0, 0)),
                  pl.BlockSpec((1, nw * KV_SLOTS, HEAD_DIM), lambda b: (b, 0, 0)),
                  pl.BlockSpec((DEC_SEQ, KV_W), lambda b: (row0 + b, 0)),
                  pl.BlockSpec((LANES, LANES * BLOCK), lambda b: (0, 0))],
        out_specs=[pl.BlockSpec((DEC_SEQ, NSA_HEADS * HEAD_DIM), lambda b: (b, 0)),
                   pl.BlockSpec((1, NSA_KV_HEADS * DEC_SEQ, PAST_LEN), lambda b: (b, 0, 0)),
                   pl.BlockSpec((1, NSA_KV_HEADS * DEC_SEQ, LANES), lambda b: (b, 0, 0))],
        out_shape=[jax.ShapeDtypeStruct((N_SAMPLE, NSA_HEADS * HEAD_DIM), jnp.float32),
                   jax.ShapeDtypeStruct((DEC_BATCH, NSA_KV_HEADS * DEC_SEQ, PAST_LEN), jnp.float32),
                   jax.ShapeDtypeStruct((DEC_BATCH, NSA_KV_HEADS * DEC_SEQ, LANES), jnp.float32)],
        compiler_params=_params("parallel"),
        name="nsa_sample_select",
    )(pooled, w_cmp, kn_cmp.reshape(1, HEAD_DIM), q, gl, q_norm.reshape(1, HEAD_DIM), nsa_slopes(),
      cache_win.reshape(DEC_BATCH, nw * KV_SLOTS, HEAD_DIM), kv32, expand)


def _sample_sel_kernel(pt_ref, *refs):
    pages = refs[:PAGES_PER_STEP]
    (mask_ref, newmask_ref, q_ref, gl_ref, qn_ref, sl_ref, new_ref, ocw_ref,
     o_ref, qs, m_sc, l_sc, acc_sc) = refs[PAGES_PER_STEP:]
    i = pl.program_id(1)
    groups = range(NSA_KV_HEADS)

    @pl.when(i == 0)
    def _():
        for g in groups:
            qs[g] = _sample_queries(q_ref, qn_ref, g)
        m_sc[...] = jnp.full(m_sc.shape, NEG, jnp.float32)
        l_sc[...] = jnp.zeros(l_sc.shape, jnp.float32)
        acc_sc[...] = jnp.zeros(acc_sc.shape, jnp.float32)

    def accumulate(keys, vals, rel, keep8):
        keep = [jnp.concatenate([keep8[g]] * GROUP, axis=0) for g in groups]
        s = [_bdot(qs[g], keys[g], _NT) + _slope_col(sl_ref, g) * rel + jnp.where(keep[g] > 0.5, 0.0, NEG)
             for g in groups]
        m_old = [m_sc[g] for g in groups]
        m_new = [jnp.maximum(m_old[g], jnp.max(s[g], axis=-1, keepdims=True)) for g in groups]
        p = [jnp.exp(s[g] - m_new[g]) * keep[g] for g in groups]
        pv = [_bdot(p[g], vals[g]) for g in groups]
        for g in groups:
            alpha = jnp.exp(m_old[g] - m_new[g])
            l_sc[g] = alpha * l_sc[g] + jnp.sum(p[g], axis=-1, keepdims=True)
            acc_sc[g] = alpha * acc_sc[g] + pv[g]
            m_sc[g] = m_new[g]

    lane = _iota2((1, PAGE_SIZE), 1)
    for j, page in enumerate(pages):
        keep_all = mask_ref[0, :, j * PAGE_SIZE:(j + 1) * PAGE_SIZE]

        @pl.when(jnp.max(keep_all) > 0.5)
        def _(j=j, page=page, keep_all=keep_all):
            rel = ((i * PAGES_PER_STEP + j) * PAGE_SIZE - PAST_LEN + lane).astype(jnp.float32)
            accumulate([_slot_rows(page, 0, g, PAGE_SIZE) for g in groups],
                       [_slot_rows(page, 0, NSA_KV_HEADS + g, PAGE_SIZE) for g in groups],
                       rel, [keep_all[g * DEC_SEQ:(g + 1) * DEC_SEQ] for g in groups])

    @pl.when(i == pl.num_programs(1) - 1)
    def _():
        gl = gl_ref[...]
        pad_rows = jnp.zeros((PAGE_SIZE - DEC_SEQ, HEAD_DIM), jnp.float32)
        causal = ((lane <= _iota2((DEC_SEQ, 1), 0)) & (lane < DEC_SEQ)).astype(jnp.float32)
        c0 = 2 * KV_HALF
        accumulate([jnp.concatenate([new_ref[:, c0 + g * HEAD_DIM:c0 + (g + 1) * HEAD_DIM], pad_rows], axis=0)
                    for g in groups],
                   [jnp.concatenate([new_ref[:, c0 + KV_HALF + g * HEAD_DIM:c0 + KV_HALF + (g + 1) * HEAD_DIM], pad_rows],
                                    axis=0) for g in groups],
                   lane.astype(jnp.float32),
                   [newmask_ref[0, g * DEC_SEQ:(g + 1) * DEC_SEQ, :] * causal for g in groups])
        for g in groups:
            gates = _sigmoid(gl[:, g * LANES:(g + 1) * LANES])
            o = _head_major_col(gates, GROUP) * (acc_sc[g] / l_sc[g])
            for r in range(GROUP):
                c0 = (g * GROUP + r) * HEAD_DIM
                o_ref[:, c0:c0 + HEAD_DIM] = ocw_ref[:, c0:c0 + HEAD_DIM] + o[r * DEC_SEQ:(r + 1) * DEC_SEQ]


def nsa_sample_sel(cache, page_table, mask, newmask, q, gl, q_norm, kv32, ocw):
    row0 = N_PROMPT // DEC_SEQ
    nrow = NSA_KV_HEADS * DEC_SEQ
    step_keys = PAGES_PER_STEP * PAGE_SIZE
    return pl.pallas_call(
        _sample_sel_kernel,
        grid_spec=pltpu.PrefetchScalarGridSpec(
            num_scalar_prefetch=1,
            grid=(DEC_BATCH, N_PAGES // PAGES_PER_STEP),
            in_specs=_page_specs() + [
                pl.BlockSpec((1, nrow, step_keys), lambda b, i, pt: (b, 0, i)),
                pl.BlockSpec((1, nrow, LANES), lambda b, i, pt: (b, 0, 0)),
                pl.BlockSpec((DEC_SEQ, NSA_HEADS * HEAD_DIM), lambda b, i, pt: (row0 + b, 0)),
                pl.BlockSpec((DEC_SEQ, NSA_KV_HEADS * LANES), lambda b, i, pt: (row0 + b, 0)),
                pl.BlockSpec((1, HEAD_DIM), lambda b, i, pt: (0, 0)),
                pl.BlockSpec((NSA_KV_HEADS, GROUP, LANES), lambda b, i, pt: (0, 0, 0)),
                pl.BlockSpec((DEC_SEQ, KV_W), lambda b, i, pt: (row0 + b, 0)),
                pl.BlockSpec((DEC_SEQ, NSA_HEADS * HEAD_DIM), lambda b, i, pt: (b, 0))],
            out_specs=pl.BlockSpec((DEC_SEQ, NSA_HEADS * HEAD_DIM), lambda b, i, pt: (b, 0)),
            scratch_shapes=[pltpu.VMEM((NSA_KV_HEADS, QROWS, HEAD_DIM), jnp.bfloat16),
                            pltpu.VMEM((NSA_KV_HEADS, QROWS, 1), jnp.float32),
                            pltpu.VMEM((NSA_KV_HEADS, QROWS, 1), jnp.float32),
                            pltpu.VMEM((NSA_KV_HEADS, QROWS, HEAD_DIM), jnp.float32)]),
        out_shape=jax.ShapeDtypeStruct((N_SAMPLE, NSA_HEADS * HEAD_DIM), jnp.float32),
        compiler_params=_params("parallel", "arbitrary"),
        name="nsa_sample_sel",
    )(page_table.reshape(-1), *_paged(cache), mask, newmask,
      q, gl, q_norm.reshape(1, HEAD_DIM), nsa_slopes(), kv32, ocw)


def nsa_sample(q, gl, kv32, cache_cmp_kv, cache_sel_kv, cache_win_kv, page_table, cmp_pos_w, w_cmp, kn_cmp, q_norm):
    pooled = cmp_pool_pages(cache_cmp_kv, page_table, cmp_pos_w)
    ocw, mask, newmask = nsa_sample_select(pooled, w_cmp, kn_cmp, q, gl, q_norm, cache_win_kv, kv32)
    return nsa_sample_sel(cache_sel_kv, page_table, mask, newmask, q, gl, q_norm, kv32, ocw)


def _split_rows(x):
    return (x[:N_PROMPT].reshape((BATCH, SEQ) + x.shape[1:]),
            x[N_PROMPT:].reshape((DEC_BATCH, DEC_SEQ) + x.shape[1:]))


def kernel(x_prompt, x_sample, state_conv, state_delta, cache_cmp_kv, cache_sel_kv, cache_win_kv, page_table,
           ffn_norm, ffn_w_gate, ffn_w_up, ffn_w_down, mix_norm,
           gdn_w_in, gdn_conv_w, gdn_a_log, gdn_dt_bias, gdn_o_norm, gdn_w_out,
           kv_norm, w_kv, cmp_pos_w, w_cmp, k_norm, nsa_w_q, nsa_q_norm, nsa_w_o):
    bf = jnp.bfloat16
    n_main = GDN_CONV_CH + GDN_V
    n_q = NSA_HEADS * HEAD_DIM
    hist = CONV_W - 1
    wd16 = ffn_w_down.astype(bf)

    def ffn_half(h, hg16, ssq, layer, i, next_gains):
        act = swiglu_up(hg16, ssq, ffn_w_gate, ffn_w_up, (layer, i))
        return matmul_residual(act, wd16, (layer, i), h, 0.5, next_gains, tn=256, tm=DOWN_ROWS)

    h, hg16, ssq = rows_prep(x_prompt.reshape(N_PROMPT, D_MODEL), x_sample.reshape(N_SAMPLE, D_MODEL), ffn_norm[0, 0])
    h, (hg16,), ssq = ffn_half(h, hg16, ssq, 0, 0, [mix_norm[0]])
    proj = matmul_normed(hg16, ssq, gdn_w_in, (0,), n_cols=n_main, tn=512)
    ab = matmul_normed(hg16, ssq, gdn_group_cols(gdn_w_in[0, :, n_main:]), tn=LANES)
    conv_w8 = jnp.pad(gdn_conv_w[0], ((0, CONV_PAD - CONV_W), (0, 0)))
    hp = gdn_head_params(gdn_a_log[0], gdn_dt_bias[0])

    def pad_sample(x):
        x = x[N_PROMPT:].reshape(DEC_BATCH, DEC_SEQ, x.shape[1])
        return jnp.pad(x, ((0, 0), (0, GDN_CHUNK - DEC_SEQ), (0, 0))).reshape(DEC_BATCH * GDN_CHUNK, x.shape[2])

    o_p, delta_p = gdn_mixer(proj, ab, jnp.zeros((BATCH, CONV_PAD, GDN_CONV_CH), jnp.float32),
                             jnp.zeros((BATCH, GDN_HEADS, GDN_DK, GDN_DV), jnp.float32),
                             conv_w8, hp, gdn_o_norm[0], batch=BATCH, n_chunks=SEQ // GDN_CHUNK)
    o_s, delta_s = gdn_mixer(pad_sample(proj), pad_sample(ab),
                             jnp.pad(state_conv[0], ((0, 0), (CONV_PAD - hist, 0), (0, 0))), state_delta[0],
                             conv_w8, hp, gdn_o_norm[0], batch=DEC_BATCH, n_chunks=1, valid_rows=DEC_SEQ)
    o_s = o_s.reshape(DEC_BATCH, GDN_CHUNK, GDN_V)[:, :DEC_SEQ].reshape(N_SAMPLE, GDN_V)
    conv_p = jnp.stack([lax.slice(proj, ((b + 1) * SEQ - hist, 0), ((b + 1) * SEQ, GDN_CONV_CH)) for b in range(BATCH)])
    u_s = lax.slice(proj, (N_PROMPT, 0), (N_ROWS, GDN_CONV_CH)).reshape(DEC_BATCH, DEC_SEQ, GDN_CONV_CH)
    conv_s = jnp.concatenate([state_conv[0], u_s], axis=1)[:, DEC_SEQ:]
    h, (hg16,), ssq = matmul_residual(jnp.concatenate([o_p, o_s], axis=0), gdn_w_out, (0,), h, 1.0, [ffn_norm[0, 1]],
                                      tn=256, tm=MM_ROWS)
    h, (hg16_kv, hg16), ssq = ffn_half(h, hg16, ssq, 0, 1, [kv_norm, ffn_norm[1, 0]])

    kv32, kv16 = kv_finish(matmul_normed(hg16_kv, ssq, w_kv, tn=512), k_norm)
    kv5 = kv32.reshape(N_ROWS, 3, 2, NSA_KV_HEADS, HEAD_DIM)
    cmp_p, cmp_s = _split_rows(kv5[:, 0])
    sel_p, sel_s = _split_rows(kv5[:, 1])
    win_rows_p, win_rows_s = _split_rows(kv5[:, 2])

    h, (hg16,), ssq = ffn_half(h, hg16, ssq, 1, 0, [mix_norm[1]])
    q = matmul_normed(hg16, ssq, nsa_w_q, (0,), n_cols=n_q, tn=512)
    gl = matmul_normed(hg16, ssq, nsa_gate_cols(nsa_w_q[0, :, n_q:]), tn=LANES)
    ck, cv = compress_prompt(kv32, cmp_pos_w, w_cmp, k_norm[0], batch=BATCH, seq=SEQ)
    o_p = nsa_prompt(q, gl, ck, cv, kv16, nsa_q_norm[0], batch=BATCH, seq=SEQ)
    o_s = nsa_sample(q, gl, kv32, cache_cmp_kv, cache_sel_kv, cache_win_kv, page_table, cmp_pos_w, w_cmp, k_norm[0],
                     nsa_q_norm[0])
    h, (hg16,), ssq = matmul_residual(jnp.concatenate([o_p, o_s.astype(bf)], axis=0), nsa_w_o, (0,), h, 1.0,
                                      [ffn_norm[1, 1]], tn=256, tm=MM_ROWS)
    h, _, _ = ffn_half(h, hg16, ssq, 1, 1, [])

    y_p, y_s = _split_rows(h)
    win_p = win_rows_p[:, -min(WINDOW, SEQ):]
    win_s = jnp.concatenate([cache_win_kv, win_rows_s], axis=1)[:, DEC_SEQ:]
    return (y_p, y_s, conv_p[None], conv_s[None], delta_p[None], delta_s[None],
            cmp_p, cmp_s, sel_p, sel_s, win_p, win_s)
```

```python
import functools
import math

import jax
import jax.numpy as jnp
from jax import lax
from jax.experimental import pallas as pl
from jax.experimental.pallas import tpu as pltpu

D_MODEL = 4096
BATCH = 4
SEQ = 2048
DEPTH = 2
DEC_BATCH = 8
DEC_SEQ = 8
PAST_LEN = 16384
PAGE_SIZE = 128
D_FF = 11008
EPS = 1e-6
GDN_HEADS = 16
GDN_DK = 128
GDN_DV = 256
CONV_W = 4
GDN_CHUNK = 64
GDN_QK = GDN_HEADS * GDN_DK
GDN_V = GDN_HEADS * GDN_DV
GDN_CONV_CH = 2 * GDN_QK + GDN_V
NSA_HEADS = 32
NSA_KV_HEADS = 4
HEAD_DIM = 128
GROUP = NSA_HEADS // NSA_KV_HEADS
BLOCK = 64
N_SELECT = 16
N_LOCAL = 2
WINDOW = 512
SCALE = HEAD_DIM ** -0.5
NEG = -1e30
LOG2E = 1.0 / math.log(2.0)
FORCE = 1e4

N_PROMPT = BATCH * SEQ
N_SAMPLE = DEC_BATCH * DEC_SEQ
N_ROWS = N_PROMPT + N_SAMPLE

VMEM_LIMIT_BYTES = 56 * 1024 * 1024
LANES = 128
SUBLANES = 8

NORM_ROWS = 192
MM_ROWS = 1376
DOWN_ROWS = 688


def _params(*sem):
    return pltpu.CompilerParams(dimension_semantics=sem, vmem_limit_bytes=VMEM_LIMIT_BYTES)


PREP_ROWS = 64


def _lane_fold(x):
    acc = x[:, 0:LANES]
    for c in range(1, x.shape[1] // LANES):
        acc = acc + x[:, c * LANES:(c + 1) * LANES]
    return acc


def _rows_prep_kernel(xp_ref, xs_ref, g_ref, h_ref, h16_ref, ssq_ref):
    n_prompt_tiles = N_PROMPT // PREP_ROWS

    def emit(x):
        h_ref[...] = x
        h16_ref[...] = (x * g_ref[...]).astype(h16_ref.dtype)
        ssq_ref[...] = _lane_fold(x * x)

    @pl.when(pl.program_id(0) < n_prompt_tiles)
    def _():
        emit(xp_ref[...])

    @pl.when(pl.program_id(0) >= n_prompt_tiles)
    def _():
        emit(xs_ref[...])


def rows_prep(x_prompt, x_sample, gain):
    d = x_prompt.shape[1]
    n_p = N_PROMPT // PREP_ROWS
    n = N_ROWS // PREP_ROWS
    return pl.pallas_call(
        _rows_prep_kernel,
        grid=(n,),
        in_specs=[pl.BlockSpec((PREP_ROWS, d), lambda i: (jnp.minimum(i, n_p - 1), 0)),
                  pl.BlockSpec((PREP_ROWS, d), lambda i: (jnp.maximum(i - n_p, 0), 0)),
                  pl.BlockSpec((1, d), lambda i: (0, 0))],
        out_specs=[pl.BlockSpec((PREP_ROWS, d), lambda i: (i, 0)),
                   pl.BlockSpec((PREP_ROWS, d), lambda i: (i, 0)),
                   pl.BlockSpec((PREP_ROWS, LANES), lambda i: (i, 0))],
        out_shape=[jax.ShapeDtypeStruct((N_ROWS, d), jnp.float32),
                   jax.ShapeDtypeStruct((N_ROWS, d), jnp.bfloat16),
                   jax.ShapeDtypeStruct((N_ROWS, LANES), jnp.float32)],
        compiler_params=_params("parallel"),
        name="rows_prep",
    )(x_prompt, x_sample, gain.reshape(1, d))


def _row_scale(ssq_ref, k):
    return lax.rsqrt(jnp.sum(ssq_ref[...], axis=-1, keepdims=True) / k + EPS)


def _weight_spec(w, lead, tn):
    k = w.shape[-2]
    return pl.BlockSpec((None,) * len(lead) + (k, tn), lambda i, j: tuple(lead) + (0, j))


def _normed_specs(tm, k):
    return [pl.BlockSpec((tm, k), lambda i, j: (i, 0)),
            pl.BlockSpec((tm, LANES), lambda i, j: (i, 0))]


def _mm_kernel(x_ref, ssq_ref, w_ref, o_ref):
    acc = jnp.dot(x_ref[...], w_ref[...].astype(jnp.bfloat16), preferred_element_type=jnp.float32)
    o_ref[...] = (_row_scale(ssq_ref, x_ref.shape[1]) * acc).astype(o_ref.dtype)


def matmul_normed(xg16, ssq, w, lead=(), *, tn, n_cols=None, tm=MM_ROWS):
    m, k = xg16.shape
    n = n_cols or w.shape[-1]
    return pl.pallas_call(
        _mm_kernel,
        grid=(m // tm, n // tn),
        in_specs=_normed_specs(tm, k) + [_weight_spec(w, lead, tn)],
        out_specs=pl.BlockSpec((tm, tn), lambda i, j: (i, j)),
        out_shape=jax.ShapeDtypeStruct((m, n), jnp.float32),
        compiler_params=_params("parallel", "arbitrary"),
        name="matmul_normed",
    )(xg16, ssq, w)


def _mm_res_kernel(x_ref, w_ref, r_ref, *refs, scale, n_gains):
    gain_refs, o_ref, o16_refs = refs[:n_gains], refs[n_gains], refs[n_gains + 1:2 * n_gains + 1]
    w = w_ref[...].astype(jnp.bfloat16)
    h = r_ref[...] + scale * jnp.dot(x_ref[...], w, preferred_element_type=jnp.float32)
    o_ref[...] = h
    for g_ref, o16_ref in zip(gain_refs, o16_refs):
        o16_ref[...] = (h * g_ref[...]).astype(o16_ref.dtype)
    if n_gains:
        ssq_ref = refs[2 * n_gains + 1]
        part = _lane_fold(h * h)

        @pl.when(pl.program_id(1) == 0)
        def _():
            ssq_ref[...] = part

        @pl.when(pl.program_id(1) > 0)
        def _():
            ssq_ref[...] += part


def matmul_residual(x, w, lead, res, scale, gains, *, tn, tm):
    m, k = x.shape
    n = w.shape[-1]
    ng = len(gains)
    tile = pl.BlockSpec((tm, tn), lambda i, j: (i, j))
    out = pl.pallas_call(
        functools.partial(_mm_res_kernel, scale=scale, n_gains=ng),
        grid=(m // tm, n // tn),
        in_specs=[pl.BlockSpec((tm, k), lambda i, j: (i, 0)), _weight_spec(w, lead, tn), tile]
                 + [pl.BlockSpec((1, tn), lambda i, j: (0, j))] * ng,
        out_specs=[tile] * (1 + ng) + [pl.BlockSpec((tm, LANES), lambda i, j: (i, 0))] * min(ng, 1),
        out_shape=[jax.ShapeDtypeStruct((m, n), jnp.float32)] + [jax.ShapeDtypeStruct((m, n), jnp.bfloat16)] * ng
                  + [jax.ShapeDtypeStruct((m, LANES), jnp.float32)] * min(ng, 1),
        compiler_params=_params("parallel", "arbitrary"),
        name="matmul_residual",
    )(x, w, res, *[g.reshape(1, n) for g in gains])
    return out[0], list(out[1:1 + ng]), (out[1 + ng] if ng else None)


WD_SLAB_PASSES = 2


def _swiglu_up_kernel(x_ref, ssq_ref, wg_ref, wu_ref, wd_ref, o_ref, wd16_ref):
    x = x_ref[...]
    r = _row_scale(ssq_ref, x.shape[1])
    g = r * jnp.dot(x, wg_ref[...].astype(jnp.bfloat16), preferred_element_type=jnp.float32)
    u = r * jnp.dot(x, wu_ref[...].astype(jnp.bfloat16), preferred_element_type=jnp.float32)
    o_ref[...] = (g * jax.nn.sigmoid(g) * u).astype(o_ref.dtype)
    wd16_ref[...] = wd_ref[...].astype(wd16_ref.dtype)


def swiglu_up(xg16, ssq, wg, wu, wd, lead, tn=256, tm=MM_ROWS):
    m, k = xg16.shape
    n = wg.shape[-1]
    nj = n // tn
    n_slabs = WD_SLAB_PASSES * nj
    slab = wd.shape[-2] // n_slabs
    d_out = wd.shape[-1]

    def slab_index(i, j):
        return jnp.minimum(i * nj + j, n_slabs - 1)

    return pl.pallas_call(
        _swiglu_up_kernel,
        grid=(m // tm, nj),
        in_specs=_normed_specs(tm, k) + [
            _weight_spec(wg, lead, tn), _weight_spec(wu, lead, tn),
            pl.BlockSpec((None,) * len(lead) + (slab, d_out), lambda i, j: tuple(lead) + (slab_index(i, j), 0))],
        out_specs=[pl.BlockSpec((tm, tn), lambda i, j: (i, j)),
                   pl.BlockSpec((slab, d_out), lambda i, j: (slab_index(i, j), 0))],
        out_shape=[jax.ShapeDtypeStruct((m, n), jnp.bfloat16),
                   jax.ShapeDtypeStruct((wd.shape[-2], d_out), jnp.bfloat16)],
        compiler_params=_params("parallel", "arbitrary"),
        name="swiglu_up",
    )(xg16, ssq, wg, wu, wd)


_HI = lax.Precision.HIGHEST
_NT = (((1,), (1,)), ((), ()))
_TN = (((0,), (0,)), ((), ()))


def _bdot(a, b, dims=None):
    a = a.astype(jnp.bfloat16)
    b = b.astype(jnp.bfloat16)
    if dims is None:
        return jnp.dot(a, b, preferred_element_type=jnp.float32)
    return lax.dot_general(a, b, dims, preferred_element_type=jnp.float32)


def _hdot(a, b, dims=None):
    if dims is None:
        return jnp.dot(a, b, precision=_HI, preferred_element_type=jnp.float32)
    return lax.dot_general(a, b, dims, precision=_HI, preferred_element_type=jnp.float32)


def _iota2(shape, axis):
    return lax.broadcasted_iota(jnp.int32, shape, axis)


def _sigmoid(x):
    return 1.0 / (1.0 + jnp.exp(-x))


def _softmax_rows(s, exp=jnp.exp):
    m = jnp.max(s, axis=-1, keepdims=True)
    e = exp(s - m)
    return e, jnp.sum(e, axis=-1, keepdims=True)


GDN_HB = 16
CONV_PAD = SUBLANES


def _gdn_kernel(uq_ref, uk_ref, uv_ref, gate_ref, ab_ref, cq_ref, ck_ref, cv_ref, s0_ref,
                wq_ref, wk_ref, wv_ref, hp_ref, onorm_ref,
                o_ref, s_ref, extq, extk, extv, *, hb, chunk, valid_rows):
    C = chunk
    c = pl.program_id(2)

    @pl.when(c == 0)
    def _():
        s_ref[...] = s0_ref[...]
        extq[0:CONV_PAD, :] = cq_ref[0]
        extk[0:CONV_PAD, :] = ck_ref[0]
        extv[0:CONV_PAD, :] = cv_ref[0]

    def conv(ext, u_ref, w_ref):
        ext[CONV_PAD:CONV_PAD + C, :] = u_ref[...]
        base = CONV_PAD - (CONV_W - 1)
        acc = ext[base:base + C, :] * w_ref[0:1, :]
        for i in range(1, CONV_W):
            acc = acc + ext[base + i:base + i + C, :] * w_ref[i:i + 1, :]
        ext[0:CONV_PAD, :] = ext[C:C + CONV_PAD, :]
        return acc * _sigmoid(acc)

    qc = conv(extq, uq_ref, wq_ref)
    kc = conv(extk, uk_ref, wk_ref)
    vc = conv(extv, uv_ref, wv_ref)

    row = _iota2((C, C), 0)
    col = _iota2((C, C), 1)
    tri_incl = row >= col
    tri_strict = row > col
    row_ok = None
    if valid_rows < C:
        row_ok = _iota2((C, 1), 0) < valid_rows

    ab = ab_ref[...]
    x = ab + hp_ref[0, 1:2, :]
    softplus = jnp.maximum(x, 0.0) + jnp.log(1.0 + jnp.exp(-jnp.abs(x)))
    gmat = -jnp.exp(hp_ref[0, 0:1, :]) * softplus
    if row_ok is not None:
        gmat = jnp.where(row_ok, gmat, 0.0)
    beta = _sigmoid(ab)
    gam = _hdot(tri_incl.astype(jnp.float32), gmat)
    eye_l = (_iota2((LANES, LANES), 0) == _iota2((LANES, LANES), 1)).astype(jnp.float32)
    gam_t = _hdot(eye_l, gam, _NT)

    heads = range(hb)
    gc = [gam[:, h:h + 1] for h in heads]
    bc = [beta[:, hb + h:hb + h + 1] for h in heads]
    glast = [gam[C - 1:C, h:h + 1] for h in heads]
    q, k, v, decay = [], [], [], []
    for h in heads:
        qh = qc[:, h * GDN_DK:(h + 1) * GDN_DK]
        kh = kc[:, h * GDN_DK:(h + 1) * GDN_DK]
        vh = vc[:, h * GDN_DV:(h + 1) * GDN_DV]
        qh = qh * lax.rsqrt(jnp.sum(qh * qh, axis=-1, keepdims=True) + EPS) * GDN_DK ** -0.5
        kh = kh * lax.rsqrt(jnp.sum(kh * kh, axis=-1, keepdims=True) + EPS)
        if row_ok is not None:
            qh = jnp.where(row_ok, qh, 0.0)
            kh = jnp.where(row_ok, kh, 0.0)
            vh = jnp.where(row_ok, vh, 0.0)
        q.append(qh)
        k.append(kh)
        v.append(vh)
        decay.append(jnp.exp(jnp.where(tri_incl, gc[h] - gam_t[h:h + 1, :], -jnp.inf)))
    kk = [_bdot(k[h], k[h], _NT) for h in heads]
    qk = [_bdot(q[h], k[h], _NT) for h in heads]
    s_old = [s_ref[0, h] for h in heads]
    kq_s = [_bdot(jnp.concatenate([k[h], q[h]], axis=0), s_old[h]) for h in heads]
    pw = [jnp.where(tri_strict, decay[h] * kk[h], 0.0) * bc[h] for h in heads]
    nil = [-pw[h] for h in heads]
    for _ in range(int(math.log2(C)) - 1):
        pw = [_bdot(pw[h], pw[h]) for h in heads]
        nil = [nil[h] + pw[h] + _bdot(nil[h], pw[h]) for h in heads]
    gt = [jnp.exp(gc[h]) for h in heads]
    rhs = [bc[h] * (v[h] - gt[h] * kq_s[h][:C]) for h in heads]
    u = [rhs[h] + _bdot(nil[h], rhs[h]) for h in heads]
    o = [gt[h] * kq_s[h][C:] + _bdot(decay[h] * qk[h], u[h]) for h in heads]
    for h in heads:
        s_ref[0, h] = jnp.exp(glast[h]) * s_old[h] + _bdot(k[h] * jnp.exp(glast[h] - gc[h]), u[h], _TN)
    for h in heads:
        on = o[h] * lax.rsqrt(jnp.mean(o[h] * o[h], axis=-1, keepdims=True) + EPS) * onorm_ref[...]
        gate = gate_ref[:, h * GDN_DV:(h + 1) * GDN_DV]
        o_ref[:, h * GDN_DV:(h + 1) * GDN_DV] = (on * gate * _sigmoid(gate)).astype(o_ref.dtype)


def gdn_mixer(proj, ab, conv_init, s0, conv_w8, hp, o_norm, *, batch, n_chunks, valid_rows=GDN_CHUNK, hb=GDN_HB):
    C = GDN_CHUNK
    ng = GDN_HEADS // hb
    rows = batch * n_chunks * C
    qw, vw = hb * GDN_DK, hb * GDN_DV

    def rowblk(off):
        return lambda b, g, c: (b * n_chunks + c, off + g)

    def fixed3(off):
        return lambda b, g, c: (b, 0, off + g)

    def wblk(off):
        return lambda b, g, c: (0, off + g)

    in_specs = [
        pl.BlockSpec((C, qw), rowblk(0)),
        pl.BlockSpec((C, qw), rowblk(ng)),
        pl.BlockSpec((C, vw), rowblk(ng)),
        pl.BlockSpec((C, vw), rowblk(2 * ng)),
        pl.BlockSpec((C, LANES), rowblk(0)),
        pl.BlockSpec((1, CONV_PAD, qw), fixed3(0)),
        pl.BlockSpec((1, CONV_PAD, qw), fixed3(ng)),
        pl.BlockSpec((1, CONV_PAD, vw), fixed3(ng)),
        pl.BlockSpec((1, hb, GDN_DK, GDN_DV), lambda b, g, c: (b, g, 0, 0)),
        pl.BlockSpec((CONV_PAD, qw), wblk(0)),
        pl.BlockSpec((CONV_PAD, qw), wblk(ng)),
        pl.BlockSpec((CONV_PAD, vw), wblk(ng)),
        pl.BlockSpec((1, CONV_PAD, LANES), lambda b, g, c: (g, 0, 0)),
        pl.BlockSpec((1, GDN_DV), lambda b, g, c: (0, 0)),
    ]
    out_specs = [
        pl.BlockSpec((C, vw), rowblk(0)),
        pl.BlockSpec((1, hb, GDN_DK, GDN_DV), lambda b, g, c: (b, g, 0, 0)),
    ]
    return pl.pallas_call(
        functools.partial(_gdn_kernel, hb=hb, chunk=C, valid_rows=valid_rows),
        grid=(batch, ng, n_chunks),
        in_specs=in_specs,
        out_specs=out_specs,
        out_shape=[jax.ShapeDtypeStruct((rows, GDN_V), jnp.bfloat16),
                   jax.ShapeDtypeStruct((batch, GDN_HEADS, GDN_DK, GDN_DV), jnp.float32)],
        scratch_shapes=[pltpu.VMEM((CONV_PAD + C, qw), jnp.float32),
                        pltpu.VMEM((CONV_PAD + C, qw), jnp.float32),
                        pltpu.VMEM((CONV_PAD + C, vw), jnp.float32)],
        compiler_params=_params("parallel", "parallel", "arbitrary"),
        name="gdn_mixer",
    )(proj, proj, proj, proj, ab, conv_init, conv_init, conv_init, s0,
      conv_w8, conv_w8, conv_w8, hp, o_norm.reshape(1, GDN_DV))


def gdn_group_cols(w_ab, hb=GDN_HB):
    d = w_ab.shape[0]
    ng = GDN_HEADS // hb
    a = w_ab[:, :GDN_HEADS].reshape(d, ng, hb)
    b = w_ab[:, GDN_HEADS:].reshape(d, ng, hb)
    blk = jnp.concatenate([a, b, jnp.zeros((d, ng, LANES - 2 * hb), w_ab.dtype)], axis=-1)
    return blk.reshape(d, ng * LANES)


def gdn_head_params(a_log, dt_bias, hb=GDN_HB):
    ng = GDN_HEADS // hb
    rows = jnp.stack([a_log.reshape(ng, hb), dt_bias.reshape(ng, hb)], axis=1)
    return jnp.pad(rows, ((0, 0), (0, CONV_PAD - 2), (0, LANES - hb)))


KV_W = 3 * 2 * NSA_KV_HEADS * HEAD_DIM
KV_HALF = NSA_KV_HEADS * HEAD_DIM
N_KV_COLS = KV_W // HEAD_DIM
NB_PAD = LANES
_NORMED_KV_COLS = tuple(range(2 * NSA_KV_HEADS, 3 * NSA_KV_HEADS)) + tuple(range(4 * NSA_KV_HEADS, 5 * NSA_KV_HEADS))


def _kv_finish_kernel(kv_ref, gain_ref, o32_ref, o16_ref):
    for j in range(N_KV_COLS):
        sl = slice(j * HEAD_DIM, (j + 1) * HEAD_DIM)
        x = kv_ref[:, sl]
        if j in _NORMED_KV_COLS:
            x = x * lax.rsqrt(jnp.mean(x * x, axis=-1, keepdims=True) + EPS) * gain_ref[0:1, sl]
        o32_ref[:, sl] = x
        o16_ref[:, sl] = x.astype(o16_ref.dtype)


def kv_finish(kv, k_norm):
    n = kv.shape[0]
    ones = jnp.ones((KV_HALF,), jnp.float32)
    gain = jnp.concatenate([ones, ones, jnp.tile(k_norm[1], NSA_KV_HEADS), ones,
                            jnp.tile(k_norm[2], NSA_KV_HEADS), ones])
    gain = jnp.broadcast_to(gain[None], (SUBLANES, KV_W))
    return pl.pallas_call(
        _kv_finish_kernel,
        grid=(n // NORM_ROWS,),
        in_specs=[pl.BlockSpec((NORM_ROWS, KV_W), lambda i: (i, 0)),
                  pl.BlockSpec((SUBLANES, KV_W), lambda i: (0, 0))],
        out_specs=[pl.BlockSpec((NORM_ROWS, KV_W), lambda i: (i, 0)),
                   pl.BlockSpec((NORM_ROWS, KV_W), lambda i: (i, 0))],
        out_shape=[jax.ShapeDtypeStruct((n, KV_W), jnp.float32),
                   jax.ShapeDtypeStruct((n, KV_W), jnp.bfloat16)],
        compiler_params=_params("parallel"),
        name="kv_finish",
    )(kv, gain)


def _compress_prompt_kernel(rows_ref, pw_ref, wcmp_ref, kn_ref, ck_ref, cv_ref, *, nb):
    pooled_k = _hdot(pw_ref[0], rows_ref[:, 0:KV_HALF])
    pooled_v = _hdot(pw_ref[1], rows_ref[:, KV_HALF:2 * KV_HALF])
    ck_ref[...] = jnp.zeros_like(ck_ref)
    cv_ref[...] = jnp.zeros_like(cv_ref)
    for g in range(NSA_KV_HEADS):
        sl = slice(g * HEAD_DIM, (g + 1) * HEAD_DIM)
        k = _hdot(pooled_k[:, sl], wcmp_ref[0, g])
        k = k * lax.rsqrt(jnp.mean(k * k, axis=-1, keepdims=True) + EPS) * kn_ref[...]
        v = _hdot(pooled_v[:, sl], wcmp_ref[1, g])
        ck_ref[0, 0:nb, sl] = k.astype(ck_ref.dtype)
        cv_ref[0, 0:nb, sl] = v.astype(cv_ref.dtype)


def compress_prompt(kv32, cmp_pos_w, w_cmp, kn_cmp, *, batch, seq):
    nb = seq // BLOCK
    pw = jnp.einsum('nm,jc->cnmj', jnp.eye(nb, dtype=jnp.float32), cmp_pos_w).reshape(2, nb, seq)
    shape = jax.ShapeDtypeStruct((batch, NB_PAD, KV_HALF), jnp.bfloat16)
    return pl.pallas_call(
        functools.partial(_compress_prompt_kernel, nb=nb),
        grid=(batch,),
        in_specs=[pl.BlockSpec((seq, 2 * KV_HALF), lambda b: (b, 0)),
                  pl.BlockSpec((2, nb, seq), lambda b: (0, 0, 0)),
                  pl.BlockSpec((2, NSA_KV_HEADS, HEAD_DIM, HEAD_DIM), lambda b: (0, 0, 0, 0)),
                  pl.BlockSpec((1, HEAD_DIM), lambda b: (0, 0))],
        out_specs=[pl.BlockSpec((1, NB_PAD, KV_HALF), lambda b: (b, 0, 0)),
                   pl.BlockSpec((1, NB_PAD, KV_HALF), lambda b: (b, 0, 0))],
        out_shape=[shape, shape],
        compiler_params=_params("parallel"),
        name="compress_prompt",
    )(kv32, pw, w_cmp, kn_cmp.reshape(1, HEAD_DIM))


NSA_TQ = 256
NSA_PROMPT_CALLS = 4


def _select_mask(imp, qpos, nb):
    lane = _iota2(imp.shape, 1)
    cur = qpos // BLOCK
    causal = lane <= cur
    forced = (lane == 0) | (causal & (lane > cur - N_LOCAL))
    score = jnp.where(forced, FORCE, jnp.where(causal, imp, -1.0))
    score = jnp.where(lane < nb, score, -2.0)
    rank = jnp.zeros(imp.shape, jnp.float32)
    for j in range(nb):
        cj = score[:, j:j + 1]
        rank = rank + jnp.where((cj > score) | ((cj == score) & (lane > j)), 1.0, 0.0)
    return (rank < float(min(N_SELECT, nb))) & (lane < nb)


def _nsa_prompt_kernel(q_ref, gl_ref, ck_ref, cv_ref, ksel_ref, vsel_ref, kwin_ref, vwin_ref,
                       qn_ref, sl_ref, ex_ref, *rest, tq, seq, tile0, n_keys):
    o_ref = rest[-1]
    nb = seq // BLOCK
    t0 = (tile0 + pl.program_id(2)) * tq
    qpos = t0 + _iota2((tq, 1), 0)
    gates = _sigmoid(gl_ref[...])
    qs = []
    for r in range(GROUP):
        x = q_ref[:, r * HEAD_DIM:(r + 1) * HEAD_DIM]
        x = x * lax.rsqrt(jnp.mean(x * x, axis=-1, keepdims=True) + EPS) * qn_ref[...]
        qs.append((x * (SCALE * LOG2E)).astype(jnp.bfloat16))
    slopes = [sl_ref[0, r:r + 1, 0:1] * LOG2E for r in range(GROUP)]

    lane = _iota2((1, NB_PAD), 1)
    blk_end = (lane + 1) * BLOCK - 1
    valid_c = (qpos >= blk_end) & (lane < nb)
    off_c = jnp.where(lane < nb, NEG, -jnp.inf)
    any_c = (qpos >= BLOCK - 1).astype(jnp.float32)
    rel_c = (blk_end - t0).astype(jnp.float32)
    ck = ck_ref[0]
    cv = cv_ref[0]
    imp = jnp.zeros((tq, NB_PAD), jnp.float32)
    outs = []
    for r in range(GROUP):
        s = _bdot(qs[r], ck, _NT) + slopes[r] * rel_c
        e, l = _softmax_rows(jnp.where(valid_c, s, off_c), jnp.exp2)
        p = e / l * any_c
        imp = imp + p
        outs.append(gates[:, r:r + 1] * _bdot(p, cv))

    sel = _select_mask(imp, qpos, nb).astype(jnp.bfloat16)
    kpos = _iota2((1, n_keys), 1)
    allowed = (_bdot(sel, ex_ref[:, 0:n_keys]) > 0.5) & (kpos <= qpos)
    mask_s = jnp.where(allowed, 0.0, NEG)
    rel_s = (kpos - t0).astype(jnp.float32)
    ksel = ksel_ref[0:n_keys, :]
    vsel = vsel_ref[0:n_keys, :]
    for r in range(GROUP):
        s = _bdot(qs[r], ksel, _NT) + slopes[r] * rel_s + mask_s
        e, l = _softmax_rows(s, jnp.exp2)
        outs[r] = outs[r] + gates[:, GROUP + r:GROUP + r + 1] * (_bdot(e, vsel) / l)

    nw = WINDOW + tq
    start = pl.multiple_of(jnp.maximum(t0 - WINDOW, 0), LANES)
    kwin = kwin_ref[pl.ds(start, nw), :]
    vwin = vwin_ref[pl.ds(start, nw), :]
    kpos_w = start + _iota2((1, nw), 1)
    dist = qpos - kpos_w
    mask_w = jnp.where((dist >= 0) & (dist < WINDOW), 0.0, NEG)
    rel_w = (kpos_w - t0).astype(jnp.float32)
    for r in range(GROUP):
        s = _bdot(qs[r], kwin, _NT) + slopes[r] * rel_w + mask_w
        e, l = _softmax_rows(s, jnp.exp2)
        o = outs[r] + gates[:, 2 * GROUP + r:2 * GROUP + r + 1] * (_bdot(e, vwin) / l)
        o_ref[:, r * HEAD_DIM:(r + 1) * HEAD_DIM] = o.astype(o_ref.dtype)


def nsa_slopes():
    h = jnp.arange(1, NSA_HEADS + 1, dtype=jnp.float32)
    s = (2.0 ** (-8.0 * h / NSA_HEADS)).reshape(NSA_KV_HEADS, GROUP, 1)
    return jnp.broadcast_to(s, (NSA_KV_HEADS, GROUP, LANES))


def nsa_gate_cols(w_g):
    d = w_g.shape[0]
    w = w_g.reshape(d, 3, NSA_KV_HEADS, GROUP).transpose(0, 2, 1, 3).reshape(d, NSA_KV_HEADS, 3 * GROUP)
    return jnp.pad(w, ((0, 0), (0, 0), (0, LANES - 3 * GROUP))).reshape(d, NSA_KV_HEADS * LANES)


def nsa_prompt(q, gl, ck, cv, kv16, q_norm, *, batch, seq, tq=NSA_TQ):
    nt = seq // tq
    expand = (jnp.arange(NB_PAD)[:, None] == (jnp.arange(seq)[None, :] // BLOCK)).astype(jnp.bfloat16)
    kvw = 2 * NSA_KV_HEADS
    tiles_per_call = nt // NSA_PROMPT_CALLS

    def kvspec(col0):
        return pl.BlockSpec((seq, HEAD_DIM), lambda b, g, i: (b, col0 + g))

    out = None
    for part in range(NSA_PROMPT_CALLS):
        tile0 = part * tiles_per_call

        def rows(b, g, i, tile0=tile0):
            return (b * nt + tile0 + i, g)

        operands = [q, gl, ck, cv, kv16, kv16, kv16, kv16, q_norm.reshape(1, HEAD_DIM), nsa_slopes(), expand]
        in_specs = [pl.BlockSpec((tq, GROUP * HEAD_DIM), rows),
                    pl.BlockSpec((tq, LANES), rows),
                    pl.BlockSpec((1, NB_PAD, HEAD_DIM), lambda b, g, i: (b, 0, g)),
                    pl.BlockSpec((1, NB_PAD, HEAD_DIM), lambda b, g, i: (b, 0, g)),
                    kvspec(kvw), kvspec(kvw + NSA_KV_HEADS), kvspec(2 * kvw), kvspec(2 * kvw + NSA_KV_HEADS),
                    pl.BlockSpec((1, HEAD_DIM), lambda b, g, i: (0, 0)),
                    pl.BlockSpec((1, GROUP, LANES), lambda b, g, i: (g, 0, 0)),
                    pl.BlockSpec((NB_PAD, seq), lambda b, g, i: (0, 0))]
        aliases = {}
        if out is not None:
            aliases = {len(operands): 0}
            operands.append(out)
            in_specs.append(pl.BlockSpec(memory_space=pl.ANY))
        out = pl.pallas_call(
            functools.partial(_nsa_prompt_kernel, tq=tq, seq=seq, tile0=tile0, n_keys=(tile0 + tiles_per_call) * tq),
            grid=(batch, NSA_KV_HEADS, tiles_per_call),
            in_specs=in_specs,
            out_specs=pl.BlockSpec((tq, GROUP * HEAD_DIM), rows),
            out_shape=jax.ShapeDtypeStruct((batch * seq, NSA_HEADS * HEAD_DIM), jnp.bfloat16),
            input_output_aliases=aliases,
            compiler_params=_params("parallel", "parallel", "arbitrary"),
            name="nsa_prompt",
        )(*operands)
    return out


KV_SLOTS = 2 * NSA_KV_HEADS
N_PAGES = PAST_LEN // PAGE_SIZE
PAGES_PER_STEP = 8
PAGE_ROWS = PAGE_SIZE * KV_SLOTS
BLOCK_ROWS = BLOCK * KV_SLOTS
BLOCKS_PER_PAGE = PAGE_SIZE // BLOCK
N_PAST_BLOCKS = PAST_LEN // BLOCK
N_SAMPLE_BLOCKS = -(-(PAST_LEN + DEC_SEQ) // BLOCK)
SEL_LANES = -(-N_SAMPLE_BLOCKS // LANES) * LANES
QROWS = GROUP * DEC_SEQ


def _slot_rows(ref, lead, slot, n):
    return ref[lead, pl.ds(slot, n, stride=KV_SLOTS), :]


def _page_specs():
    def spec(j):
        return pl.BlockSpec((1, PAGE_ROWS, HEAD_DIM),
                            lambda b, i, pt: (pt[b * N_PAGES + PAGES_PER_STEP * i + j], 0, 0))
    return [spec(j) for j in range(PAGES_PER_STEP)]


def _paged(cache):
    return [cache.reshape(-1, PAGE_ROWS, HEAD_DIM)] * PAGES_PER_STEP


def _cmp_pool_kernel(pt_ref, *refs):
    pages, (w_ref, o_ref) = refs[:PAGES_PER_STEP], refs[PAGES_PER_STEP:]
    i = pl.program_id(1)
    tiles = []
    for p in pages:
        prod = p[0] * w_ref[...]
        for h in range(BLOCKS_PER_PAGE):
            blk = prod[h * BLOCK_ROWS:(h + 1) * BLOCK_ROWS].reshape(BLOCK, KV_SLOTS, HEAD_DIM)
            tiles.append(jnp.sum(blk, axis=0))
    n = PAGES_PER_STEP * BLOCKS_PER_PAGE * KV_SLOTS
    o_ref[0, pl.ds(pl.multiple_of(i * n, n), n), :] = jnp.concatenate(tiles, axis=0)


def cmp_pool_pages(cache, page_table, cmp_pos_w):
    w = jnp.repeat(jnp.tile(cmp_pos_w, (BLOCKS_PER_PAGE, 1)), NSA_KV_HEADS, axis=1)
    wt = jnp.broadcast_to(w.reshape(PAGE_ROWS, 1), (PAGE_ROWS, HEAD_DIM))
    return pl.pallas_call(
        _cmp_pool_kernel,
        grid_spec=pltpu.PrefetchScalarGridSpec(
            num_scalar_prefetch=1,
            grid=(DEC_BATCH, N_PAGES // PAGES_PER_STEP),
            in_specs=_page_specs() + [pl.BlockSpec((PAGE_ROWS, HEAD_DIM), lambda b, i, pt: (0, 0))],
            out_specs=pl.BlockSpec((1, N_PAST_BLOCKS * KV_SLOTS, HEAD_DIM), lambda b, i, pt: (b, 0, 0))),
        out_shape=jax.ShapeDtypeStruct((DEC_BATCH, N_PAST_BLOCKS * KV_SLOTS, HEAD_DIM), jnp.float32),
        compiler_params=_params("parallel", "arbitrary"),
        name="cmp_pool_pages",
    )(page_table.reshape(-1), *_paged(cache), wt)


def _sample_queries(q_ref, qn_ref, g):
    parts = []
    for r in range(GROUP):
        c0 = (g * GROUP + r) * HEAD_DIM
        x = q_ref[:, c0:c0 + HEAD_DIM]
        x = x * lax.rsqrt(jnp.mean(x * x, axis=-1, keepdims=True) + EPS) * qn_ref[...]
        parts.append(x * SCALE)
    return jnp.concatenate(parts, axis=0).astype(jnp.bfloat16)


def _head_major_col(x, lane0):
    return jnp.concatenate([x[:, lane0 + r:lane0 + r + 1] for r in range(GROUP)], axis=0)


def _slope_col(sl_ref, g):
    return jnp.concatenate([jnp.broadcast_to(sl_ref[g, r:r + 1, 0:1], (DEC_SEQ, 1)) for r in range(GROUP)], axis=0)


def _sample_select_kernel(pooled_ref, wcmp_ref, kn_ref, q_ref, gl_ref, qn_ref, sl_ref, wcache_ref, new_ref, ex_ref,
                          ocw_ref, mask_ref, newmask_ref):
    srow = _iota2((QROWS, 1), 0) % DEC_SEQ
    qpos = PAST_LEN + srow
    gl = gl_ref[...]
    nbp = N_PAST_BLOCKS
    lane_c = _iota2((1, nbp), 1)
    blk_end = (lane_c + 1) * BLOCK - 1
    rel_c = (blk_end - PAST_LEN).astype(jnp.float32)
    nw = wcache_ref.shape[1] // KV_SLOTS
    nwk = nw + LANES
    jw = _iota2((1, nwk), 1)
    kpos_w = jnp.where(jw < nw, PAST_LEN - nw + jw, PAST_LEN + jw - nw)
    dist_w = qpos - kpos_w
    ok_w = (dist_w >= 0) & (dist_w < WINDOW) & (jw < nw + DEC_SEQ)
    mask_w = jnp.where(ok_w, 0.0, NEG)
    rel_w = (kpos_w - PAST_LEN).astype(jnp.float32)
    pad_rows = jnp.zeros((LANES - DEC_SEQ, HEAD_DIM), jnp.float32)
    for g in range(NSA_KV_HEADS):
        ck = _hdot(_slot_rows(pooled_ref, 0, g, nbp), wcmp_ref[0, g])
        ck = ck * lax.rsqrt(jnp.mean(ck * ck, axis=-1, keepdims=True) + EPS) * kn_ref[...]
        cv = _hdot(_slot_rows(pooled_ref, 0, NSA_KV_HEADS + g, nbp), wcmp_ref[1, g])
        qg = _sample_queries(q_ref, qn_ref, g)
        slope = _slope_col(sl_ref, g)
        gates = _sigmoid(gl[:, g * LANES:(g + 1) * LANES])
        s = _bdot(qg, ck, _NT) + slope * rel_c
        e, l = _softmax_rows(jnp.where(qpos >= blk_end, s, NEG))
        p = e / l
        o = _head_major_col(gates, 0) * _bdot(p, cv)
        imp = p[0:DEC_SEQ]
        for r in range(1, GROUP):
            imp = imp + p[r * DEC_SEQ:(r + 1) * DEC_SEQ]
        imp = jnp.concatenate([imp, jnp.zeros((DEC_SEQ, SEL_LANES - nbp), jnp.float32)], axis=1)
        sel = _select_mask(imp, PAST_LEN + _iota2((DEC_SEQ, 1), 0), N_SAMPLE_BLOCKS).astype(jnp.bfloat16)
        for half in range(nbp // LANES):
            keys = _bdot(sel[:, half * LANES:(half + 1) * LANES], ex_ref[...])
            mask_ref[0, g * DEC_SEQ:(g + 1) * DEC_SEQ, half * LANES * BLOCK:(half + 1) * LANES * BLOCK] = keys
        newmask_ref[0, g * DEC_SEQ:(g + 1) * DEC_SEQ, :] = jnp.broadcast_to(
            sel[:, nbp:nbp + 1].astype(jnp.float32), (DEC_SEQ, LANES))
        c0 = 4 * KV_HALF + g * HEAD_DIM
        kw = jnp.concatenate([_slot_rows(wcache_ref, 0, g, nw), new_ref[:, c0:c0 + HEAD_DIM], pad_rows], axis=0)
        vw = jnp.concatenate([_slot_rows(wcache_ref, 0, NSA_KV_HEADS + g, nw),
                              new_ref[:, c0 + KV_HALF:c0 + KV_HALF + HEAD_DIM], pad_rows], axis=0)
        s = _bdot(qg, kw, _NT) + slope * rel_w + mask_w
        e, l = _softmax_rows(s)
        o = o + _head_major_col(gates, 2 * GROUP) * (_bdot(e, vw) / l)
        for r in range(GROUP):
            c0 = (g * GROUP + r) * HEAD_DIM
            ocw_ref[:, c0:c0 + HEAD_DIM] = o[r * DEC_SEQ:(r + 1) * DEC_SEQ]


def nsa_sample_select(pooled, w_cmp, kn_cmp, q, gl, q_norm, cache_win, kv32):
    expand = (jnp.arange(LANES)[:, None] == (jnp.arange(LANES * BLOCK)[None, :] // BLOCK)).astype(jnp.bfloat16)
    row0 = N_PROMPT // DEC_SEQ
    nw = cache_win.shape[1]
    return pl.pallas_call(
        _sample_select_kernel,
        grid=(DEC_BATCH,),
        in_specs=[pl.BlockSpec((1, N_PAST_BLOCKS * KV_SLOTS, HEAD_DIM), lambda b: (b, 0, 0)),
                  pl.BlockSpec((2, NSA_KV_HEADS, HEAD_DIM, HEAD_DIM), lambda b: (0, 0, 0, 0)),
                  pl.BlockSpec((1, HEAD_DIM), lambda b: (0, 0)),
                  pl.BlockSpec((DEC_SEQ, NSA_HEADS * HEAD_DIM), lambda b: (row0 + b, 0)),
                  pl.BlockSpec((DEC_SEQ, NSA_KV_HEADS * LANES), lambda b: (row0 + b, 0)),
                  pl.BlockSpec((1, HEAD_DIM), lambda b: (0, 0)),
                  pl.BlockSpec((NSA_KV_HEADS, GROUP, LANES), lambda b: (0, 0, 0)),
                  pl.BlockSpec((1, nw * KV_SLOTS, HEAD_DIM), lambda b: (b, 0, 0)),
                  pl.BlockSpec((DEC_SEQ, KV_W), lambda b: (row0 + b, 0)),
                  pl.BlockSpec((LANES, LANES * BLOCK), lambda b: (0, 0))],
        out_specs=[pl.BlockSpec((DEC_SEQ, NSA_HEADS * HEAD_DIM), lambda b: (b, 0)),
                   pl.BlockSpec((1, NSA_KV_HEADS * DEC_SEQ, PAST_LEN), lambda b: (b, 0, 0)),
                   pl.BlockSpec((1, NSA_KV_HEADS * DEC_SEQ, LANES), lambda b: (b, 0, 0))],
        out_shape=[jax.ShapeDtypeStruct((N_SAMPLE, NSA_HEADS * HEAD_DIM), jnp.float32),
                   jax.ShapeDtypeStruct((DEC_BATCH, NSA_KV_HEADS * DEC_SEQ, PAST_LEN), jnp.float32),
                   jax.ShapeDtypeStruct((DEC_BATCH, NSA_KV_HEADS * DEC_SEQ, LANES), jnp.float32)],
        compiler_params=_params("parallel"),
        name="nsa_sample_select",
    )(pooled, w_cmp, kn_cmp.reshape(1, HEAD_DIM), q, gl, q_norm.reshape(1, HEAD_DIM), nsa_slopes(),
      cache_win.reshape(DEC_BATCH, nw * KV_SLOTS, HEAD_DIM), kv32, expand)


def _sample_sel_kernel(pt_ref, *refs):
    pages = refs[:PAGES_PER_STEP]
    (mask_ref, newmask_ref, q_ref, gl_ref, qn_ref, sl_ref, new_ref, ocw_ref,
     o_ref, qs, m_sc, l_sc, acc_sc) = refs[PAGES_PER_STEP:]
    i = pl.program_id(1)
    groups = range(NSA_KV_HEADS)

    @pl.when(i == 0)
    def _():
        for g in groups:
            qs[g] = _sample_queries(q_ref, qn_ref, g)
        m_sc[...] = jnp.full(m_sc.shape, NEG, jnp.float32)
        l_sc[...] = jnp.zeros(l_sc.shape, jnp.float32)
        acc_sc[...] = jnp.zeros(acc_sc.shape, jnp.float32)

    def accumulate(keys, vals, rel, keep8):
        keep = [jnp.concatenate([keep8[g]] * GROUP, axis=0) for g in groups]
        s = [_bdot(qs[g], keys[g], _NT) + _slope_col(sl_ref, g) * rel + jnp.where(keep[g] > 0.5, 0.0, NEG)
             for g in groups]
        m_old = [m_sc[g] for g in groups]
        m_new = [jnp.maximum(m_old[g], jnp.max(s[g], axis=-1, keepdims=True)) for g in groups]
        p = [jnp.exp(s[g] - m_new[g]) * keep[g] for g in groups]
        pv = [_bdot(p[g], vals[g]) for g in groups]
        for g in groups:
            alpha = jnp.exp(m_old[g] - m_new[g])
            l_sc[g] = alpha * l_sc[g] + jnp.sum(p[g], axis=-1, keepdims=True)
            acc_sc[g] = alpha * acc_sc[g] + pv[g]
            m_sc[g] = m_new[g]

    lane = _iota2((1, PAGE_SIZE), 1)
    for j, page in enumerate(pages):
        keep_all = mask_ref[0, :, j * PAGE_SIZE:(j + 1) * PAGE_SIZE]

        @pl.when(jnp.max(keep_all) > 0.5)
        def _(j=j, page=page, keep_all=keep_all):
            rel = ((i * PAGES_PER_STEP + j) * PAGE_SIZE - PAST_LEN + lane).astype(jnp.float32)
            accumulate([_slot_rows(page, 0, g, PAGE_SIZE) for g in groups],
                       [_slot_rows(page, 0, NSA_KV_HEADS + g, PAGE_SIZE) for g in groups],
                       rel, [keep_all[g * DEC_SEQ:(g + 1) * DEC_SEQ] for g in groups])

    @pl.when(i == pl.num_programs(1) - 1)
    def _():
        gl = gl_ref[...]
        pad_rows = jnp.zeros((PAGE_SIZE - DEC_SEQ, HEAD_DIM), jnp.float32)
        causal = ((lane <= _iota2((DEC_SEQ, 1), 0)) & (lane < DEC_SEQ)).astype(jnp.float32)
        c0 = 2 * KV_HALF
        accumulate([jnp.concatenate([new_ref[:, c0 + g * HEAD_DIM:c0 + (g + 1) * HEAD_DIM], pad_rows], axis=0)
                    for g in groups],
                   [jnp.concatenate([new_ref[:, c0 + KV_HALF + g * HEAD_DIM:c0 + KV_HALF + (g + 1) * HEAD_DIM], pad_rows],
                                    axis=0) for g in groups],
                   lane.astype(jnp.float32),
                   [newmask_ref[0, g * DEC_SEQ:(g + 1) * DEC_SEQ, :] * causal for g in groups])
        for g in groups:
            gates = _sigmoid(gl[:, g * LANES:(g + 1) * LANES])
            o = _head_major_col(gates, GROUP) * (acc_sc[g] / l_sc[g])
            for r in range(GROUP):
                c0 = (g * GROUP + r) * HEAD_DIM
                o_ref[:, c0:c0 + HEAD_DIM] = ocw_ref[:, c0:c0 + HEAD_DIM] + o[r * DEC_SEQ:(r + 1) * DEC_SEQ]


def nsa_sample_sel(cache, page_table, mask, newmask, q, gl, q_norm, kv32, ocw):
    row0 = N_PROMPT // DEC_SEQ
    nrow = NSA_KV_HEADS * DEC_SEQ
    step_keys = PAGES_PER_STEP * PAGE_SIZE
    return pl.pallas_call(
        _sample_sel_kernel,
        grid_spec=pltpu.PrefetchScalarGridSpec(
            num_scalar_prefetch=1,
            grid=(DEC_BATCH, N_PAGES // PAGES_PER_STEP),
            in_specs=_page_specs() + [
                pl.BlockSpec((1, nrow, step_keys), lambda b, i, pt: (b, 0, i)),
                pl.BlockSpec((1, nrow, LANES), lambda b, i, pt: (b, 0, 0)),
                pl.BlockSpec((DEC_SEQ, NSA_HEADS * HEAD_DIM), lambda b, i, pt: (row0 + b, 0)),
                pl.BlockSpec((DEC_SEQ, NSA_KV_HEADS * LANES), lambda b, i, pt: (row0 + b, 0)),
                pl.BlockSpec((1, HEAD_DIM), lambda b, i, pt: (0, 0)),
                pl.BlockSpec((NSA_KV_HEADS, GROUP, LANES), lambda b, i, pt: (0, 0, 0)),
                pl.BlockSpec((DEC_SEQ, KV_W), lambda b, i, pt: (row0 + b, 0)),
                pl.BlockSpec((DEC_SEQ, NSA_HEADS * HEAD_DIM), lambda b, i, pt: (b, 0))],
            out_specs=pl.BlockSpec((DEC_SEQ, NSA_HEADS * HEAD_DIM), lambda b, i, pt: (b, 0)),
            scratch_shapes=[pltpu.VMEM((NSA_KV_HEADS, QROWS, HEAD_DIM), jnp.bfloat16),
                            pltpu.VMEM((NSA_KV_HEADS, QROWS, 1), jnp.float32),
                            pltpu.VMEM((NSA_KV_HEADS, QROWS, 1), jnp.float32),
                            pltpu.VMEM((NSA_KV_HEADS, QROWS, HEAD_DIM), jnp.float32)]),
        out_shape=jax.ShapeDtypeStruct((N_SAMPLE, NSA_HEADS * HEAD_DIM), jnp.float32),
        compiler_params=_params("parallel", "arbitrary"),
        name="nsa_sample_sel",
    )(page_table.reshape(-1), *_paged(cache), mask, newmask,
      q, gl, q_norm.reshape(1, HEAD_DIM), nsa_slopes(), kv32, ocw)


def nsa_sample(q, gl, kv32, cache_cmp_kv, cache_sel_kv, cache_win_kv, page_table, cmp_pos_w, w_cmp, kn_cmp, q_norm):
    pooled = cmp_pool_pages(cache_cmp_kv, page_table, cmp_pos_w)
    ocw, mask, newmask = nsa_sample_select(pooled, w_cmp, kn_cmp, q, gl, q_norm, cache_win_kv, kv32)
    return nsa_sample_sel(cache_sel_kv, page_table, mask, newmask, q, gl, q_norm, kv32, ocw)


def _split_rows(x):
    return (x[:N_PROMPT].reshape((BATCH, SEQ) + x.shape[1:]),
            x[N_PROMPT:].reshape((DEC_BATCH, DEC_SEQ) + x.shape[1:]))


def kernel(x_prompt, x_sample, state_conv, state_delta, cache_cmp_kv, cache_sel_kv, cache_win_kv, page_table,
           ffn_norm, ffn_w_gate, ffn_w_up, ffn_w_down, mix_norm,
           gdn_w_in, gdn_conv_w, gdn_a_log, gdn_dt_bias, gdn_o_norm, gdn_w_out,
           kv_norm, w_kv, cmp_pos_w, w_cmp, k_norm, nsa_w_q, nsa_q_norm, nsa_w_o):
    bf = jnp.bfloat16
    n_main = GDN_CONV_CH + GDN_V
    n_q = NSA_HEADS * HEAD_DIM
    hist = CONV_W - 1

    def ffn_half(h, hg16, ssq, layer, i, next_gains):
        act, wd16 = swiglu_up(hg16, ssq, ffn_w_gate, ffn_w_up, ffn_w_down, (layer, i))
        return matmul_residual(act, wd16, (), h, 0.5, next_gains, tn=256, tm=DOWN_ROWS)

    h, hg16, ssq = rows_prep(x_prompt.reshape(N_PROMPT, D_MODEL), x_sample.reshape(N_SAMPLE, D_MODEL), ffn_norm[0, 0])
    h, (hg16,), ssq = ffn_half(h, hg16, ssq, 0, 0, [mix_norm[0]])
    proj = matmul_normed(hg16, ssq, gdn_w_in, (0,), n_cols=n_main, tn=512)
    ab = matmul_normed(hg16, ssq, gdn_group_cols(gdn_w_in[0, :, n_main:]), tn=LANES)
    conv_w8 = jnp.pad(gdn_conv_w[0], ((0, CONV_PAD - CONV_W), (0, 0)))
    hp = gdn_head_params(gdn_a_log[0], gdn_dt_bias[0])

    def pad_sample(x):
        x = x[N_PROMPT:].reshape(DEC_BATCH, DEC_SEQ, x.shape[1])
        return jnp.pad(x, ((0, 0), (0, GDN_CHUNK - DEC_SEQ), (0, 0))).reshape(DEC_BATCH * GDN_CHUNK, x.shape[2])

    o_p, delta_p = gdn_mixer(proj, ab, jnp.zeros((BATCH, CONV_PAD, GDN_CONV_CH), jnp.float32),
                             jnp.zeros((BATCH, GDN_HEADS, GDN_DK, GDN_DV), jnp.float32),
                             conv_w8, hp, gdn_o_norm[0], batch=BATCH, n_chunks=SEQ // GDN_CHUNK)
    o_s, delta_s = gdn_mixer(pad_sample(proj), pad_sample(ab),
                             jnp.pad(state_conv[0], ((0, 0), (CONV_PAD - hist, 0), (0, 0))), state_delta[0],
                             conv_w8, hp, gdn_o_norm[0], batch=DEC_BATCH, n_chunks=1, valid_rows=DEC_SEQ)
    o_s = o_s.reshape(DEC_BATCH, GDN_CHUNK, GDN_V)[:, :DEC_SEQ].reshape(N_SAMPLE, GDN_V)
    conv_p = jnp.stack([lax.slice(proj, ((b + 1) * SEQ - hist, 0), ((b + 1) * SEQ, GDN_CONV_CH)) for b in range(BATCH)])
    u_s = lax.slice(proj, (N_PROMPT, 0), (N_ROWS, GDN_CONV_CH)).reshape(DEC_BATCH, DEC_SEQ, GDN_CONV_CH)
    conv_s = jnp.concatenate([state_conv[0], u_s], axis=1)[:, DEC_SEQ:]
    h, (hg16,), ssq = matmul_residual(jnp.concatenate([o_p, o_s], axis=0), gdn_w_out, (0,), h, 1.0, [ffn_norm[0, 1]],
                                      tn=256, tm=MM_ROWS)
    h, (hg16_kv, hg16), ssq = ffn_half(h, hg16, ssq, 0, 1, [kv_norm, ffn_norm[1, 0]])

    kv32, kv16 = kv_finish(matmul_normed(hg16_kv, ssq, w_kv, tn=512), k_norm)
    kv5 = kv32.reshape(N_ROWS, 3, 2, NSA_KV_HEADS, HEAD_DIM)
    cmp_p, cmp_s = _split_rows(kv5[:, 0])
    sel_p, sel_s = _split_rows(kv5[:, 1])
    win_rows_p, win_rows_s = _split_rows(kv5[:, 2])

    h, (hg16,), ssq = ffn_half(h, hg16, ssq, 1, 0, [mix_norm[1]])
    q = matmul_normed(hg16, ssq, nsa_w_q, (0,), n_cols=n_q, tn=512)
    gl = matmul_normed(hg16, ssq, nsa_gate_cols(nsa_w_q[0, :, n_q:]), tn=LANES)
    ck, cv = compress_prompt(kv32, cmp_pos_w, w_cmp, k_norm[0], batch=BATCH, seq=SEQ)
    o_p = nsa_prompt(q, gl, ck, cv, kv16, nsa_q_norm[0], batch=BATCH, seq=SEQ)
    o_s = nsa_sample(q, gl, kv32, cache_cmp_kv, cache_sel_kv, cache_win_kv, page_table, cmp_pos_w, w_cmp, k_norm[0],
                     nsa_q_norm[0])
    h, (hg16,), ssq = matmul_residual(jnp.concatenate([o_p, o_s.astype(bf)], axis=0), nsa_w_o, (0,), h, 1.0,
                                      [ffn_norm[1, 1]], tn=256, tm=MM_ROWS)
    h, _, _ = ffn_half(h, hg16, ssq, 1, 1, [])

    y_p, y_s = _split_rows(h)
    win_p = win_rows_p[:, -min(WINDOW, SEQ):]
    win_s = jnp.concatenate([cache_win_kv, win_rows_s], axis=1)[:, DEC_SEQ:]
    return (y_p, y_s, conv_p[None], conv_s[None], delta_p[None], delta_s[None],
            cmp_p, cmp_s, sel_p, sel_s, win_p, win_s)
```

```python
import functools
import math

import jax
import jax.numpy as jnp
from jax import lax
from jax.experimental import pallas as pl
from jax.experimental.pallas import tpu as pltpu

D_MODEL = 4096
BATCH = 4
SEQ = 2048
DEPTH = 2
DEC_BATCH = 8
DEC_SEQ = 8
PAST_LEN = 16384
PAGE_SIZE = 128
D_FF = 11008
EPS = 1e-6
GDN_HEADS = 16
GDN_DK = 128
GDN_DV = 256
CONV_W = 4
GDN_CHUNK = 64
GDN_QK = GDN_HEADS * GDN_DK
GDN_V = GDN_HEADS * GDN_DV
GDN_CONV_CH = 2 * GDN_QK + GDN_V
NSA_HEADS = 32
NSA_KV_HEADS = 4
HEAD_DIM = 128
GROUP = NSA_HEADS // NSA_KV_HEADS
BLOCK = 64
N_SELECT = 16
N_LOCAL = 2
WINDOW = 512
SCALE = HEAD_DIM ** -0.5
NEG = -1e30
LOG2E = 1.0 / math.log(2.0)
FORCE = 1e4

N_PROMPT = BATCH * SEQ
N_SAMPLE = DEC_BATCH * DEC_SEQ
N_ROWS = N_PROMPT + N_SAMPLE

VMEM_LIMIT_BYTES = 56 * 1024 * 1024
LANES = 128
SUBLANES = 8

NORM_ROWS = 192
MM_ROWS = 1376
DOWN_ROWS = 688


def _params(*sem):
    return pltpu.CompilerParams(dimension_semantics=sem, vmem_limit_bytes=VMEM_LIMIT_BYTES)


PREP_ROWS = 64


def _lane_fold(x):
    acc = x[:, 0:LANES]
    for c in range(1, x.shape[1] // LANES):
        acc = acc + x[:, c * LANES:(c + 1) * LANES]
    return acc


def _rows_prep_kernel(xp_ref, xs_ref, g_ref, h_ref, h16_ref, ssq_ref):
    n_prompt_tiles = N_PROMPT // PREP_ROWS

    def emit(x):
        h_ref[...] = x
        h16_ref[...] = (x * g_ref[...]).astype(h16_ref.dtype)
        ssq_ref[...] = _lane_fold(x * x)

    @pl.when(pl.program_id(0) < n_prompt_tiles)
    def _():
        emit(xp_ref[...])

    @pl.when(pl.program_id(0) >= n_prompt_tiles)
    def _():
        emit(xs_ref[...])


def rows_prep(x_prompt, x_sample, gain):
    d = x_prompt.shape[1]
    n_p = N_PROMPT // PREP_ROWS
    n = N_ROWS // PREP_ROWS
    return pl.pallas_call(
        _rows_prep_kernel,
        grid=(n,),
        in_specs=[pl.BlockSpec((PREP_ROWS, d), lambda i: (jnp.minimum(i, n_p - 1), 0)),
                  pl.BlockSpec((PREP_ROWS, d), lambda i: (jnp.maximum(i - n_p, 0), 0)),
                  pl.BlockSpec((1, d), lambda i: (0, 0))],
        out_specs=[pl.BlockSpec((PREP_ROWS, d), lambda i: (i, 0)),
                   pl.BlockSpec((PREP_ROWS, d), lambda i: (i, 0)),
                   pl.BlockSpec((PREP_ROWS, LANES), lambda i: (i, 0))],
        out_shape=[jax.ShapeDtypeStruct((N_ROWS, d), jnp.float32),
                   jax.ShapeDtypeStruct((N_ROWS, d), jnp.bfloat16),
                   jax.ShapeDtypeStruct((N_ROWS, LANES), jnp.float32)],
        compiler_params=_params("parallel"),
        name="rows_prep",
    )(x_prompt, x_sample, gain.reshape(1, d))


def _row_scale(ssq_ref, k):
    return lax.rsqrt(jnp.sum(ssq_ref[...], axis=-1, keepdims=True) / k + EPS)


def _weight_spec(w, lead, tn):
    k = w.shape[-2]
    return pl.BlockSpec((None,) * len(lead) + (k, tn), lambda i, j: tuple(lead) + (0, j))


def _normed_specs(tm, k):
    return [pl.BlockSpec((tm, k), lambda i, j: (i, 0)),
            pl.BlockSpec((tm, LANES), lambda i, j: (i, 0))]


def _mm_kernel(x_ref, ssq_ref, w_ref, o_ref, *, w_is_transposed):
    w = w_ref[...].astype(jnp.bfloat16)
    if w_is_transposed:
        acc = lax.dot_general(x_ref[...], w, (((1,), (1,)), ((), ())), preferred_element_type=jnp.float32)
    else:
        acc = jnp.dot(x_ref[...], w, preferred_element_type=jnp.float32)
    o_ref[...] = (_row_scale(ssq_ref, x_ref.shape[1]) * acc).astype(o_ref.dtype)


def matmul_normed(xg16, ssq, w, lead=(), *, tn, n_cols=None, tm=MM_ROWS, w_is_transposed=False):
    m, k = xg16.shape
    n = n_cols or (w.shape[0] if w_is_transposed else w.shape[-1])
    w_spec = pl.BlockSpec((tn, k), lambda i, j: (j, 0)) if w_is_transposed else _weight_spec(w, lead, tn)
    return pl.pallas_call(
        functools.partial(_mm_kernel, w_is_transposed=w_is_transposed),
        grid=(m // tm, n // tn),
        in_specs=_normed_specs(tm, k) + [w_spec],
        out_specs=pl.BlockSpec((tm, tn), lambda i, j: (i, j)),
        out_shape=jax.ShapeDtypeStruct((m, n), jnp.float32),
        compiler_params=_params("parallel", "arbitrary"),
        name="matmul_normed",
    )(xg16, ssq, w)


def _mm_res_kernel(x_ref, w_ref, r_ref, *refs, scale, n_gains):
    gain_refs, o_ref, o16_refs = refs[:n_gains], refs[n_gains], refs[n_gains + 1:2 * n_gains + 1]
    w = w_ref[...].astype(jnp.bfloat16)
    h = r_ref[...] + scale * jnp.dot(x_ref[...], w, preferred_element_type=jnp.float32)
    o_ref[...] = h
    for g_ref, o16_ref in zip(gain_refs, o16_refs):
        o16_ref[...] = (h * g_ref[...]).astype(o16_ref.dtype)
    if n_gains:
        ssq_ref = refs[2 * n_gains + 1]
        part = _lane_fold(h * h)

        @pl.when(pl.program_id(1) == 0)
        def _():
            ssq_ref[...] = part

        @pl.when(pl.program_id(1) > 0)
        def _():
            ssq_ref[...] += part


def matmul_residual(x, w, lead, res, scale, gains, *, tn, tm):
    m, k = x.shape
    n = w.shape[-1]
    ng = len(gains)
    tile = pl.BlockSpec((tm, tn), lambda i, j: (i, j))
    out = pl.pallas_call(
        functools.partial(_mm_res_kernel, scale=scale, n_gains=ng),
        grid=(m // tm, n // tn),
        in_specs=[pl.BlockSpec((tm, k), lambda i, j: (i, 0)), _weight_spec(w, lead, tn), tile]
                 + [pl.BlockSpec((1, tn), lambda i, j: (0, j))] * ng,
        out_specs=[tile] * (1 + ng) + [pl.BlockSpec((tm, LANES), lambda i, j: (i, 0))] * min(ng, 1),
        out_shape=[jax.ShapeDtypeStruct((m, n), jnp.float32)] + [jax.ShapeDtypeStruct((m, n), jnp.bfloat16)] * ng
                  + [jax.ShapeDtypeStruct((m, LANES), jnp.float32)] * min(ng, 1),
        compiler_params=_params("parallel", "arbitrary"),
        name="matmul_residual",
    )(x, w, res, *[g.reshape(1, n) for g in gains])
    return out[0], list(out[1:1 + ng]), (out[1 + ng] if ng else None)


WD_SLAB_PASSES = 2


def _swiglu_up_kernel(x_ref, ssq_ref, wg_ref, wu_ref, wd_ref, o_ref, wd16_ref):
    x = x_ref[...]
    r = _row_scale(ssq_ref, x.shape[1])
    g = r * jnp.dot(x, wg_ref[...].astype(jnp.bfloat16), preferred_element_type=jnp.float32)
    u = r * jnp.dot(x, wu_ref[...].astype(jnp.bfloat16), preferred_element_type=jnp.float32)
    o_ref[...] = (g * jax.nn.sigmoid(g) * u).astype(o_ref.dtype)
    wd16_ref[...] = wd_ref[...].astype(wd16_ref.dtype)


def swiglu_up(xg16, ssq, wg, wu, wd, lead, tn=256, tm=MM_ROWS):
    m, k = xg16.shape
    n = wg.shape[-1]
    nj = n // tn
    n_slabs = WD_SLAB_PASSES * nj
    slab = wd.shape[-2] // n_slabs
    d_out = wd.shape[-1]

    def slab_index(i, j):
        return jnp.minimum(i * nj + j, n_slabs - 1)

    return pl.pallas_call(
        _swiglu_up_kernel,
        grid=(m // tm, nj),
        in_specs=_normed_specs(tm, k) + [
            _weight_spec(wg, lead, tn), _weight_spec(wu, lead, tn),
            pl.BlockSpec((None,) * len(lead) + (slab, d_out), lambda i, j: tuple(lead) + (slab_index(i, j), 0))],
        out_specs=[pl.BlockSpec((tm, tn), lambda i, j: (i, j)),
                   pl.BlockSpec((slab, d_out), lambda i, j: (slab_index(i, j), 0))],
        out_shape=[jax.ShapeDtypeStruct((m, n), jnp.bfloat16),
                   jax.ShapeDtypeStruct((wd.shape[-2], d_out), jnp.bfloat16)],
        compiler_params=_params("parallel", "arbitrary"),
        name="swiglu_up",
    )(xg16, ssq, wg, wu, wd)


_HI = lax.Precision.HIGHEST
_NT = (((1,), (1,)), ((), ()))
_TN = (((0,), (0,)), ((), ()))


def _bdot(a, b, dims=None):
    a = a.astype(jnp.bfloat16)
    b = b.astype(jnp.bfloat16)
    if dims is None:
        return jnp.dot(a, b, preferred_element_type=jnp.float32)
    return lax.dot_general(a, b, dims, preferred_element_type=jnp.float32)


def _hdot(a, b, dims=None):
    if dims is None:
        return jnp.dot(a, b, precision=_HI, preferred_element_type=jnp.float32)
    return lax.dot_general(a, b, dims, precision=_HI, preferred_element_type=jnp.float32)


def _iota2(shape, axis):
    return lax.broadcasted_iota(jnp.int32, shape, axis)


def _sigmoid(x):
    return 1.0 / (1.0 + jnp.exp(-x))


def _softmax_rows(s, exp=jnp.exp):
    m = jnp.max(s, axis=-1, keepdims=True)
    e = exp(s - m)
    return e, jnp.sum(e, axis=-1, keepdims=True)


GDN_HB = 16
CONV_PAD = SUBLANES


def _gdn_kernel(uq_ref, uk_ref, uv_ref, gate_ref, ab_ref, cq_ref, ck_ref, cv_ref, s0_ref,
                wq_ref, wk_ref, wv_ref, hp_ref, onorm_ref,
                o_ref, s_ref, extq, extk, extv, *, hb, chunk, valid_rows):
    C = chunk
    c = pl.program_id(2)

    @pl.when(c == 0)
    def _():
        s_ref[...] = s0_ref[...]
        extq[0:CONV_PAD, :] = cq_ref[0]
        extk[0:CONV_PAD, :] = ck_ref[0]
        extv[0:CONV_PAD, :] = cv_ref[0]

    def conv(ext, u_ref, w_ref):
        ext[CONV_PAD:CONV_PAD + C, :] = u_ref[...]
        base = CONV_PAD - (CONV_W - 1)
        acc = ext[base:base + C, :] * w_ref[0:1, :]
        for i in range(1, CONV_W):
            acc = acc + ext[base + i:base + i + C, :] * w_ref[i:i + 1, :]
        ext[0:CONV_PAD, :] = ext[C:C + CONV_PAD, :]
        return acc * _sigmoid(acc)

    qc = conv(extq, uq_ref, wq_ref)
    kc = conv(extk, uk_ref, wk_ref)
    vc = conv(extv, uv_ref, wv_ref)

    row = _iota2((C, C), 0)
    col = _iota2((C, C), 1)
    tri_incl = row >= col
    tri_strict = row > col
    row_ok = None
    if valid_rows < C:
        row_ok = _iota2((C, 1), 0) < valid_rows

    ab = ab_ref[...]
    x = ab + hp_ref[0, 1:2, :]
    softplus = jnp.maximum(x, 0.0) + jnp.log(1.0 + jnp.exp(-jnp.abs(x)))
    gmat = -jnp.exp(hp_ref[0, 0:1, :]) * softplus
    if row_ok is not None:
        gmat = jnp.where(row_ok, gmat, 0.0)
    beta = _sigmoid(ab)
    gam = _hdot(tri_incl.astype(jnp.float32), gmat)
    eye_l = (_iota2((LANES, LANES), 0) == _iota2((LANES, LANES), 1)).astype(jnp.float32)
    gam_t = _hdot(eye_l, gam, _NT)

    heads = range(hb)
    gc = [gam[:, h:h + 1] for h in heads]
    bc = [beta[:, hb + h:hb + h + 1] for h in heads]
    glast = [gam[C - 1:C, h:h + 1] for h in heads]
    q, k, v, decay = [], [], [], []
    for h in heads:
        qh = qc[:, h * GDN_DK:(h + 1) * GDN_DK]
        kh = kc[:, h * GDN_DK:(h + 1) * GDN_DK]
        vh = vc[:, h * GDN_DV:(h + 1) * GDN_DV]
        qh = qh * lax.rsqrt(jnp.sum(qh * qh, axis=-1, keepdims=True) + EPS) * GDN_DK ** -0.5
        kh = kh * lax.rsqrt(jnp.sum(kh * kh, axis=-1, keepdims=True) + EPS)
        if row_ok is not None:
            qh = jnp.where(row_ok, qh, 0.0)
            kh = jnp.where(row_ok, kh, 0.0)
            vh = jnp.where(row_ok, vh, 0.0)
        q.append(qh)
        k.append(kh)
        v.append(vh)
        decay.append(jnp.exp(jnp.where(tri_incl, gc[h] - gam_t[h:h + 1, :], -jnp.inf)))
    kk = [_bdot(k[h], k[h], _NT) for h in heads]
    qk = [_bdot(q[h], k[h], _NT) for h in heads]
    s_old = [s_ref[0, h] for h in heads]
    kq_s = [_bdot(jnp.concatenate([k[h], q[h]], axis=0), s_old[h]) for h in heads]
    pw = [jnp.where(tri_strict, decay[h] * kk[h], 0.0) * bc[h] for h in heads]
    nil = [-pw[h] for h in heads]
    for _ in range(int(math.log2(C)) - 1):
        pw = [_bdot(pw[h], pw[h]) for h in heads]
        nil = [nil[h] + pw[h] + _bdot(nil[h], pw[h]) for h in heads]
    gt = [jnp.exp(gc[h]) for h in heads]
    rhs = [bc[h] * (v[h] - gt[h] * kq_s[h][:C]) for h in heads]
    u = [rhs[h] + _bdot(nil[h], rhs[h]) for h in heads]
    o = [gt[h] * kq_s[h][C:] + _bdot(decay[h] * qk[h], u[h]) for h in heads]
    for h in heads:
        s_ref[0, h] = jnp.exp(glast[h]) * s_old[h] + _bdot(k[h] * jnp.exp(glast[h] - gc[h]), u[h], _TN)
    for h in heads:
        on = o[h] * lax.rsqrt(jnp.mean(o[h] * o[h], axis=-1, keepdims=True) + EPS) * onorm_ref[...]
        gate = gate_ref[:, h * GDN_DV:(h + 1) * GDN_DV]
        o_ref[:, h * GDN_DV:(h + 1) * GDN_DV] = (on * gate * _sigmoid(gate)).astype(o_ref.dtype)


def gdn_mixer(proj, ab, conv_init, s0, conv_w8, hp, o_norm, *, batch, n_chunks, valid_rows=GDN_CHUNK, hb=GDN_HB):
    C = GDN_CHUNK
    ng = GDN_HEADS // hb
    rows = batch * n_chunks * C
    qw, vw = hb * GDN_DK, hb * GDN_DV

    def rowblk(off):
        return lambda b, g, c: (b * n_chunks + c, off + g)

    def fixed3(off):
        return lambda b, g, c: (b, 0, off + g)

    def wblk(off):
        return lambda b, g, c: (0, off + g)

    in_specs = [
        pl.BlockSpec((C, qw), rowblk(0)),
        pl.BlockSpec((C, qw), rowblk(ng)),
        pl.BlockSpec((C, vw), rowblk(ng)),
        pl.BlockSpec((C, vw), rowblk(2 * ng)),
        pl.BlockSpec((C, LANES), rowblk(0)),
        pl.BlockSpec((1, CONV_PAD, qw), fixed3(0)),
        pl.BlockSpec((1, CONV_PAD, qw), fixed3(ng)),
        pl.BlockSpec((1, CONV_PAD, vw), fixed3(ng)),
        pl.BlockSpec((1, hb, GDN_DK, GDN_DV), lambda b, g, c: (b, g, 0, 0)),
        pl.BlockSpec((CONV_PAD, qw), wblk(0)),
        pl.BlockSpec((CONV_PAD, qw), wblk(ng)),
        pl.BlockSpec((CONV_PAD, vw), wblk(ng)),
        pl.BlockSpec((1, CONV_PAD, LANES), lambda b, g, c: (g, 0, 0)),
        pl.BlockSpec((1, GDN_DV), lambda b, g, c: (0, 0)),
    ]
    out_specs = [
        pl.BlockSpec((C, vw), rowblk(0)),
        pl.BlockSpec((1, hb, GDN_DK, GDN_DV), lambda b, g, c: (b, g, 0, 0)),
    ]
    return pl.pallas_call(
        functools.partial(_gdn_kernel, hb=hb, chunk=C, valid_rows=valid_rows),
        grid=(batch, ng, n_chunks),
        in_specs=in_specs,
        out_specs=out_specs,
        out_shape=[jax.ShapeDtypeStruct((rows, GDN_V), jnp.bfloat16),
                   jax.ShapeDtypeStruct((batch, GDN_HEADS, GDN_DK, GDN_DV), jnp.float32)],
        scratch_shapes=[pltpu.VMEM((CONV_PAD + C, qw), jnp.float32),
                        pltpu.VMEM((CONV_PAD + C, qw), jnp.float32),
                        pltpu.VMEM((CONV_PAD + C, vw), jnp.float32)],
        compiler_params=_params("parallel", "parallel", "arbitrary"),
        name="gdn_mixer",
    )(proj, proj, proj, proj, ab, conv_init, conv_init, conv_init, s0,
      conv_w8, conv_w8, conv_w8, hp, o_norm.reshape(1, GDN_DV))


def gdn_group_cols(w_ab, hb=GDN_HB):
    d = w_ab.shape[0]
    ng = GDN_HEADS // hb
    a = w_ab[:, :GDN_HEADS].reshape(d, ng, hb)
    b = w_ab[:, GDN_HEADS:].reshape(d, ng, hb)
    blk = jnp.concatenate([a, b, jnp.zeros((d, ng, LANES - 2 * hb), w_ab.dtype)], axis=-1)
    return blk.reshape(d, ng * LANES)


def gdn_head_params(a_log, dt_bias, hb=GDN_HB):
    ng = GDN_HEADS // hb
    rows = jnp.stack([a_log.reshape(ng, hb), dt_bias.reshape(ng, hb)], axis=1)
    return jnp.pad(rows, ((0, 0), (0, CONV_PAD - 2), (0, LANES - hb)))


KV_W = 3 * 2 * NSA_KV_HEADS * HEAD_DIM
KV_HALF = NSA_KV_HEADS * HEAD_DIM
N_KV_COLS = KV_W // HEAD_DIM
NB_PAD = LANES
_NORMED_KV_COLS = tuple(range(2 * NSA_KV_HEADS, 3 * NSA_KV_HEADS)) + tuple(range(4 * NSA_KV_HEADS, 5 * NSA_KV_HEADS))


def _kv_finish_kernel(kv_ref, gain_ref, o32_ref, o16_ref):
    for j in range(N_KV_COLS):
        sl = slice(j * HEAD_DIM, (j + 1) * HEAD_DIM)
        x = kv_ref[:, sl]
        if j in _NORMED_KV_COLS:
            x = x * lax.rsqrt(jnp.mean(x * x, axis=-1, keepdims=True) + EPS) * gain_ref[0:1, sl]
        o32_ref[:, sl] = x
        o16_ref[:, sl] = x.astype(o16_ref.dtype)


def kv_finish(kv, k_norm):
    n = kv.shape[0]
    ones = jnp.ones((KV_HALF,), jnp.float32)
    gain = jnp.concatenate([ones, ones, jnp.tile(k_norm[1], NSA_KV_HEADS), ones,
                            jnp.tile(k_norm[2], NSA_KV_HEADS), ones])
    gain = jnp.broadcast_to(gain[None], (SUBLANES, KV_W))
    return pl.pallas_call(
        _kv_finish_kernel,
        grid=(n // NORM_ROWS,),
        in_specs=[pl.BlockSpec((NORM_ROWS, KV_W), lambda i: (i, 0)),
                  pl.BlockSpec((SUBLANES, KV_W), lambda i: (0, 0))],
        out_specs=[pl.BlockSpec((NORM_ROWS, KV_W), lambda i: (i, 0)),
                   pl.BlockSpec((NORM_ROWS, KV_W), lambda i: (i, 0))],
        out_shape=[jax.ShapeDtypeStruct((n, KV_W), jnp.float32),
                   jax.ShapeDtypeStruct((n, KV_W), jnp.bfloat16)],
        compiler_params=_params("parallel"),
        name="kv_finish",
    )(kv, gain)


def _compress_prompt_kernel(rows_ref, pw_ref, wcmp_ref, kn_ref, ck_ref, cv_ref, *, nb):
    pooled_k = _hdot(pw_ref[0], rows_ref[:, 0:KV_HALF])
    pooled_v = _hdot(pw_ref[1], rows_ref[:, KV_HALF:2 * KV_HALF])
    ck_ref[...] = jnp.zeros_like(ck_ref)
    cv_ref[...] = jnp.zeros_like(cv_ref)
    for g in range(NSA_KV_HEADS):
        sl = slice(g * HEAD_DIM, (g + 1) * HEAD_DIM)
        k = _hdot(pooled_k[:, sl], wcmp_ref[0, g])
        k = k * lax.rsqrt(jnp.mean(k * k, axis=-1, keepdims=True) + EPS) * kn_ref[...]
        v = _hdot(pooled_v[:, sl], wcmp_ref[1, g])
        ck_ref[0, 0:nb, sl] = k.astype(ck_ref.dtype)
        cv_ref[0, 0:nb, sl] = v.astype(cv_ref.dtype)


def compress_prompt(kv32, cmp_pos_w, w_cmp, kn_cmp, *, batch, seq):
    nb = seq // BLOCK
    pw = jnp.einsum('nm,jc->cnmj', jnp.eye(nb, dtype=jnp.float32), cmp_pos_w).reshape(2, nb, seq)
    shape = jax.ShapeDtypeStruct((batch, NB_PAD, KV_HALF), jnp.bfloat16)
    return pl.pallas_call(
        functools.partial(_compress_prompt_kernel, nb=nb),
        grid=(batch,),
        in_specs=[pl.BlockSpec((seq, 2 * KV_HALF), lambda b: (b, 0)),
                  pl.BlockSpec((2, nb, seq), lambda b: (0, 0, 0)),
                  pl.BlockSpec((2, NSA_KV_HEADS, HEAD_DIM, HEAD_DIM), lambda b: (0, 0, 0, 0)),
                  pl.BlockSpec((1, HEAD_DIM), lambda b: (0, 0))],
        out_specs=[pl.BlockSpec((1, NB_PAD, KV_HALF), lambda b: (b, 0, 0)),
                   pl.BlockSpec((1, NB_PAD, KV_HALF), lambda b: (b, 0, 0))],
        out_shape=[shape, shape],
        compiler_params=_params("parallel"),
        name="compress_prompt",
    )(kv32, pw, w_cmp, kn_cmp.reshape(1, HEAD_DIM))


NSA_TQ = 256
NSA_PROMPT_CALLS = 4


def _select_mask(imp, qpos, nb):
    lane = _iota2(imp.shape, 1)
    cur = qpos // BLOCK
    causal = lane <= cur
    forced = (lane == 0) | (causal & (lane > cur - N_LOCAL))
    score = jnp.where(forced, FORCE, jnp.where(causal, imp, -1.0))
    score = jnp.where(lane < nb, score, -2.0)
    rank = jnp.zeros(imp.shape, jnp.float32)
    for j in range(nb):
        cj = score[:, j:j + 1]
        rank = rank + jnp.where((cj > score) | ((cj == score) & (lane > j)), 1.0, 0.0)
    return (rank < float(min(N_SELECT, nb))) & (lane < nb)


def _nsa_prompt_kernel(q_ref, gl_ref, ck_ref, cv_ref, ksel_ref, vsel_ref, kwin_ref, vwin_ref,
                       qn_ref, sl_ref, ex_ref, *rest, tq, seq, tile0, n_keys):
    o_ref = rest[-1]
    nb = seq // BLOCK
    t0 = (tile0 + pl.program_id(2)) * tq
    qpos = t0 + _iota2((tq, 1), 0)
    gates = _sigmoid(gl_ref[...])
    qs = []
    for r in range(GROUP):
        x = q_ref[:, r * HEAD_DIM:(r + 1) * HEAD_DIM]
        x = x * lax.rsqrt(jnp.mean(x * x, axis=-1, keepdims=True) + EPS) * qn_ref[...]
        qs.append((x * (SCALE * LOG2E)).astype(jnp.bfloat16))
    slopes = [sl_ref[0, r:r + 1, 0:1] * LOG2E for r in range(GROUP)]

    lane = _iota2((1, NB_PAD), 1)
    blk_end = (lane + 1) * BLOCK - 1
    valid_c = (qpos >= blk_end) & (lane < nb)
    off_c = jnp.where(lane < nb, NEG, -jnp.inf)
    any_c = (qpos >= BLOCK - 1).astype(jnp.float32)
    rel_c = (blk_end - t0).astype(jnp.float32)
    ck = ck_ref[0]
    cv = cv_ref[0]
    imp = jnp.zeros((tq, NB_PAD), jnp.float32)
    outs = []
    for r in range(GROUP):
        s = _bdot(qs[r], ck, _NT) + slopes[r] * rel_c
        e, l = _softmax_rows(jnp.where(valid_c, s, off_c), jnp.exp2)
        p = e / l * any_c
        imp = imp + p
        outs.append(gates[:, r:r + 1] * _bdot(p, cv))

    sel = _select_mask(imp, qpos, nb).astype(jnp.bfloat16)
    kpos = _iota2((1, n_keys), 1)
    allowed = (_bdot(sel, ex_ref[:, 0:n_keys]) > 0.5) & (kpos <= qpos)
    mask_s = jnp.where(allowed, 0.0, NEG)
    rel_s = (kpos - t0).astype(jnp.float32)
    ksel = ksel_ref[0:n_keys, :]
    vsel = vsel_ref[0:n_keys, :]
    for r in range(GROUP):
        s = _bdot(qs[r], ksel, _NT) + slopes[r] * rel_s + mask_s
        e, l = _softmax_rows(s, jnp.exp2)
        outs[r] = outs[r] + gates[:, GROUP + r:GROUP + r + 1] * (_bdot(e, vsel) / l)

    nw = WINDOW + tq
    start = pl.multiple_of(jnp.maximum(t0 - WINDOW, 0), LANES)
    kwin = kwin_ref[pl.ds(start, nw), :]
    vwin = vwin_ref[pl.ds(start, nw), :]
    kpos_w = start + _iota2((1, nw), 1)
    dist = qpos - kpos_w
    mask_w = jnp.where((dist >= 0) & (dist < WINDOW), 0.0, NEG)
    rel_w = (kpos_w - t0).astype(jnp.float32)
    for r in range(GROUP):
        s = _bdot(qs[r], kwin, _NT) + slopes[r] * rel_w + mask_w
        e, l = _softmax_rows(s, jnp.exp2)
        o = outs[r] + gates[:, 2 * GROUP + r:2 * GROUP + r + 1] * (_bdot(e, vwin) / l)
        o_ref[:, r * HEAD_DIM:(r + 1) * HEAD_DIM] = o.astype(o_ref.dtype)


def nsa_slopes():
    h = jnp.arange(1, NSA_HEADS + 1, dtype=jnp.float32)
    s = (2.0 ** (-8.0 * h / NSA_HEADS)).reshape(NSA_KV_HEADS, GROUP, 1)
    return jnp.broadcast_to(s, (NSA_KV_HEADS, GROUP, LANES))


def nsa_gate_cols(w_g):
    d = w_g.shape[0]
    w = w_g.reshape(d, 3, NSA_KV_HEADS, GROUP).transpose(0, 2, 1, 3).reshape(d, NSA_KV_HEADS, 3 * GROUP)
    return jnp.pad(w, ((0, 0), (0, 0), (0, LANES - 3 * GROUP))).reshape(d, NSA_KV_HEADS * LANES)


def nsa_prompt(q, gl, ck, cv, kv16, q_norm, *, batch, seq, tq=NSA_TQ):
    nt = seq // tq
    expand = (jnp.arange(NB_PAD)[:, None] == (jnp.arange(seq)[None, :] // BLOCK)).astype(jnp.bfloat16)
    kvw = 2 * NSA_KV_HEADS
    tiles_per_call = nt // NSA_PROMPT_CALLS

    def kvspec(col0):
        return pl.BlockSpec((seq, HEAD_DIM), lambda b, g, i: (b, col0 + g))

    out = None
    for part in range(NSA_PROMPT_CALLS):
        tile0 = part * tiles_per_call

        def rows(b, g, i, tile0=tile0):
            return (b * nt + tile0 + i, g)

        operands = [q, gl, ck, cv, kv16, kv16, kv16, kv16, q_norm.reshape(1, HEAD_DIM), nsa_slopes(), expand]
        in_specs = [pl.BlockSpec((tq, GROUP * HEAD_DIM), rows),
                    pl.BlockSpec((tq, LANES), rows),
                    pl.BlockSpec((1, NB_PAD, HEAD_DIM), lambda b, g, i: (b, 0, g)),
                    pl.BlockSpec((1, NB_PAD, HEAD_DIM), lambda b, g, i: (b, 0, g)),
                    kvspec(kvw), kvspec(kvw + NSA_KV_HEADS), kvspec(2 * kvw), kvspec(2 * kvw + NSA_KV_HEADS),
                    pl.BlockSpec((1, HEAD_DIM), lambda b, g, i: (0, 0)),
                    pl.BlockSpec((1, GROUP, LANES), lambda b, g, i: (g, 0, 0)),
                    pl.BlockSpec((NB_PAD, seq), lambda b, g, i: (0, 0))]
        aliases = {}
        if out is not None:
            aliases = {len(operands): 0}
            operands.append(out)
            in_specs.append(pl.BlockSpec(memory_space=pl.ANY))
        out = pl.pallas_call(
            functools.partial(_nsa_prompt_kernel, tq=tq, seq=seq, tile0=tile0, n_keys=(tile0 + tiles_per_call) * tq),
            grid=(batch, NSA_KV_HEADS, tiles_per_call),
            in_specs=in_specs,
            out_specs=pl.BlockSpec((tq, GROUP * HEAD_DIM), rows),
            out_shape=jax.ShapeDtypeStruct((batch * seq, NSA_HEADS * HEAD_DIM), jnp.bfloat16),
            input_output_aliases=aliases,
            compiler_params=_params("parallel", "parallel", "arbitrary"),
            name="nsa_prompt",
        )(*operands)
    return out


KV_SLOTS = 2 * NSA_KV_HEADS
N_PAGES = PAST_LEN // PAGE_SIZE
PAGES_PER_STEP = 8
PAGE_ROWS = PAGE_SIZE * KV_SLOTS
BLOCK_ROWS = BLOCK * KV_SLOTS
BLOCKS_PER_PAGE = PAGE_SIZE // BLOCK
N_PAST_BLOCKS = PAST_LEN // BLOCK
N_SAMPLE_BLOCKS = -(-(PAST_LEN + DEC_SEQ) // BLOCK)
SEL_LANES = -(-N_SAMPLE_BLOCKS // LANES) * LANES
QROWS = GROUP * DEC_SEQ


def _slot_rows(ref, lead, slot, n):
    return ref[lead, pl.ds(slot, n, stride=KV_SLOTS), :]


def _page_specs():
    def spec(j):
        return pl.BlockSpec((1, PAGE_ROWS, HEAD_DIM),
                            lambda b, i, pt: (pt[b * N_PAGES + PAGES_PER_STEP * i + j], 0, 0))
    return [spec(j) for j in range(PAGES_PER_STEP)]


def _paged(cache):
    return [cache.reshape(-1, PAGE_ROWS, HEAD_DIM)] * PAGES_PER_STEP


def _cmp_pool_kernel(pt_ref, *refs):
    pages, (w_ref, o_ref) = refs[:PAGES_PER_STEP], refs[PAGES_PER_STEP:]
    i = pl.program_id(1)
    tiles = []
    for p in pages:
        prod = p[0] * w_ref[...]
        for h in range(BLOCKS_PER_PAGE):
            blk = prod[h * BLOCK_ROWS:(h + 1) * BLOCK_ROWS].reshape(BLOCK, KV_SLOTS, HEAD_DIM)
            tiles.append(jnp.sum(blk, axis=0))
    n = PAGES_PER_STEP * BLOCKS_PER_PAGE * KV_SLOTS
    o_ref[0, pl.ds(pl.multiple_of(i * n, n), n), :] = jnp.concatenate(tiles, axis=0)


def cmp_pool_pages(cache, page_table, cmp_pos_w):
    w = jnp.repeat(jnp.tile(cmp_pos_w, (BLOCKS_PER_PAGE, 1)), NSA_KV_HEADS, axis=1)
    wt = jnp.broadcast_to(w.reshape(PAGE_ROWS, 1), (PAGE_ROWS, HEAD_DIM))
    return pl.pallas_call(
        _cmp_pool_kernel,
        grid_spec=pltpu.PrefetchScalarGridSpec(
            num_scalar_prefetch=1,
            grid=(DEC_BATCH, N_PAGES // PAGES_PER_STEP),
            in_specs=_page_specs() + [pl.BlockSpec((PAGE_ROWS, HEAD_DIM), lambda b, i, pt: (0, 0))],
            out_specs=pl.BlockSpec((1, N_PAST_BLOCKS * KV_SLOTS, HEAD_DIM), lambda b, i, pt: (b, 0, 0))),
        out_shape=jax.ShapeDtypeStruct((DEC_BATCH, N_PAST_BLOCKS * KV_SLOTS, HEAD_DIM), jnp.float32),
        compiler_params=_params("parallel", "arbitrary"),
        name="cmp_pool_pages",
    )(page_table.reshape(-1), *_paged(cache), wt)


def _sample_queries(q_ref, qn_ref, g):
    parts = []
    for r in range(GROUP):
        c0 = (g * GROUP + r) * HEAD_DIM
        x = q_ref[:, c0:c0 + HEAD_DIM]
        x = x * lax.rsqrt(jnp.mean(x * x, axis=-1, keepdims=True) + EPS) * qn_ref[...]
        parts.append(x * SCALE)
    return jnp.concatenate(parts, axis=0).astype(jnp.bfloat16)


def _head_major_col(x, lane0):
    return jnp.concatenate([x[:, lane0 + r:lane0 + r + 1] for r in range(GROUP)], axis=0)


def _slope_col(sl_ref, g):
    return jnp.concatenate([jnp.broadcast_to(sl_ref[g, r:r + 1, 0:1], (DEC_SEQ, 1)) for r in range(GROUP)], axis=0)


def _sample_select_kernel(pooled_ref, wcmp_ref, kn_ref, q_ref, gl_ref, qn_ref, sl_ref, wcache_ref, new_ref, ex_ref,
                          ocw_ref, mask_ref, newmask_ref):
    srow = _iota2((QROWS, 1), 0) % DEC_SEQ
    qpos = PAST_LEN + srow
    gl = gl_ref[...]
    nbp = N_PAST_BLOCKS
    lane_c = _iota2((1, nbp), 1)
    blk_end = (lane_c + 1) * BLOCK - 1
    rel_c = (blk_end - PAST_LEN).astype(jnp.float32)
    nw = wcache_ref.shape[1] // KV_SLOTS
    nwk = nw + LANES
    jw = _iota2((1, nwk), 1)
    kpos_w = jnp.where(jw < nw, PAST_LEN - nw + jw, PAST_LEN + jw - nw)
    dist_w = qpos - kpos_w
    ok_w = (dist_w >= 0) & (dist_w < WINDOW) & (jw < nw + DEC_SEQ)
    mask_w = jnp.where(ok_w, 0.0, NEG)
    rel_w = (kpos_w - PAST_LEN).astype(jnp.float32)
    pad_rows = jnp.zeros((LANES - DEC_SEQ, HEAD_DIM), jnp.float32)
    for g in range(NSA_KV_HEADS):
        ck = _hdot(_slot_rows(pooled_ref, 0, g, nbp), wcmp_ref[0, g])
        ck = ck * lax.rsqrt(jnp.mean(ck * ck, axis=-1, keepdims=True) + EPS) * kn_ref[...]
        cv = _hdot(_slot_rows(pooled_ref, 0, NSA_KV_HEADS + g, nbp), wcmp_ref[1, g])
        qg = _sample_queries(q_ref, qn_ref, g)
        slope = _slope_col(sl_ref, g)
        gates = _sigmoid(gl[:, g * LANES:(g + 1) * LANES])
        s = _bdot(qg, ck, _NT) + slope * rel_c
        e, l = _softmax_rows(jnp.where(qpos >= blk_end, s, NEG))
        p = e / l
        o = _head_major_col(gates, 0) * _bdot(p, cv)
        imp = p[0:DEC_SEQ]
        for r in range(1, GROUP):
            imp = imp + p[r * DEC_SEQ:(r + 1) * DEC_SEQ]
        imp = jnp.concatenate([imp, jnp.zeros((DEC_SEQ, SEL_LANES - nbp), jnp.float32)], axis=1)
        sel = _select_mask(imp, PAST_LEN + _iota2((DEC_SEQ, 1), 0), N_SAMPLE_BLOCKS).astype(jnp.bfloat16)
        for half in range(nbp // LANES):
            keys = _bdot(sel[:, half * LANES:(half + 1) * LANES], ex_ref[...])
            mask_ref[0, g * DEC_SEQ:(g + 1) * DEC_SEQ, half * LANES * BLOCK:(half + 1) * LANES * BLOCK] = keys
        newmask_ref[0, g * DEC_SEQ:(g + 1) * DEC_SEQ, :] = jnp.broadcast_to(
            sel[:, nbp:nbp + 1].astype(jnp.float32), (DEC_SEQ, LANES))
        c0 = 4 * KV_HALF + g * HEAD_DIM
        kw = jnp.concatenate([_slot_rows(wcache_ref, 0, g, nw), new_ref[:, c0:c0 + HEAD_DIM], pad_rows], axis=0)
        vw = jnp.concatenate([_slot_rows(wcache_ref, 0, NSA_KV_HEADS + g, nw),
                              new_ref[:, c0 + KV_HALF:c0 + KV_HALF + HEAD_DIM], pad_rows], axis=0)
        s = _bdot(qg, kw, _NT) + slope * rel_w + mask_w
        e, l = _softmax_rows(s)
        o = o + _head_major_col(gates, 2 * GROUP) * (_bdot(e, vw) / l)
        for r in range(GROUP):
            c0 = (g * GROUP + r) * HEAD_DIM
            ocw_ref[:, c0:c0 + HEAD_DIM] = o[r * DEC_SEQ:(r + 1) * DEC_SEQ]


def nsa_sample_select(pooled, w_cmp, kn_cmp, q, gl, q_norm, cache_win, kv32):
    expand = (jnp.arange(LANES)[:, None] == (jnp.arange(LANES * BLOCK)[None, :] // BLOCK)).astype(jnp.bfloat16)
    row0 = N_PROMPT // DEC_SEQ
    nw = cache_win.shape[1]
    return pl.pallas_call(
        _sample_select_kernel,
        grid=(DEC_BATCH,),
        in_specs=[pl.BlockSpec((1, N_PAST_BLOCKS * KV_SLOTS, HEAD_DIM), lambda b: (b, 0, 0)),
                  pl.BlockSpec((2, NSA_KV_HEADS, HEAD_DIM, HEAD_DIM), lambda b: (0, 0, 0, 0)),
                  pl.BlockSpec((1, HEAD_DIM), lambda b: (0, 0)),
                  pl.BlockSpec((DEC_SEQ, NSA_HEADS * HEAD_DIM), lambda b: (row0 + b, 0)),
                  pl.BlockSpec((DEC_SEQ, NSA_KV_HEADS * LANES), lambda b: (row0 + b, 0)),
                  pl.BlockSpec((1, HEAD_DIM), lambda b: (0, 0)),
                  pl.BlockSpec((NSA_KV_HEADS, GROUP, LANES), lambda b: (0, 0, 0)),
                  pl.BlockSpec((1, nw * KV_SLOTS, HEAD_DIM), lambda b: (b, 0, 0)),
                  pl.BlockSpec((DEC_SEQ, KV_W), lambda b: (row0 + b, 0)),
                  pl.BlockSpec((LANES, LANES * BLOCK), lambda b: (0, 0))],
        out_specs=[pl.BlockSpec((DEC_SEQ, NSA_HEADS * HEAD_DIM), lambda b: (b, 0)),
                   pl.BlockSpec((1, NSA_KV_HEADS * DEC_SEQ, PAST_LEN), lambda b: (b, 0, 0)),
                   pl.BlockSpec((1, NSA_KV_HEADS * DEC_SEQ, LANES), lambda b: (b, 0, 0))],
        out_shape=[jax.ShapeDtypeStruct((N_SAMPLE, NSA_HEADS * HEAD_DIM), jnp.float32),
                   jax.ShapeDtypeStruct((DEC_BATCH, NSA_KV_HEADS * DEC_SEQ, PAST_LEN), jnp.float32),
                   jax.ShapeDtypeStruct((DEC_BATCH, NSA_KV_HEADS * DEC_SEQ, LANES), jnp.float32)],
        compiler_params=_params("parallel"),
        name="nsa_sample_select",
    )(pooled, w_cmp, kn_cmp.reshape(1, HEAD_DIM), q, gl, q_norm.reshape(1, HEAD_DIM), nsa_slopes(),
      cache_win.reshape(DEC_BATCH, nw * KV_SLOTS, HEAD_DIM), kv32, expand)


def _sample_sel_kernel(pt_ref, *refs):
    pages = refs[:PAGES_PER_STEP]
    (mask_ref, newmask_ref, q_ref, gl_ref, qn_ref, sl_ref, new_ref, ocw_ref,
     o_ref, qs, m_sc, l_sc, acc_sc) = refs[PAGES_PER_STEP:]
    i = pl.program_id(1)
    groups = range(NSA_KV_HEADS)

    @pl.when(i == 0)
    def _():
        for g in groups:
            qs[g] = _sample_queries(q_ref, qn_ref, g)
        m_sc[...] = jnp.full(m_sc.shape, NEG, jnp.float32)
        l_sc[...] = jnp.zeros(l_sc.shape, jnp.float32)
        acc_sc[...] = jnp.zeros(acc_sc.shape, jnp.float32)

    def accumulate(keys, vals, rel, keep8):
        keep = [jnp.concatenate([keep8[g]] * GROUP, axis=0) for g in groups]
        s = [_bdot(qs[g], keys[g], _NT) + _slope_col(sl_ref, g) * rel + jnp.where(keep[g] > 0.5, 0.0, NEG)
             for g in groups]
        m_old = [m_sc[g] for g in groups]
        m_new = [jnp.maximum(m_old[g], jnp.max(s[g], axis=-1, keepdims=True)) for g in groups]
        p = [jnp.exp(s[g] - m_new[g]) * keep[g] for g in groups]
        pv = [_bdot(p[g], vals[g]) for g in groups]
        for g in groups:
            alpha = jnp.exp(m_old[g] - m_new[g])
            l_sc[g] = alpha * l_sc[g] + jnp.sum(p[g], axis=-1, keepdims=True)
            acc_sc[g] = alpha * acc_sc[g] + pv[g]
            m_sc[g] = m_new[g]

    lane = _iota2((1, PAGE_SIZE), 1)
    for j, page in enumerate(pages):
        keep_all = mask_ref[0, :, j * PAGE_SIZE:(j + 1) * PAGE_SIZE]

        @pl.when(jnp.max(keep_all) > 0.5)
        def _(j=j, page=page, keep_all=keep_all):
            rel = ((i * PAGES_PER_STEP + j) * PAGE_SIZE - PAST_LEN + lane).astype(jnp.float32)
            accumulate([_slot_rows(page, 0, g, PAGE_SIZE) for g in groups],
                       [_slot_rows(page, 0, NSA_KV_HEADS + g, PAGE_SIZE) for g in groups],
                       rel, [keep_all[g * DEC_SEQ:(g + 1) * DEC_SEQ] for g in groups])

    @pl.when(i == pl.num_programs(1) - 1)
    def _():
        gl = gl_ref[...]
        pad_rows = jnp.zeros((PAGE_SIZE - DEC_SEQ, HEAD_DIM), jnp.float32)
        causal = ((lane <= _iota2((DEC_SEQ, 1), 0)) & (lane < DEC_SEQ)).astype(jnp.float32)
        c0 = 2 * KV_HALF
        accumulate([jnp.concatenate([new_ref[:, c0 + g * HEAD_DIM:c0 + (g + 1) * HEAD_DIM], pad_rows], axis=0)
                    for g in groups],
                   [jnp.concatenate([new_ref[:, c0 + KV_HALF + g * HEAD_DIM:c0 + KV_HALF + (g + 1) * HEAD_DIM], pad_rows],
                                    axis=0) for g in groups],
                   lane.astype(jnp.float32),
                   [newmask_ref[0, g * DEC_SEQ:(g + 1) * DEC_SEQ, :] * causal for g in groups])
        for g in groups:
            gates = _sigmoid(gl[:, g * LANES:(g + 1) * LANES])
            o = _head_major_col(gates, GROUP) * (acc_sc[g] / l_sc[g])
            for r in range(GROUP):
                c0 = (g * GROUP + r) * HEAD_DIM
                o_ref[:, c0:c0 + HEAD_DIM] = ocw_ref[:, c0:c0 + HEAD_DIM] + o[r * DEC_SEQ:(r + 1) * DEC_SEQ]


def nsa_sample_sel(cache, page_table, mask, newmask, q, gl, q_norm, kv32, ocw):
    row0 = N_PROMPT // DEC_SEQ
    nrow = NSA_KV_HEADS * DEC_SEQ
    step_keys = PAGES_PER_STEP * PAGE_SIZE
    return pl.pallas_call(
        _sample_sel_kernel,
        grid_spec=pltpu.PrefetchScalarGridSpec(
            num_scalar_prefetch=1,
            grid=(DEC_BATCH, N_PAGES // PAGES_PER_STEP),
            in_specs=_page_specs() + [
                pl.BlockSpec((1, nrow, step_keys), lambda b, i, pt: (b, 0, i)),
                pl.BlockSpec((1, nrow, LANES), lambda b, i, pt: (b, 0, 0)),
                pl.BlockSpec((DEC_SEQ, NSA_HEADS * HEAD_DIM), lambda b, i, pt: (row0 + b, 0)),
                pl.BlockSpec((DEC_SEQ, NSA_KV_HEADS * LANES), lambda b, i, pt: (row0 + b, 0)),
                pl.BlockSpec((1, HEAD_DIM), lambda b, i, pt: (0, 0)),
                pl.BlockSpec((NSA_KV_HEADS, GROUP, LANES), lambda b, i, pt: (0, 0, 0)),
                pl.BlockSpec((DEC_SEQ, KV_W), lambda b, i, pt: (row0 + b, 0)),
                pl.BlockSpec((DEC_SEQ, NSA_HEADS * HEAD_DIM), lambda b, i, pt: (b, 0))],
            out_specs=pl.BlockSpec((DEC_SEQ, NSA_HEADS * HEAD_DIM), lambda b, i, pt: (b, 0)),
            scratch_shapes=[pltpu.VMEM((NSA_KV_HEADS, QROWS, HEAD_DIM), jnp.bfloat16),
                            pltpu.VMEM((NSA_KV_HEADS, QROWS, 1), jnp.float32),
                            pltpu.VMEM((NSA_KV_HEADS, QROWS, 1), jnp.float32),
                            pltpu.VMEM((NSA_KV_HEADS, QROWS, HEAD_DIM), jnp.float32)]),
        out_shape=jax.ShapeDtypeStruct((N_SAMPLE, NSA_HEADS * HEAD_DIM), jnp.float32),
        compiler_params=_params("parallel", "arbitrary"),
        name="nsa_sample_sel",
    )(page_table.reshape(-1), *_paged(cache), mask, newmask,
      q, gl, q_norm.reshape(1, HEAD_DIM), nsa_slopes(), kv32, ocw)


def nsa_sample(q, gl, kv32, cache_cmp_kv, cache_sel_kv, cache_win_kv, page_table, cmp_pos_w, w_cmp, kn_cmp, q_norm):
    pooled = cmp_pool_pages(cache_cmp_kv, page_table, cmp_pos_w)
    ocw, mask, newmask = nsa_sample_select(pooled, w_cmp, kn_cmp, q, gl, q_norm, cache_win_kv, kv32)
    return nsa_sample_sel(cache_sel_kv, page_table, mask, newmask, q, gl, q_norm, kv32, ocw)


def _split_rows(x):
    return (x[:N_PROMPT].reshape((BATCH, SEQ) + x.shape[1:]),
            x[N_PROMPT:].reshape((DEC_BATCH, DEC_SEQ) + x.shape[1:]))


def kernel(x_prompt, x_sample, state_conv, state_delta, cache_cmp_kv, cache_sel_kv, cache_win_kv, page_table,
           ffn_norm, ffn_w_gate, ffn_w_up, ffn_w_down, mix_norm,
           gdn_w_in, gdn_conv_w, gdn_a_log, gdn_dt_bias, gdn_o_norm, gdn_w_out,
           kv_norm, w_kv, cmp_pos_w, w_cmp, k_norm, nsa_w_q, nsa_q_norm, nsa_w_o):
    bf = jnp.bfloat16
    n_main = GDN_CONV_CH + GDN_V
    n_q = NSA_HEADS * HEAD_DIM
    hist = CONV_W - 1

    def ffn_half(h, hg16, ssq, layer, i, next_gains):
        act, wd16 = swiglu_up(hg16, ssq, ffn_w_gate, ffn_w_up, ffn_w_down, (layer, i))
        return matmul_residual(act, wd16, (), h, 0.5, next_gains, tn=256, tm=DOWN_ROWS)

    h, hg16, ssq = rows_prep(x_prompt.reshape(N_PROMPT, D_MODEL), x_sample.reshape(N_SAMPLE, D_MODEL), ffn_norm[0, 0])
    h, (hg16,), ssq = ffn_half(h, hg16, ssq, 0, 0, [mix_norm[0]])
    proj = matmul_normed(hg16, ssq, jnp.swapaxes(gdn_w_in[0], 0, 1), n_cols=n_main, tn=512, w_is_transposed=True)
    ab = matmul_normed(hg16, ssq, gdn_group_cols(gdn_w_in[0, :, n_main:]), tn=LANES)
    conv_w8 = jnp.pad(gdn_conv_w[0], ((0, CONV_PAD - CONV_W), (0, 0)))
    hp = gdn_head_params(gdn_a_log[0], gdn_dt_bias[0])

    def pad_sample(x):
        x = x[N_PROMPT:].reshape(DEC_BATCH, DEC_SEQ, x.shape[1])
        return jnp.pad(x, ((0, 0), (0, GDN_CHUNK - DEC_SEQ), (0, 0))).reshape(DEC_BATCH * GDN_CHUNK, x.shape[2])

    o_p, delta_p = gdn_mixer(proj, ab, jnp.zeros((BATCH, CONV_PAD, GDN_CONV_CH), jnp.float32),
                             jnp.zeros((BATCH, GDN_HEADS, GDN_DK, GDN_DV), jnp.float32),
                             conv_w8, hp, gdn_o_norm[0], batch=BATCH, n_chunks=SEQ // GDN_CHUNK)
    o_s, delta_s = gdn_mixer(pad_sample(proj), pad_sample(ab),
                             jnp.pad(state_conv[0], ((0, 0), (CONV_PAD - hist, 0), (0, 0))), state_delta[0],
                             conv_w8, hp, gdn_o_norm[0], batch=DEC_BATCH, n_chunks=1, valid_rows=DEC_SEQ)
    o_s = o_s.reshape(DEC_BATCH, GDN_CHUNK, GDN_V)[:, :DEC_SEQ].reshape(N_SAMPLE, GDN_V)
    conv_p = jnp.stack([lax.slice(proj, ((b + 1) * SEQ - hist, 0), ((b + 1) * SEQ, GDN_CONV_CH)) for b in range(BATCH)])
    u_s = lax.slice(proj, (N_PROMPT, 0), (N_ROWS, GDN_CONV_CH)).reshape(DEC_BATCH, DEC_SEQ, GDN_CONV_CH)
    conv_s = jnp.concatenate([state_conv[0], u_s], axis=1)[:, DEC_SEQ:]
    h, (hg16,), ssq = matmul_residual(jnp.concatenate([o_p, o_s], axis=0), gdn_w_out, (0,), h, 1.0, [ffn_norm[0, 1]],
                                      tn=256, tm=MM_ROWS)
    h, (hg16_kv, hg16), ssq = ffn_half(h, hg16, ssq, 0, 1, [kv_norm, ffn_norm[1, 0]])

    kv32, kv16 = kv_finish(matmul_normed(hg16_kv, ssq, w_kv, tn=512), k_norm)
    kv5 = kv32.reshape(N_ROWS, 3, 2, NSA_KV_HEADS, HEAD_DIM)
    cmp_p, cmp_s = _split_rows(kv5[:, 0])
    sel_p, sel_s = _split_rows(kv5[:, 1])
    win_rows_p, win_rows_s = _split_rows(kv5[:, 2])

    h, (hg16,), ssq = ffn_half(h, hg16, ssq, 1, 0, [mix_norm[1]])
    q = matmul_normed(hg16, ssq, jnp.swapaxes(nsa_w_q[0], 0, 1), n_cols=n_q, tn=512, w_is_transposed=True)
    gl = matmul_normed(hg16, ssq, nsa_gate_cols(nsa_w_q[0, :, n_q:]), tn=LANES)
    ck, cv = compress_prompt(kv32, cmp_pos_w, w_cmp, k_norm[0], batch=BATCH, seq=SEQ)
    o_p = nsa_prompt(q, gl, ck, cv, kv16, nsa_q_norm[0], batch=BATCH, seq=SEQ)
    o_s = nsa_sample(q, gl, kv32, cache_cmp_kv, cache_sel_kv, cache_win_kv, page_table, cmp_pos_w, w_cmp, k_norm[0],
                     nsa_q_norm[0])
    h, (hg16,), ssq = matmul_residual(jnp.concatenate([o_p, o_s.astype(bf)], axis=0), nsa_w_o, (0,), h, 1.0,
                                      [ffn_norm[1, 1]], tn=256, tm=MM_ROWS)
    h, _, _ = ffn_half(h, hg16, ssq, 1, 1, [])

    y_p, y_s = _split_rows(h)
    win_p = win_rows_p[:, -min(WINDOW, SEQ):]
    win_s = jnp.concatenate([cache_win_kv, win_rows_s], axis=1)[:, DEC_SEQ:]
    return (y_p, y_s, conv_p[None], conv_s[None], delta_p[None], delta_s[None],
            cmp_p, cmp_s, sel_p, sel_s, win_p, win_s)
```

```python
import functools
import math

import jax
import jax.numpy as jnp
from jax import lax
from jax.experimental import pallas as pl
from jax.experimental.pallas import tpu as pltpu

D_MODEL = 4096
BATCH = 4
SEQ = 2048
DEPTH = 2
DEC_BATCH = 8
DEC_SEQ = 8
PAST_LEN = 16384
PAGE_SIZE = 128
D_FF = 11008
EPS = 1e-6
GDN_HEADS = 16
GDN_DK = 128
GDN_DV = 256
CONV_W = 4
GDN_CHUNK = 64
GDN_QK = GDN_HEADS * GDN_DK
GDN_V = GDN_HEADS * GDN_DV
GDN_CONV_CH = 2 * GDN_QK + GDN_V
NSA_HEADS = 32
NSA_KV_HEADS = 4
HEAD_DIM = 128
GROUP = NSA_HEADS // NSA_KV_HEADS
BLOCK = 64
N_SELECT = 16
N_LOCAL = 2
WINDOW = 512
SCALE = HEAD_DIM ** -0.5
NEG = -1e30
LOG2E = 1.0 / math.log(2.0)
FORCE = 1e4

N_PROMPT = BATCH * SEQ
N_SAMPLE = DEC_BATCH * DEC_SEQ
N_ROWS = N_PROMPT + N_SAMPLE

VMEM_LIMIT_BYTES = 56 * 1024 * 1024
LANES = 128
SUBLANES = 8

NORM_ROWS = 192
MM_ROWS = 1376
DOWN_ROWS = 688


def _params(*sem):
    return pltpu.CompilerParams(dimension_semantics=sem, vmem_limit_bytes=VMEM_LIMIT_BYTES)


PREP_ROWS = 64


def _lane_fold(x):
    acc = x[:, 0:LANES]
    for c in range(1, x.shape[1] // LANES):
        acc = acc + x[:, c * LANES:(c + 1) * LANES]
    return acc


def _rows_prep_kernel(xp_ref, xs_ref, g_ref, h_ref, h16_ref, ssq_ref):
    n_prompt_tiles = N_PROMPT // PREP_ROWS

    def emit(x):
        h_ref[...] = x
        h16_ref[...] = (x * g_ref[...]).astype(h16_ref.dtype)
        ssq_ref[...] = _lane_fold(x * x)

    @pl.when(pl.program_id(0) < n_prompt_tiles)
    def _():
        emit(xp_ref[...])

    @pl.when(pl.program_id(0) >= n_prompt_tiles)
    def _():
        emit(xs_ref[...])


def rows_prep(x_prompt, x_sample, gain):
    d = x_prompt.shape[1]
    n_p = N_PROMPT // PREP_ROWS
    n = N_ROWS // PREP_ROWS
    return pl.pallas_call(
        _rows_prep_kernel,
        grid=(n,),
        in_specs=[pl.BlockSpec((PREP_ROWS, d), lambda i: (jnp.minimum(i, n_p - 1), 0)),
                  pl.BlockSpec((PREP_ROWS, d), lambda i: (jnp.maximum(i - n_p, 0), 0)),
                  pl.BlockSpec((1, d), lambda i: (0, 0))],
        out_specs=[pl.BlockSpec((PREP_ROWS, d), lambda i: (i, 0)),
                   pl.BlockSpec((PREP_ROWS, d), lambda i: (i, 0)),
                   pl.BlockSpec((PREP_ROWS, LANES), lambda i: (i, 0))],
        out_shape=[jax.ShapeDtypeStruct((N_ROWS, d), jnp.float32),
                   jax.ShapeDtypeStruct((N_ROWS, d), jnp.bfloat16),
                   jax.ShapeDtypeStruct((N_ROWS, LANES), jnp.float32)],
        compiler_params=_params("parallel"),
        name="rows_prep",
    )(x_prompt, x_sample, gain.reshape(1, d))


def _row_scale(ssq_ref, k):
    return lax.rsqrt(jnp.sum(ssq_ref[...], axis=-1, keepdims=True) / k + EPS)


def _weight_spec(w, lead, tn):
    k = w.shape[-2]
    return pl.BlockSpec((None,) * len(lead) + (k, tn), lambda i, j: tuple(lead) + (0, j))


def _normed_specs(tm, k):
    return [pl.BlockSpec((tm, k), lambda i, j: (i, 0)),
            pl.BlockSpec((tm, LANES), lambda i, j: (i, 0))]


def _mm_kernel(x_ref, ssq_ref, w_ref, o_ref, *, w_is_transposed):
    w = w_ref[...].astype(jnp.bfloat16)
    if w_is_transposed:
        acc = lax.dot_general(x_ref[...], w, (((1,), (1,)), ((), ())), preferred_element_type=jnp.float32)
    else:
        acc = jnp.dot(x_ref[...], w, preferred_element_type=jnp.float32)
    o_ref[...] = (_row_scale(ssq_ref, x_ref.shape[1]) * acc).astype(o_ref.dtype)


def matmul_normed(xg16, ssq, w, lead=(), *, tn, n_cols=None, tm=MM_ROWS, w_is_transposed=False):
    m, k = xg16.shape
    n = n_cols or (w.shape[0] if w_is_transposed else w.shape[-1])
    w_spec = pl.BlockSpec((tn, k), lambda i, j: (j, 0)) if w_is_transposed else _weight_spec(w, lead, tn)
    return pl.pallas_call(
        functools.partial(_mm_kernel, w_is_transposed=w_is_transposed),
        grid=(m // tm, n // tn),
        in_specs=_normed_specs(tm, k) + [w_spec],
        out_specs=pl.BlockSpec((tm, tn), lambda i, j: (i, j)),
        out_shape=jax.ShapeDtypeStruct((m, n), jnp.float32),
        compiler_params=_params("parallel", "arbitrary"),
        name="matmul_normed",
    )(xg16, ssq, w)


def _mm_res_kernel(x_ref, w_ref, r_ref, *refs, scale, n_gains):
    gain_refs, o_ref, o16_refs = refs[:n_gains], refs[n_gains], refs[n_gains + 1:2 * n_gains + 1]
    w = w_ref[...].astype(jnp.bfloat16)
    h = r_ref[...] + scale * jnp.dot(x_ref[...], w, preferred_element_type=jnp.float32)
    o_ref[...] = h
    for g_ref, o16_ref in zip(gain_refs, o16_refs):
        o16_ref[...] = (h * g_ref[...]).astype(o16_ref.dtype)
    if n_gains:
        ssq_ref = refs[2 * n_gains + 1]
        part = _lane_fold(h * h)

        @pl.when(pl.program_id(1) == 0)
        def _():
            ssq_ref[...] = part

        @pl.when(pl.program_id(1) > 0)
        def _():
            ssq_ref[...] += part


def matmul_residual(x, w, lead, res, scale, gains, *, tn, tm):
    m, k = x.shape
    n = w.shape[-1]
    ng = len(gains)
    tile = pl.BlockSpec((tm, tn), lambda i, j: (i, j))
    out = pl.pallas_call(
        functools.partial(_mm_res_kernel, scale=scale, n_gains=ng),
        grid=(m // tm, n // tn),
        in_specs=[pl.BlockSpec((tm, k), lambda i, j: (i, 0)), _weight_spec(w, lead, tn), tile]
                 + [pl.BlockSpec((1, tn), lambda i, j: (0, j))] * ng,
        out_specs=[tile] * (1 + ng) + [pl.BlockSpec((tm, LANES), lambda i, j: (i, 0))] * min(ng, 1),
        out_shape=[jax.ShapeDtypeStruct((m, n), jnp.float32)] + [jax.ShapeDtypeStruct((m, n), jnp.bfloat16)] * ng
                  + [jax.ShapeDtypeStruct((m, LANES), jnp.float32)] * min(ng, 1),
        compiler_params=_params("parallel", "arbitrary"),
        name="matmul_residual",
    )(x, w, res, *[g.reshape(1, n) for g in gains])
    return out[0], list(out[1:1 + ng]), (out[1 + ng] if ng else None)


WD_SLAB_PASSES = 2


def _swiglu_up_kernel(x_ref, ssq_ref, wg_ref, wu_ref, wd_ref, o_ref, wd16_ref):
    x = x_ref[...]
    r = _row_scale(ssq_ref, x.shape[1])
    g = r * jnp.dot(x, wg_ref[...].astype(jnp.bfloat16), preferred_element_type=jnp.float32)
    u = r * jnp.dot(x, wu_ref[...].astype(jnp.bfloat16), preferred_element_type=jnp.float32)
    o_ref[...] = (g * jax.nn.sigmoid(g) * u).astype(o_ref.dtype)
    wd16_ref[...] = wd_ref[...].astype(wd16_ref.dtype)


def swiglu_up(xg16, ssq, wg, wu, wd, lead, tn=256, tm=MM_ROWS):
    m, k = xg16.shape
    n = wg.shape[-1]
    nj = n // tn
    n_slabs = WD_SLAB_PASSES * nj
    slab = wd.shape[-2] // n_slabs
    d_out = wd.shape[-1]

    def slab_index(i, j):
        return jnp.minimum(i * nj + j, n_slabs - 1)

    return pl.pallas_call(
        _swiglu_up_kernel,
        grid=(m // tm, nj),
        in_specs=_normed_specs(tm, k) + [
            _weight_spec(wg, lead, tn), _weight_spec(wu, lead, tn),
            pl.BlockSpec((None,) * len(lead) + (slab, d_out), lambda i, j: tuple(lead) + (slab_index(i, j), 0))],
        out_specs=[pl.BlockSpec((tm, tn), lambda i, j: (i, j)),
                   pl.BlockSpec((slab, d_out), lambda i, j: (slab_index(i, j), 0))],
        out_shape=[jax.ShapeDtypeStruct((m, n), jnp.bfloat16),
                   jax.ShapeDtypeStruct((wd.shape[-2], d_out), jnp.bfloat16)],
        compiler_params=_params("parallel", "arbitrary"),
        name="swiglu_up",
    )(xg16, ssq, wg, wu, wd)


_HI = lax.Precision.HIGHEST
_NT = (((1,), (1,)), ((), ()))
_TN = (((0,), (0,)), ((), ()))


def _bdot(a, b, dims=None):
    a = a.astype(jnp.bfloat16)
    b = b.astype(jnp.bfloat16)
    if dims is None:
        return jnp.dot(a, b, preferred_element_type=jnp.float32)
    return lax.dot_general(a, b, dims, preferred_element_type=jnp.float32)


def _hdot(a, b, dims=None):
    if dims is None:
        return jnp.dot(a, b, precision=_HI, preferred_element_type=jnp.float32)
    return lax.dot_general(a, b, dims, precision=_HI, preferred_element_type=jnp.float32)


def _iota2(shape, axis):
    return lax.broadcasted_iota(jnp.int32, shape, axis)


def _sigmoid(x):
    return 1.0 / (1.0 + jnp.exp(-x))


def _softmax_rows(s, exp=jnp.exp):
    m = jnp.max(s, axis=-1, keepdims=True)
    e = exp(s - m)
    return e, jnp.sum(e, axis=-1, keepdims=True)


GDN_HB = 16
CONV_PAD = SUBLANES


def _gdn_kernel(uq_ref, uk_ref, uv_ref, gate_ref, ab_ref, cq_ref, ck_ref, cv_ref, s0_ref,
                wq_ref, wk_ref, wv_ref, hp_ref, onorm_ref,
                o_ref, s_ref, extq, extk, extv, *, hb, chunk, valid_rows):
    C = chunk
    c = pl.program_id(2)

    @pl.when(c == 0)
    def _():
        s_ref[...] = s0_ref[...]
        extq[0:CONV_PAD, :] = cq_ref[0]
        extk[0:CONV_PAD, :] = ck_ref[0]
        extv[0:CONV_PAD, :] = cv_ref[0]

    def conv(ext, u_ref, w_ref):
        ext[CONV_PAD:CONV_PAD + C, :] = u_ref[...]
        base = CONV_PAD - (CONV_W - 1)
        acc = ext[base:base + C, :] * w_ref[0:1, :]
        for i in range(1, CONV_W):
            acc = acc + ext[base + i:base + i + C, :] * w_ref[i:i + 1, :]
        ext[0:CONV_PAD, :] = ext[C:C + CONV_PAD, :]
        return acc * _sigmoid(acc)

    qc = conv(extq, uq_ref, wq_ref)
    kc = conv(extk, uk_ref, wk_ref)
    vc = conv(extv, uv_ref, wv_ref)

    row = _iota2((C, C), 0)
    col = _iota2((C, C), 1)
    tri_incl = row >= col
    tri_strict = row > col
    row_ok = None
    if valid_rows < C:
        row_ok = _iota2((C, 1), 0) < valid_rows

    ab = ab_ref[...]
    x = ab + hp_ref[0, 1:2, :]
    softplus = jnp.maximum(x, 0.0) + jnp.log(1.0 + jnp.exp(-jnp.abs(x)))
    gmat = -jnp.exp(hp_ref[0, 0:1, :]) * softplus
    if row_ok is not None:
        gmat = jnp.where(row_ok, gmat, 0.0)
    beta = _sigmoid(ab)
    gam = _hdot(tri_incl.astype(jnp.float32), gmat)
    eye_l = (_iota2((LANES, LANES), 0) == _iota2((LANES, LANES), 1)).astype(jnp.float32)
    gam_t = _hdot(eye_l, gam, _NT)

    heads = range(hb)
    gc = [gam[:, h:h + 1] for h in heads]
    bc = [beta[:, hb + h:hb + h + 1] for h in heads]
    glast = [gam[C - 1:C, h:h + 1] for h in heads]
    q, k, v, decay = [], [], [], []
    for h in heads:
        qh = qc[:, h * GDN_DK:(h + 1) * GDN_DK]
        kh = kc[:, h * GDN_DK:(h + 1) * GDN_DK]
        vh = vc[:, h * GDN_DV:(h + 1) * GDN_DV]
        qh = qh * lax.rsqrt(jnp.sum(qh * qh, axis=-1, keepdims=True) + EPS) * GDN_DK ** -0.5
        kh = kh * lax.rsqrt(jnp.sum(kh * kh, axis=-1, keepdims=True) + EPS)
        if row_ok is not None:
            qh = jnp.where(row_ok, qh, 0.0)
            kh = jnp.where(row_ok, kh, 0.0)
            vh = jnp.where(row_ok, vh, 0.0)
        q.append(qh)
        k.append(kh)
        v.append(vh)
        decay.append(jnp.exp(jnp.where(tri_incl, gc[h] - gam_t[h:h + 1, :], -jnp.inf)))
    kk = [_bdot(k[h], k[h], _NT) for h in heads]
    qk = [_bdot(q[h], k[h], _NT) for h in heads]
    s_old = [s_ref[0, h] for h in heads]
    kq_s = [_bdot(jnp.concatenate([k[h], q[h]], axis=0), s_old[h]) for h in heads]
    pw = [jnp.where(tri_strict, decay[h] * kk[h], 0.0) * bc[h] for h in heads]
    nil = [-pw[h] for h in heads]
    for _ in range(int(math.log2(C)) - 1):
        pw = [_bdot(pw[h], pw[h]) for h in heads]
        nil = [nil[h] + pw[h] + _bdot(nil[h], pw[h]) for h in heads]
    gt = [jnp.exp(gc[h]) for h in heads]
    rhs = [bc[h] * (v[h] - gt[h] * kq_s[h][:C]) for h in heads]
    u = [rhs[h] + _bdot(nil[h], rhs[h]) for h in heads]
    o = [gt[h] * kq_s[h][C:] + _bdot(decay[h] * qk[h], u[h]) for h in heads]
    for h in heads:
        s_ref[0, h] = jnp.exp(glast[h]) * s_old[h] + _bdot(k[h] * jnp.exp(glast[h] - gc[h]), u[h], _TN)
    for h in heads:
        on = o[h] * lax.rsqrt(jnp.mean(o[h] * o[h], axis=-1, keepdims=True) + EPS) * onorm_ref[...]
        gate = gate_ref[:, h * GDN_DV:(h + 1) * GDN_DV]
        o_ref[:, h * GDN_DV:(h + 1) * GDN_DV] = (on * gate * _sigmoid(gate)).astype(o_ref.dtype)


def gdn_mixer(proj, ab, conv_init, s0, conv_w8, hp, o_norm, *, batch, n_chunks, valid_rows=GDN_CHUNK, hb=GDN_HB):
    C = GDN_CHUNK
    ng = GDN_HEADS // hb
    rows = batch * n_chunks * C
    qw, vw = hb * GDN_DK, hb * GDN_DV

    def rowblk(off):
        return lambda b, g, c: (b * n_chunks + c, off + g)

    def fixed3(off):
        return lambda b, g, c: (b, 0, off + g)

    def wblk(off):
        return lambda b, g, c: (0, off + g)

    in_specs = [
        pl.BlockSpec((C, qw), rowblk(0)),
        pl.BlockSpec((C, qw), rowblk(ng)),
        pl.BlockSpec((C, vw), rowblk(ng)),
        pl.BlockSpec((C, vw), rowblk(2 * ng)),
        pl.BlockSpec((C, LANES), rowblk(0)),
        pl.BlockSpec((1, CONV_PAD, qw), fixed3(0)),
        pl.BlockSpec((1, CONV_PAD, qw), fixed3(ng)),
        pl.BlockSpec((1, CONV_PAD, vw), fixed3(ng)),
        pl.BlockSpec((1, hb, GDN_DK, GDN_DV), lambda b, g, c: (b, g, 0, 0)),
        pl.BlockSpec((CONV_PAD, qw), wblk(0)),
        pl.BlockSpec((CONV_PAD, qw), wblk(ng)),
        pl.BlockSpec((CONV_PAD, vw), wblk(ng)),
        pl.BlockSpec((1, CONV_PAD, LANES), lambda b, g, c: (g, 0, 0)),
        pl.BlockSpec((1, GDN_DV), lambda b, g, c: (0, 0)),
    ]
    out_specs = [
        pl.BlockSpec((C, vw), rowblk(0)),
        pl.BlockSpec((1, hb, GDN_DK, GDN_DV), lambda b, g, c: (b, g, 0, 0)),
    ]
    return pl.pallas_call(
        functools.partial(_gdn_kernel, hb=hb, chunk=C, valid_rows=valid_rows),
        grid=(batch, ng, n_chunks),
        in_specs=in_specs,
        out_specs=out_specs,
        out_shape=[jax.ShapeDtypeStruct((rows, GDN_V), jnp.bfloat16),
                   jax.ShapeDtypeStruct((batch, GDN_HEADS, GDN_DK, GDN_DV), jnp.float32)],
        scratch_shapes=[pltpu.VMEM((CONV_PAD + C, qw), jnp.float32),
                        pltpu.VMEM((CONV_PAD + C, qw), jnp.float32),
                        pltpu.VMEM((CONV_PAD + C, vw), jnp.float32)],
        compiler_params=_params("parallel", "parallel", "arbitrary"),
        name="gdn_mixer",
    )(proj, proj, proj, proj, ab, conv_init, conv_init, conv_init, s0,
      conv_w8, conv_w8, conv_w8, hp, o_norm.reshape(1, GDN_DV))


def gdn_group_cols(w_ab, hb=GDN_HB):
    d = w_ab.shape[0]
    ng = GDN_HEADS // hb
    a = w_ab[:, :GDN_HEADS].reshape(d, ng, hb)
    b = w_ab[:, GDN_HEADS:].reshape(d, ng, hb)
    blk = jnp.concatenate([a, b, jnp.zeros((d, ng, LANES - 2 * hb), w_ab.dtype)], axis=-1)
    return blk.reshape(d, ng * LANES)


def gdn_head_params(a_log, dt_bias, hb=GDN_HB):
    ng = GDN_HEADS // hb
    rows = jnp.stack([a_log.reshape(ng, hb), dt_bias.reshape(ng, hb)], axis=1)
    return jnp.pad(rows, ((0, 0), (0, CONV_PAD - 2), (0, LANES - hb)))


KV_W = 3 * 2 * NSA_KV_HEADS * HEAD_DIM
KV_HALF = NSA_KV_HEADS * HEAD_DIM
N_KV_COLS = KV_W // HEAD_DIM
NB_PAD = LANES
_NORMED_KV_COLS = tuple(range(2 * NSA_KV_HEADS, 3 * NSA_KV_HEADS)) + tuple(range(4 * NSA_KV_HEADS, 5 * NSA_KV_HEADS))


def _kv_finish_kernel(kv_ref, gain_ref, o32_ref, o16_ref):
    for j in range(N_KV_COLS):
        sl = slice(j * HEAD_DIM, (j + 1) * HEAD_DIM)
        x = kv_ref[:, sl]
        if j in _NORMED_KV_COLS:
            x = x * lax.rsqrt(jnp.mean(x * x, axis=-1, keepdims=True) + EPS) * gain_ref[0:1, sl]
        o32_ref[:, sl] = x
        o16_ref[:, sl] = x.astype(o16_ref.dtype)


def kv_finish(kv, k_norm):
    n = kv.shape[0]
    ones = jnp.ones((KV_HALF,), jnp.float32)
    gain = jnp.concatenate([ones, ones, jnp.tile(k_norm[1], NSA_KV_HEADS), ones,
                            jnp.tile(k_norm[2], NSA_KV_HEADS), ones])
    gain = jnp.broadcast_to(gain[None], (SUBLANES, KV_W))
    return pl.pallas_call(
        _kv_finish_kernel,
        grid=(n // NORM_ROWS,),
        in_specs=[pl.BlockSpec((NORM_ROWS, KV_W), lambda i: (i, 0)),
                  pl.BlockSpec((SUBLANES, KV_W), lambda i: (0, 0))],
        out_specs=[pl.BlockSpec((NORM_ROWS, KV_W), lambda i: (i, 0)),
                   pl.BlockSpec((NORM_ROWS, KV_W), lambda i: (i, 0))],
        out_shape=[jax.ShapeDtypeStruct((n, KV_W), jnp.float32),
                   jax.ShapeDtypeStruct((n, KV_W), jnp.bfloat16)],
        compiler_params=_params("parallel"),
        name="kv_finish",
    )(kv, gain)


def _compress_prompt_kernel(rows_ref, pw_ref, wcmp_ref, kn_ref, ck_ref, cv_ref, *, nb):
    pooled_k = _hdot(pw_ref[0], rows_ref[:, 0:KV_HALF])
    pooled_v = _hdot(pw_ref[1], rows_ref[:, KV_HALF:2 * KV_HALF])
    ck_ref[...] = jnp.zeros_like(ck_ref)
    cv_ref[...] = jnp.zeros_like(cv_ref)
    for g in range(NSA_KV_HEADS):
        sl = slice(g * HEAD_DIM, (g + 1) * HEAD_DIM)
        k = _hdot(pooled_k[:, sl], wcmp_ref[0, g])
        k = k * lax.rsqrt(jnp.mean(k * k, axis=-1, keepdims=True) + EPS) * kn_ref[...]
        v = _hdot(pooled_v[:, sl], wcmp_ref[1, g])
        ck_ref[0, 0:nb, sl] = k.astype(ck_ref.dtype)
        cv_ref[0, 0:nb, sl] = v.astype(cv_ref.dtype)


def compress_prompt(kv32, cmp_pos_w, w_cmp, kn_cmp, *, batch, seq):
    nb = seq // BLOCK
    pw = jnp.einsum('nm,jc->cnmj', jnp.eye(nb, dtype=jnp.float32), cmp_pos_w).reshape(2, nb, seq)
    shape = jax.ShapeDtypeStruct((batch, NB_PAD, KV_HALF), jnp.bfloat16)
    return pl.pallas_call(
        functools.partial(_compress_prompt_kernel, nb=nb),
        grid=(batch,),
        in_specs=[pl.BlockSpec((seq, 2 * KV_HALF), lambda b: (b, 0)),
                  pl.BlockSpec((2, nb, seq), lambda b: (0, 0, 0)),
                  pl.BlockSpec((2, NSA_KV_HEADS, HEAD_DIM, HEAD_DIM), lambda b: (0, 0, 0, 0)),
                  pl.BlockSpec((1, HEAD_DIM), lambda b: (0, 0))],
        out_specs=[pl.BlockSpec((1, NB_PAD, KV_HALF), lambda b: (b, 0, 0)),
                   pl.BlockSpec((1, NB_PAD, KV_HALF), lambda b: (b, 0, 0))],
        out_shape=[shape, shape],
        compiler_params=_params("parallel"),
        name="compress_prompt",
    )(kv32, pw, w_cmp, kn_cmp.reshape(1, HEAD_DIM))


NSA_TQ = 256
NSA_PROMPT_CALLS = 8


def _select_mask(imp, qpos, nb, n_causal=None):
    lane = _iota2(imp.shape, 1)
    cur = qpos // BLOCK
    causal = lane <= cur
    forced = (lane == 0) | (causal & (lane > cur - N_LOCAL))
    score = jnp.where(forced, FORCE, jnp.where(causal, imp, -1.0))
    score = jnp.where(lane < nb, score, -2.0)
    rank = jnp.zeros(imp.shape, jnp.float32)
    for j in range(nb if n_causal is None else min(nb, n_causal)):
        cj = score[:, j:j + 1]
        rank = rank + jnp.where((cj > score) | ((cj == score) & (lane > j)), 1.0, 0.0)
    return (rank < float(min(N_SELECT, nb))) & (lane < nb)


def _nsa_prompt_kernel(q_ref, gl_ref, ck_ref, cv_ref, ksel_ref, vsel_ref, kwin_ref, vwin_ref,
                       qn_ref, sl_ref, ex_ref, *rest, tq, seq, tile0, n_keys):
    o_ref = rest[-1]
    nb = seq // BLOCK
    t0 = (tile0 + pl.program_id(2)) * tq
    qpos = t0 + _iota2((tq, 1), 0)
    gates = _sigmoid(gl_ref[...])
    qs = []
    for r in range(GROUP):
        x = q_ref[:, r * HEAD_DIM:(r + 1) * HEAD_DIM]
        x = x * lax.rsqrt(jnp.mean(x * x, axis=-1, keepdims=True) + EPS) * qn_ref[...]
        qs.append((x * (SCALE * LOG2E)).astype(jnp.bfloat16))
    slopes = [sl_ref[0, r:r + 1, 0:1] * LOG2E for r in range(GROUP)]

    lane = _iota2((1, NB_PAD), 1)
    blk_end = (lane + 1) * BLOCK - 1
    valid_c = (qpos >= blk_end) & (lane < nb)
    off_c = jnp.where(lane < nb, NEG, -jnp.inf)
    any_c = (qpos >= BLOCK - 1).astype(jnp.float32)
    rel_c = (blk_end - t0).astype(jnp.float32)
    ck = ck_ref[0]
    cv = cv_ref[0]
    imp = jnp.zeros((tq, NB_PAD), jnp.float32)
    outs = []
    for r in range(GROUP):
        s = _bdot(qs[r], ck, _NT) + slopes[r] * rel_c
        e, l = _softmax_rows(jnp.where(valid_c, s, off_c), jnp.exp2)
        p = e / l * any_c
        imp = imp + p
        outs.append(gates[:, r:r + 1] * _bdot(p, cv))

    sel = _select_mask(imp, qpos, nb, n_keys // BLOCK).astype(jnp.bfloat16)
    kpos = _iota2((1, n_keys), 1)
    allowed = (_bdot(sel, ex_ref[:, 0:n_keys]) > 0.5) & (kpos <= qpos)
    mask_s = jnp.where(allowed, 0.0, NEG)
    rel_s = (kpos - t0).astype(jnp.float32)
    ksel = ksel_ref[0:n_keys, :]
    vsel = vsel_ref[0:n_keys, :]
    for r in range(GROUP):
        s = _bdot(qs[r], ksel, _NT) + slopes[r] * rel_s + mask_s
        e, l = _softmax_rows(s, jnp.exp2)
        outs[r] = outs[r] + gates[:, GROUP + r:GROUP + r + 1] * (_bdot(e, vsel) / l)

    nw = WINDOW + tq
    start = pl.multiple_of(jnp.maximum(t0 - WINDOW, 0), LANES)
    kwin = kwin_ref[pl.ds(start, nw), :]
    vwin = vwin_ref[pl.ds(start, nw), :]
    kpos_w = start + _iota2((1, nw), 1)
    dist = qpos - kpos_w
    mask_w = jnp.where((dist >= 0) & (dist < WINDOW), 0.0, NEG)
    rel_w = (kpos_w - t0).astype(jnp.float32)
    for r in range(GROUP):
        s = _bdot(qs[r], kwin, _NT) + slopes[r] * rel_w + mask_w
        e, l = _softmax_rows(s, jnp.exp2)
        o = outs[r] + gates[:, 2 * GROUP + r:2 * GROUP + r + 1] * (_bdot(e, vwin) / l)
        o_ref[:, r * HEAD_DIM:(r + 1) * HEAD_DIM] = o.astype(o_ref.dtype)


def nsa_slopes():
    h = jnp.arange(1, NSA_HEADS + 1, dtype=jnp.float32)
    s = (2.0 ** (-8.0 * h / NSA_HEADS)).reshape(NSA_KV_HEADS, GROUP, 1)
    return jnp.broadcast_to(s, (NSA_KV_HEADS, GROUP, LANES))


def nsa_gate_cols(w_g):
    d = w_g.shape[0]
    w = w_g.reshape(d, 3, NSA_KV_HEADS, GROUP).transpose(0, 2, 1, 3).reshape(d, NSA_KV_HEADS, 3 * GROUP)
    return jnp.pad(w, ((0, 0), (0, 0), (0, LANES - 3 * GROUP))).reshape(d, NSA_KV_HEADS * LANES)


def nsa_prompt(q, gl, ck, cv, kv16, q_norm, *, batch, seq, tq=NSA_TQ):
    nt = seq // tq
    expand = (jnp.arange(NB_PAD)[:, None] == (jnp.arange(seq)[None, :] // BLOCK)).astype(jnp.bfloat16)
    kvw = 2 * NSA_KV_HEADS
    n_calls = min(NSA_PROMPT_CALLS, nt)
    tiles_per_call = nt // n_calls

    def kvspec(col0):
        return pl.BlockSpec((seq, HEAD_DIM), lambda b, g, i: (b, col0 + g))

    out = None
    for part in range(n_calls):
        tile0 = part * tiles_per_call

        def rows(b, g, i, tile0=tile0):
            return (b * nt + tile0 + i, g)

        operands = [q, gl, ck, cv, kv16, kv16, kv16, kv16, q_norm.reshape(1, HEAD_DIM), nsa_slopes(), expand]
        in_specs = [pl.BlockSpec((tq, GROUP * HEAD_DIM), rows),
                    pl.BlockSpec((tq, LANES), rows),
                    pl.BlockSpec((1, NB_PAD, HEAD_DIM), lambda b, g, i: (b, 0, g)),
                    pl.BlockSpec((1, NB_PAD, HEAD_DIM), lambda b, g, i: (b, 0, g)),
                    kvspec(kvw), kvspec(kvw + NSA_KV_HEADS), kvspec(2 * kvw), kvspec(2 * kvw + NSA_KV_HEADS),
                    pl.BlockSpec((1, HEAD_DIM), lambda b, g, i: (0, 0)),
                    pl.BlockSpec((1, GROUP, LANES), lambda b, g, i: (g, 0, 0)),
                    pl.BlockSpec((NB_PAD, seq), lambda b, g, i: (0, 0))]
        aliases = {}
        if out is not None:
            aliases = {len(operands): 0}
            operands.append(out)
            in_specs.append(pl.BlockSpec(memory_space=pl.ANY))
        out = pl.pallas_call(
            functools.partial(_nsa_prompt_kernel, tq=tq, seq=seq, tile0=tile0, n_keys=(tile0 + tiles_per_call) * tq),
            grid=(batch, NSA_KV_HEADS, tiles_per_call),
            in_specs=in_specs,
            out_specs=pl.BlockSpec((tq, GROUP * HEAD_DIM), rows),
            out_shape=jax.ShapeDtypeStruct((batch * seq, NSA_HEADS * HEAD_DIM), jnp.bfloat16),
            input_output_aliases=aliases,
            compiler_params=_params("parallel", "parallel", "arbitrary"),
            name="nsa_prompt",
        )(*operands)
    return out


KV_SLOTS = 2 * NSA_KV_HEADS
N_PAGES = PAST_LEN // PAGE_SIZE
PAGES_PER_STEP = 8
PAGE_ROWS = PAGE_SIZE * KV_SLOTS
BLOCK_ROWS = BLOCK * KV_SLOTS
BLOCKS_PER_PAGE = PAGE_SIZE // BLOCK
N_PAST_BLOCKS = PAST_LEN // BLOCK
N_SAMPLE_BLOCKS = -(-(PAST_LEN + DEC_SEQ) // BLOCK)
SEL_LANES = -(-N_SAMPLE_BLOCKS // LANES) * LANES
QROWS = GROUP * DEC_SEQ


def _slot_rows(ref, lead, slot, n):
    return ref[lead, pl.ds(slot, n, stride=KV_SLOTS), :]


def _page_specs():
    def spec(j):
        return pl.BlockSpec((1, PAGE_ROWS, HEAD_DIM),
                            lambda b, i, pt: (pt[b * N_PAGES + PAGES_PER_STEP * i + j], 0, 0))
    return [spec(j) for j in range(PAGES_PER_STEP)]


def _paged(cache):
    return [cache.reshape(-1, PAGE_ROWS, HEAD_DIM)] * PAGES_PER_STEP


def _cmp_pool_kernel(pt_ref, *refs):
    pages, (w_ref, o_ref) = refs[:PAGES_PER_STEP], refs[PAGES_PER_STEP:]
    i = pl.program_id(1)
    tiles = []
    for p in pages:
        prod = p[0] * w_ref[...]
        for h in range(BLOCKS_PER_PAGE):
            blk = prod[h * BLOCK_ROWS:(h + 1) * BLOCK_ROWS].reshape(BLOCK, KV_SLOTS, HEAD_DIM)
            tiles.append(jnp.sum(blk, axis=0))
    n = PAGES_PER_STEP * BLOCKS_PER_PAGE * KV_SLOTS
    o_ref[0, pl.ds(pl.multiple_of(i * n, n), n), :] = jnp.concatenate(tiles, axis=0)


def cmp_pool_pages(cache, page_table, cmp_pos_w):
    w = jnp.repeat(jnp.tile(cmp_pos_w, (BLOCKS_PER_PAGE, 1)), NSA_KV_HEADS, axis=1)
    wt = jnp.broadcast_to(w.reshape(PAGE_ROWS, 1), (PAGE_ROWS, HEAD_DIM))
    return pl.pallas_call(
        _cmp_pool_kernel,
        grid_spec=pltpu.PrefetchScalarGridSpec(
            num_scalar_prefetch=1,
            grid=(DEC_BATCH, N_PAGES // PAGES_PER_STEP),
            in_specs=_page_specs() + [pl.BlockSpec((PAGE_ROWS, HEAD_DIM), lambda b, i, pt: (0, 0))],
            out_specs=pl.BlockSpec((1, N_PAST_BLOCKS * KV_SLOTS, HEAD_DIM), lambda b, i, pt: (b, 0, 0))),
        out_shape=jax.ShapeDtypeStruct((DEC_BATCH, N_PAST_BLOCKS * KV_SLOTS, HEAD_DIM), jnp.float32),
        compiler_params=_params("parallel", "arbitrary"),
        name="cmp_pool_pages",
    )(page_table.reshape(-1), *_paged(cache), wt)


def _sample_queries(q_ref, qn_ref, g):
    parts = []
    for r in range(GROUP):
        c0 = (g * GROUP + r) * HEAD_DIM
        x = q_ref[:, c0:c0 + HEAD_DIM]
        x = x * lax.rsqrt(jnp.mean(x * x, axis=-1, keepdims=True) + EPS) * qn_ref[...]
        parts.append(x * SCALE)
    return jnp.concatenate(parts, axis=0).astype(jnp.bfloat16)


def _head_major_col(x, lane0):
    return jnp.concatenate([x[:, lane0 + r:lane0 + r + 1] for r in range(GROUP)], axis=0)


def _slope_col(sl_ref, g):
    return jnp.concatenate([jnp.broadcast_to(sl_ref[g, r:r + 1, 0:1], (DEC_SEQ, 1)) for r in range(GROUP)], axis=0)


def _sample_select_kernel(pooled_ref, wcmp_ref, kn_ref, q_ref, gl_ref, qn_ref, sl_ref, wcache_ref, new_ref, ex_ref,
                          ocw_ref, mask_ref, newmask_ref):
    srow = _iota2((QROWS, 1), 0) % DEC_SEQ
    qpos = PAST_LEN + srow
    gl = gl_ref[...]
    nbp = N_PAST_BLOCKS
    lane_c = _iota2((1, nbp), 1)
    blk_end = (lane_c + 1) * BLOCK - 1
    rel_c = (blk_end - PAST_LEN).astype(jnp.float32)
    nw = wcache_ref.shape[1] // KV_SLOTS
    nwk = nw + LANES
    jw = _iota2((1, nwk), 1)
    kpos_w = jnp.where(jw < nw, PAST_LEN - nw + jw, PAST_LEN + jw - nw)
    dist_w = qpos - kpos_w
    ok_w = (dist_w >= 0) & (dist_w < WINDOW) & (jw < nw + DEC_SEQ)
    mask_w = jnp.where(ok_w, 0.0, NEG)
    rel_w = (kpos_w - PAST_LEN).astype(jnp.float32)
    pad_rows = jnp.zeros((LANES - DEC_SEQ, HEAD_DIM), jnp.float32)
    for g in range(NSA_KV_HEADS):
        ck = _hdot(_slot_rows(pooled_ref, 0, g, nbp), wcmp_ref[0, g])
        ck = ck * lax.rsqrt(jnp.mean(ck * ck, axis=-1, keepdims=True) + EPS) * kn_ref[...]
        cv = _hdot(_slot_rows(pooled_ref, 0, NSA_KV_HEADS + g, nbp), wcmp_ref[1, g])
        qg = _sample_queries(q_ref, qn_ref, g)
        slope = _slope_col(sl_ref, g)
        gates = _sigmoid(gl[:, g * LANES:(g + 1) * LANES])
        s = _bdot(qg, ck, _NT) + slope * rel_c
        e, l = _softmax_rows(jnp.where(qpos >= blk_end, s, NEG))
        p = e / l
        o = _head_major_col(gates, 0) * _bdot(p, cv)
        imp = p[0:DEC_SEQ]
        for r in range(1, GROUP):
            imp = imp + p[r * DEC_SEQ:(r + 1) * DEC_SEQ]
        imp = jnp.concatenate([imp, jnp.zeros((DEC_SEQ, SEL_LANES - nbp), jnp.float32)], axis=1)
        sel = _select_mask(imp, PAST_LEN + _iota2((DEC_SEQ, 1), 0), N_SAMPLE_BLOCKS).astype(jnp.bfloat16)
        for half in range(nbp // LANES):
            keys = _bdot(sel[:, half * LANES:(half + 1) * LANES], ex_ref[...])
            mask_ref[0, g * DEC_SEQ:(g + 1) * DEC_SEQ, half * LANES * BLOCK:(half + 1) * LANES * BLOCK] = keys
        newmask_ref[0, g * DEC_SEQ:(g + 1) * DEC_SEQ, :] = jnp.broadcast_to(
            sel[:, nbp:nbp + 1].astype(jnp.float32), (DEC_SEQ, LANES))
        c0 = 4 * KV_HALF + g * HEAD_DIM
        kw = jnp.concatenate([_slot_rows(wcache_ref, 0, g, nw), new_ref[:, c0:c0 + HEAD_DIM], pad_rows], axis=0)
        vw = jnp.concatenate([_slot_rows(wcache_ref, 0, NSA_KV_HEADS + g, nw),
                              new_ref[:, c0 + KV_HALF:c0 + KV_HALF + HEAD_DIM], pad_rows], axis=0)
        s = _bdot(qg, kw, _NT) + slope * rel_w + mask_w
        e, l = _softmax_rows(s)
        o = o + _head_major_col(gates, 2 * GROUP) * (_bdot(e, vw) / l)
        for r in range(GROUP):
            c0 = (g * GROUP + r) * HEAD_DIM
            ocw_ref[:, c0:c0 + HEAD_DIM] = o[r * DEC_SEQ:(r + 1) * DEC_SEQ]


def nsa_sample_select(pooled, w_cmp, kn_cmp, q, gl, q_norm, cache_win, kv32):
    expand = (jnp.arange(LANES)[:, None] == (jnp.arange(LANES * BLOCK)[None, :] // BLOCK)).astype(jnp.bfloat16)
    row0 = N_PROMPT // DEC_SEQ
    nw = cache_win.shape[1]
    return pl.pallas_call(
        _sample_select_kernel,
        grid=(DEC_BATCH,),
        in_specs=[pl.BlockSpec((1, N_PAST_BLOCKS * KV_SLOTS, HEAD_DIM), lambda b: (b, 0, 0)),
                  pl.BlockSpec((2, NSA_KV_HEADS, HEAD_DIM, HEAD_DIM), lambda b: (0, 0, 0, 0)),
                  pl.BlockSpec((1, HEAD_DIM), lambda b: (0, 0)),
                  pl.BlockSpec((DEC_SEQ, NSA_HEADS * HEAD_DIM), lambda b: (row0 + b, 0)),
                  pl.BlockSpec((DEC_SEQ, NSA_KV_HEADS * LANES), lambda b: (row0 + b, 0)),
                  pl.BlockSpec((1, HEAD_DIM), lambda b: (0, 0)),
                  pl.BlockSpec((NSA_KV_HEADS, GROUP, LANES), lambda b: (0, 0, 0)),
                  pl.BlockSpec((1, nw * KV_SLOTS, HEAD_DIM), lambda b: (b, 0, 0)),
                  pl.BlockSpec((DEC_SEQ, KV_W), lambda b: (row0 + b, 0)),
                  pl.BlockSpec((LANES, LANES * BLOCK), lambda b: (0, 0))],
        out_specs=[pl.BlockSpec((DEC_SEQ, NSA_HEADS * HEAD_DIM), lambda b: (b, 0)),
                   pl.BlockSpec((1, NSA_KV_HEADS * DEC_SEQ, PAST_LEN), lambda b: (b, 0, 0)),
                   pl.BlockSpec((1, NSA_KV_HEADS * DEC_SEQ, LANES), lambda b: (b, 0, 0))],
        out_shape=[jax.ShapeDtypeStruct((N_SAMPLE, NSA_HEADS * HEAD_DIM), jnp.float32),
                   jax.ShapeDtypeStruct((DEC_BATCH, NSA_KV_HEADS * DEC_SEQ, PAST_LEN), jnp.float32),
                   jax.ShapeDtypeStruct((DEC_BATCH, NSA_KV_HEADS * DEC_SEQ, LANES), jnp.float32)],
        compiler_params=_params("parallel"),
        name="nsa_sample_select",
    )(pooled, w_cmp, kn_cmp.reshape(1, HEAD_DIM), q, gl, q_norm.reshape(1, HEAD_DIM), nsa_slopes(),
      cache_win.reshape(DEC_BATCH, nw * KV_SLOTS, HEAD_DIM), kv32, expand)


def _sample_sel_kernel(pt_ref, *refs):
    pages = refs[:PAGES_PER_STEP]
    (mask_ref, newmask_ref, q_ref, gl_ref, qn_ref, sl_ref, new_ref, ocw_ref,
     o_ref, qs, m_sc, l_sc, acc_sc) = refs[PAGES_PER_STEP:]
    i = pl.program_id(1)
    groups = range(NSA_KV_HEADS)

    @pl.when(i == 0)
    def _():
        for g in groups:
            qs[g] = _sample_queries(q_ref, qn_ref, g)
        m_sc[...] = jnp.full(m_sc.shape, NEG, jnp.float32)
        l_sc[...] = jnp.zeros(l_sc.shape, jnp.float32)
        acc_sc[...] = jnp.zeros(acc_sc.shape, jnp.float32)

    def accumulate(keys, vals, rel, keep8):
        keep = [jnp.concatenate([keep8[g]] * GROUP, axis=0) for g in groups]
        s = [_bdot(qs[g], keys[g], _NT) + _slope_col(sl_ref, g) * rel + jnp.where(keep[g] > 0.5, 0.0, NEG)
             for g in groups]
        m_old = [m_sc[g] for g in groups]
        m_new = [jnp.maximum(m_old[g], jnp.max(s[g], axis=-1, keepdims=True)) for g in groups]
        p = [jnp.exp(s[g] - m_new[g]) * keep[g] for g in groups]
        pv = [_bdot(p[g], vals[g]) for g in groups]
        for g in groups:
            alpha = jnp.exp(m_old[g] - m_new[g])
            l_sc[g] = alpha * l_sc[g] + jnp.sum(p[g], axis=-1, keepdims=True)
            acc_sc[g] = alpha * acc_sc[g] + pv[g]
            m_sc[g] = m_new[g]

    lane = _iota2((1, PAGE_SIZE), 1)
    for j, page in enumerate(pages):
        keep_all = mask_ref[0, :, j * PAGE_SIZE:(j + 1) * PAGE_SIZE]

        @pl.when(jnp.max(keep_all) > 0.5)
        def _(j=j, page=page, keep_all=keep_all):
            rel = ((i * PAGES_PER_STEP + j) * PAGE_SIZE - PAST_LEN + lane).astype(jnp.float32)
            accumulate([_slot_rows(page, 0, g, PAGE_SIZE) for g in groups],
                       [_slot_rows(page, 0, NSA_KV_HEADS + g, PAGE_SIZE) for g in groups],
                       rel, [keep_all[g * DEC_SEQ:(g + 1) * DEC_SEQ] for g in groups])

    @pl.when(i == pl.num_programs(1) - 1)
    def _():
        gl = gl_ref[...]
        pad_rows = jnp.zeros((PAGE_SIZE - DEC_SEQ, HEAD_DIM), jnp.float32)
        causal = ((lane <= _iota2((DEC_SEQ, 1), 0)) & (lane < DEC_SEQ)).astype(jnp.float32)
        c0 = 2 * KV_HALF
        accumulate([jnp.concatenate([new_ref[:, c0 + g * HEAD_DIM:c0 + (g + 1) * HEAD_DIM], pad_rows], axis=0)
                    for g in groups],
                   [jnp.concatenate([new_ref[:, c0 + KV_HALF + g * HEAD_DIM:c0 + KV_HALF + (g + 1) * HEAD_DIM], pad_rows],
                                    axis=0) for g in groups],
                   lane.astype(jnp.float32),
                   [newmask_ref[0, g * DEC_SEQ:(g + 1) * DEC_SEQ, :] * causal for g in groups])
        for g in groups:
            gates = _sigmoid(gl[:, g * LANES:(g + 1) * LANES])
            o = _head_major_col(gates, GROUP) * (acc_sc[g] / l_sc[g])
            for r in range(GROUP):
                c0 = (g * GROUP + r) * HEAD_DIM
                o_ref[:, c0:c0 + HEAD_DIM] = ocw_ref[:, c0:c0 + HEAD_DIM] + o[r * DEC_SEQ:(r + 1) * DEC_SEQ]


def nsa_sample_sel(cache, page_table, mask, newmask, q, gl, q_norm, kv32, ocw):
    row0 = N_PROMPT // DEC_SEQ
    nrow = NSA_KV_HEADS * DEC_SEQ
    step_keys = PAGES_PER_STEP * PAGE_SIZE
    return pl.pallas_call(
        _sample_sel_kernel,
        grid_spec=pltpu.PrefetchScalarGridSpec(
            num_scalar_prefetch=1,
            grid=(DEC_BATCH, N_PAGES // PAGES_PER_STEP),
            in_specs=_page_specs() + [
                pl.BlockSpec((1, nrow, step_keys), lambda b, i, pt: (b, 0, i)),
                pl.BlockSpec((1, nrow, LANES), lambda b, i, pt: (b, 0, 0)),
                pl.BlockSpec((DEC_SEQ, NSA_HEADS * HEAD_DIM), lambda b, i, pt: (row0 + b, 0)),
                pl.BlockSpec((DEC_SEQ, NSA_KV_HEADS * LANES), lambda b, i, pt: (row0 + b, 0)),
                pl.BlockSpec((1, HEAD_DIM), lambda b, i, pt: (0, 0)),
                pl.BlockSpec((NSA_KV_HEADS, GROUP, LANES), lambda b, i, pt: (0, 0, 0)),
                pl.BlockSpec((DEC_SEQ, KV_W), lambda b, i, pt: (row0 + b, 0)),
                pl.BlockSpec((DEC_SEQ, NSA_HEADS * HEAD_DIM), lambda b, i, pt: (b, 0))],
            out_specs=pl.BlockSpec((DEC_SEQ, NSA_HEADS * HEAD_DIM), lambda b, i, pt: (b, 0)),
            scratch_shapes=[pltpu.VMEM((NSA_KV_HEADS, QROWS, HEAD_DIM), jnp.bfloat16),
                            pltpu.VMEM((NSA_KV_HEADS, QROWS, 1), jnp.float32),
                            pltpu.VMEM((NSA_KV_HEADS, QROWS, 1), jnp.float32),
                            pltpu.VMEM((NSA_KV_HEADS, QROWS, HEAD_DIM), jnp.float32)]),
        out_shape=jax.ShapeDtypeStruct((N_SAMPLE, NSA_HEADS * HEAD_DIM), jnp.float32),
        compiler_params=_params("parallel", "arbitrary"),
        name="nsa_sample_sel",
    )(page_table.reshape(-1), *_paged(cache), mask, newmask,
      q, gl, q_norm.reshape(1, HEAD_DIM), nsa_slopes(), kv32, ocw)


def nsa_sample(q, gl, kv32, cache_cmp_kv, cache_sel_kv, cache_win_kv, page_table, cmp_pos_w, w_cmp, kn_cmp, q_norm):
    pooled = cmp_pool_pages(cache_cmp_kv, page_table, cmp_pos_w)
    ocw, mask, newmask = nsa_sample_select(pooled, w_cmp, kn_cmp, q, gl, q_norm, cache_win_kv, kv32)
    return nsa_sample_sel(cache_sel_kv, page_table, mask, newmask, q, gl, q_norm, kv32, ocw)


def _split_rows(x):
    return (x[:N_PROMPT].reshape((BATCH, SEQ) + x.shape[1:]),
            x[N_PROMPT:].reshape((DEC_BATCH, DEC_SEQ) + x.shape[1:]))


def kernel(x_prompt, x_sample, state_conv, state_delta, cache_cmp_kv, cache_sel_kv, cache_win_kv, page_table,
           ffn_norm, ffn_w_gate, ffn_w_up, ffn_w_down, mix_norm,
           gdn_w_in, gdn_conv_w, gdn_a_log, gdn_dt_bias, gdn_o_norm, gdn_w_out,
           kv_norm, w_kv, cmp_pos_w, w_cmp, k_norm, nsa_w_q, nsa_q_norm, nsa_w_o):
    bf = jnp.bfloat16
    n_main = GDN_CONV_CH + GDN_V
    n_q = NSA_HEADS * HEAD_DIM
    hist = CONV_W - 1

    def ffn_half(h, hg16, ssq, layer, i, next_gains):
        act, wd16 = swiglu_up(hg16, ssq, ffn_w_gate, ffn_w_up, ffn_w_down, (layer, i))
        return matmul_residual(act, wd16, (), h, 0.5, next_gains, tn=256, tm=DOWN_ROWS)

    h, hg16, ssq = rows_prep(x_prompt.reshape(N_PROMPT, D_MODEL), x_sample.reshape(N_SAMPLE, D_MODEL), ffn_norm[0, 0])
    h, (hg16,), ssq = ffn_half(h, hg16, ssq, 0, 0, [mix_norm[0]])
    proj = matmul_normed(hg16, ssq, jnp.swapaxes(gdn_w_in[0], 0, 1), n_cols=n_main, tn=512, w_is_transposed=True)
    ab = matmul_normed(hg16, ssq, gdn_group_cols(gdn_w_in[0, :, n_main:]), tn=LANES)
    conv_w8 = jnp.pad(gdn_conv_w[0], ((0, CONV_PAD - CONV_W), (0, 0)))
    hp = gdn_head_params(gdn_a_log[0], gdn_dt_bias[0])

    def pad_sample(x):
        x = x[N_PROMPT:].reshape(DEC_BATCH, DEC_SEQ, x.shape[1])
        return jnp.pad(x, ((0, 0), (0, GDN_CHUNK - DEC_SEQ), (0, 0))).reshape(DEC_BATCH * GDN_CHUNK, x.shape[2])

    o_p, delta_p = gdn_mixer(proj, ab, jnp.zeros((BATCH, CONV_PAD, GDN_CONV_CH), jnp.float32),
                             jnp.zeros((BATCH, GDN_HEADS, GDN_DK, GDN_DV), jnp.float32),
                             conv_w8, hp, gdn_o_norm[0], batch=BATCH, n_chunks=SEQ // GDN_CHUNK)
    o_s, delta_s = gdn_mixer(pad_sample(proj), pad_sample(ab),
                             jnp.pad(state_conv[0], ((0, 0), (CONV_PAD - hist, 0), (0, 0))), state_delta[0],
                             conv_w8, hp, gdn_o_norm[0], batch=DEC_BATCH, n_chunks=1, valid_rows=DEC_SEQ)
    o_s = o_s.reshape(DEC_BATCH, GDN_CHUNK, GDN_V)[:, :DEC_SEQ].reshape(N_SAMPLE, GDN_V)
    conv_p = jnp.stack([lax.slice(proj, ((b + 1) * SEQ - hist, 0), ((b + 1) * SEQ, GDN_CONV_CH)) for b in range(BATCH)])
    u_s = lax.slice(proj, (N_PROMPT, 0), (N_ROWS, GDN_CONV_CH)).reshape(DEC_BATCH, DEC_SEQ, GDN_CONV_CH)
    conv_s = jnp.concatenate([state_conv[0], u_s], axis=1)[:, DEC_SEQ:]
    h, (hg16,), ssq = matmul_residual(jnp.concatenate([o_p, o_s], axis=0), gdn_w_out, (0,), h, 1.0, [ffn_norm[0, 1]],
                                      tn=256, tm=MM_ROWS)
    h, (hg16_kv, hg16), ssq = ffn_half(h, hg16, ssq, 0, 1, [kv_norm, ffn_norm[1, 0]])

    kv32, kv16 = kv_finish(matmul_normed(hg16_kv, ssq, w_kv, tn=512), k_norm)
    kv5 = kv32.reshape(N_ROWS, 3, 2, NSA_KV_HEADS, HEAD_DIM)
    cmp_p, cmp_s = _split_rows(kv5[:, 0])
    sel_p, sel_s = _split_rows(kv5[:, 1])
    win_rows_p, win_rows_s = _split_rows(kv5[:, 2])

    h, (hg16,), ssq = ffn_half(h, hg16, ssq, 1, 0, [mix_norm[1]])
    q = matmul_normed(hg16, ssq, jnp.swapaxes(nsa_w_q[0], 0, 1), n_cols=n_q, tn=512, w_is_transposed=True)
    gl = matmul_normed(hg16, ssq, nsa_gate_cols(nsa_w_q[0, :, n_q:]), tn=LANES)
    ck, cv = compress_prompt(kv32, cmp_pos_w, w_cmp, k_norm[0], batch=BATCH, seq=SEQ)
    o_p = nsa_prompt(q, gl, ck, cv, kv16, nsa_q_norm[0], batch=BATCH, seq=SEQ)
    o_s = nsa_sample(q, gl, kv32, cache_cmp_kv, cache_sel_kv, cache_win_kv, page_table, cmp_pos_w, w_cmp, k_norm[0],
                     nsa_q_norm[0])
    h, (hg16,), ssq = matmul_residual(jnp.concatenate([o_p, o_s.astype(bf)], axis=0), nsa_w_o, (0,), h, 1.0,
                                      [ffn_norm[1, 1]], tn=256, tm=MM_ROWS)
    h, _, _ = ffn_half(h, hg16, ssq, 1, 1, [])

    y_p, y_s = _split_rows(h)
    win_p = win_rows_p[:, -min(WINDOW, SEQ):]
    win_s = jnp.concatenate([cache_win_kv, win_rows_s], axis=1)[:, DEC_SEQ:]
    return (y_p, y_s, conv_p[None], conv_s[None], delta_p[None], delta_s[None],
            cmp_p, cmp_s, sel_p, sel_s, win_p, win_s)
```

```python
import functools
import math

import jax
import jax.numpy as jnp
from jax import lax
from jax.experimental import pallas as pl
from jax.experimental.pallas import tpu as pltpu

D_MODEL = 4096
BATCH = 4
SEQ = 2048
DEPTH = 2
DEC_BATCH = 8
DEC_SEQ = 8
PAST_LEN = 16384
PAGE_SIZE = 128
D_FF = 11008
EPS = 1e-6
GDN_HEADS = 16
GDN_DK = 128
GDN_DV = 256
CONV_W = 4
GDN_CHUNK = 64
GDN_QK = GDN_HEADS * GDN_DK
GDN_V = GDN_HEADS * GDN_DV
GDN_CONV_CH = 2 * GDN_QK + GDN_V
NSA_HEADS = 32
NSA_KV_HEADS = 4
HEAD_DIM = 128
GROUP = NSA_HEADS // NSA_KV_HEADS
BLOCK = 64
N_SELECT = 16
N_LOCAL = 2
WINDOW = 512
SCALE = HEAD_DIM ** -0.5
NEG = -1e30
LOG2E = 1.0 / math.log(2.0)
FORCE = 1e4

N_PROMPT = BATCH * SEQ
N_SAMPLE = DEC_BATCH * DEC_SEQ
N_ROWS = N_PROMPT + N_SAMPLE

VMEM_LIMIT_BYTES = 56 * 1024 * 1024
LANES = 128
SUBLANES = 8

NORM_ROWS = 192
MM_ROWS = 1376
DOWN_ROWS = 688


def _params(*sem):
    return pltpu.CompilerParams(dimension_semantics=sem, vmem_limit_bytes=VMEM_LIMIT_BYTES)


PREP_ROWS = 64


def _lane_fold(x):
    acc = x[:, 0:LANES]
    for c in range(1, x.shape[1] // LANES):
        acc = acc + x[:, c * LANES:(c + 1) * LANES]
    return acc


def _rows_prep_kernel(xp_ref, xs_ref, g_ref, h_ref, h16_ref, ssq_ref):
    n_prompt_tiles = N_PROMPT // PREP_ROWS

    def emit(x):
        h_ref[...] = x
        h16_ref[...] = (x * g_ref[...]).astype(h16_ref.dtype)
        ssq_ref[...] = _lane_fold(x * x)

    @pl.when(pl.program_id(0) < n_prompt_tiles)
    def _():
        emit(xp_ref[...])

    @pl.when(pl.program_id(0) >= n_prompt_tiles)
    def _():
        emit(xs_ref[...])


def rows_prep(x_prompt, x_sample, gain):
    d = x_prompt.shape[1]
    n_p = N_PROMPT // PREP_ROWS
    n = N_ROWS // PREP_ROWS
    return pl.pallas_call(
        _rows_prep_kernel,
        grid=(n,),
        in_specs=[pl.BlockSpec((PREP_ROWS, d), lambda i: (jnp.minimum(i, n_p - 1), 0)),
                  pl.BlockSpec((PREP_ROWS, d), lambda i: (jnp.maximum(i - n_p, 0), 0)),
                  pl.BlockSpec((1, d), lambda i: (0, 0))],
        out_specs=[pl.BlockSpec((PREP_ROWS, d), lambda i: (i, 0)),
                   pl.BlockSpec((PREP_ROWS, d), lambda i: (i, 0)),
                   pl.BlockSpec((PREP_ROWS, LANES), lambda i: (i, 0))],
        out_shape=[jax.ShapeDtypeStruct((N_ROWS, d), jnp.float32),
                   jax.ShapeDtypeStruct((N_ROWS, d), jnp.bfloat16),
                   jax.ShapeDtypeStruct((N_ROWS, LANES), jnp.float32)],
        compiler_params=_params("parallel"),
        name="rows_prep",
    )(x_prompt, x_sample, gain.reshape(1, d))


def _row_scale(ssq_ref, k):
    return lax.rsqrt(jnp.sum(ssq_ref[...], axis=-1, keepdims=True) / k + EPS)


def _weight_spec(w, lead, tn):
    k = w.shape[-2]
    return pl.BlockSpec((None,) * len(lead) + (k, tn), lambda i, j: tuple(lead) + (0, j))


def _normed_specs(tm, k):
    return [pl.BlockSpec((tm, k), lambda i, j: (i, 0)),
            pl.BlockSpec((tm, LANES), lambda i, j: (i, 0))]


def _mm_kernel(x_ref, ssq_ref, w_ref, o_ref, *, w_is_transposed):
    w = w_ref[...].astype(jnp.bfloat16)
    if w_is_transposed:
        acc = lax.dot_general(x_ref[...], w, (((1,), (1,)), ((), ())), preferred_element_type=jnp.float32)
    else:
        acc = jnp.dot(x_ref[...], w, preferred_element_type=jnp.float32)
    o_ref[...] = (_row_scale(ssq_ref, x_ref.shape[1]) * acc).astype(o_ref.dtype)


def matmul_normed(xg16, ssq, w, lead=(), *, tn, n_cols=None, tm=MM_ROWS, w_is_transposed=False):
    m, k = xg16.shape
    n = n_cols or (w.shape[0] if w_is_transposed else w.shape[-1])
    w_spec = pl.BlockSpec((tn, k), lambda i, j: (j, 0)) if w_is_transposed else _weight_spec(w, lead, tn)
    return pl.pallas_call(
        functools.partial(_mm_kernel, w_is_transposed=w_is_transposed),
        grid=(m // tm, n // tn),
        in_specs=_normed_specs(tm, k) + [w_spec],
        out_specs=pl.BlockSpec((tm, tn), lambda i, j: (i, j)),
        out_shape=jax.ShapeDtypeStruct((m, n), jnp.float32),
        compiler_params=_params("parallel", "arbitrary"),
        name="matmul_normed",
    )(xg16, ssq, w)


def _mm_res_kernel(x_ref, w_ref, r_ref, *refs, scale, n_gains):
    gain_refs, o_ref, o16_refs = refs[:n_gains], refs[n_gains], refs[n_gains + 1:2 * n_gains + 1]
    w = w_ref[...].astype(jnp.bfloat16)
    h = r_ref[...] + scale * jnp.dot(x_ref[...], w, preferred_element_type=jnp.float32)
    o_ref[...] = h
    for g_ref, o16_ref in zip(gain_refs, o16_refs):
        o16_ref[...] = (h * g_ref[...]).astype(o16_ref.dtype)
    if n_gains:
        ssq_ref = refs[2 * n_gains + 1]
        part = _lane_fold(h * h)

        @pl.when(pl.program_id(1) == 0)
        def _():
            ssq_ref[...] = part

        @pl.when(pl.program_id(1) > 0)
        def _():
            ssq_ref[...] += part


def matmul_residual(x, w, lead, res, scale, gains, *, tn, tm):
    m, k = x.shape
    n = w.shape[-1]
    ng = len(gains)
    tile = pl.BlockSpec((tm, tn), lambda i, j: (i, j))
    out = pl.pallas_call(
        functools.partial(_mm_res_kernel, scale=scale, n_gains=ng),
        grid=(m // tm, n // tn),
        in_specs=[pl.BlockSpec((tm, k), lambda i, j: (i, 0)), _weight_spec(w, lead, tn), tile]
                 + [pl.BlockSpec((1, tn), lambda i, j: (0, j))] * ng,
        out_specs=[tile] * (1 + ng) + [pl.BlockSpec((tm, LANES), lambda i, j: (i, 0))] * min(ng, 1),
        out_shape=[jax.ShapeDtypeStruct((m, n), jnp.float32)] + [jax.ShapeDtypeStruct((m, n), jnp.bfloat16)] * ng
                  + [jax.ShapeDtypeStruct((m, LANES), jnp.float32)] * min(ng, 1),
        compiler_params=_params("parallel", "arbitrary"),
        name="matmul_residual",
    )(x, w, res, *[g.reshape(1, n) for g in gains])
    return out[0], list(out[1:1 + ng]), (out[1 + ng] if ng else None)


WD_SLAB_PASSES = 2


def _swiglu_up_kernel(x_ref, ssq_ref, wg_ref, wu_ref, wd_ref, o_ref, wd16_ref):
    x = x_ref[...]
    r = _row_scale(ssq_ref, x.shape[1])
    g = r * jnp.dot(x, wg_ref[...].astype(jnp.bfloat16), preferred_element_type=jnp.float32)
    u = r * jnp.dot(x, wu_ref[...].astype(jnp.bfloat16), preferred_element_type=jnp.float32)
    o_ref[...] = (g * jax.nn.sigmoid(g) * u).astype(o_ref.dtype)
    wd16_ref[...] = wd_ref[...].astype(wd16_ref.dtype)


def swiglu_up(xg16, ssq, wg, wu, wd, lead, tn=256, tm=MM_ROWS):
    m, k = xg16.shape
    n = wg.shape[-1]
    nj = n // tn
    n_slabs = WD_SLAB_PASSES * nj
    slab = wd.shape[-2] // n_slabs
    d_out = wd.shape[-1]

    def slab_index(i, j):
        return jnp.minimum(i * nj + j, n_slabs - 1)

    return pl.pallas_call(
        _swiglu_up_kernel,
        grid=(m // tm, nj),
        in_specs=_normed_specs(tm, k) + [
            _weight_spec(wg, lead, tn), _weight_spec(wu, lead, tn),
            pl.BlockSpec((None,) * len(lead) + (slab, d_out), lambda i, j: tuple(lead) + (slab_index(i, j), 0))],
        out_specs=[pl.BlockSpec((tm, tn), lambda i, j: (i, j)),
                   pl.BlockSpec((slab, d_out), lambda i, j: (slab_index(i, j), 0))],
        out_shape=[jax.ShapeDtypeStruct((m, n), jnp.bfloat16),
                   jax.ShapeDtypeStruct((wd.shape[-2], d_out), jnp.bfloat16)],
        compiler_params=_params("parallel", "arbitrary"),
        name="swiglu_up",
    )(xg16, ssq, wg, wu, wd)


_HI = lax.Precision.HIGHEST
_NT = (((1,), (1,)), ((), ()))
_TN = (((0,), (0,)), ((), ()))


def _bdot(a, b, dims=None):
    a = a.astype(jnp.bfloat16)
    b = b.astype(jnp.bfloat16)
    if dims is None:
        return jnp.dot(a, b, preferred_element_type=jnp.float32)
    return lax.dot_general(a, b, dims, preferred_element_type=jnp.float32)


def _hdot(a, b, dims=None):
    if dims is None:
        return jnp.dot(a, b, precision=_HI, preferred_element_type=jnp.float32)
    return lax.dot_general(a, b, dims, precision=_HI, preferred_element_type=jnp.float32)


def _iota2(shape, axis):
    return lax.broadcasted_iota(jnp.int32, shape, axis)


def _sigmoid(x):
    return 1.0 / (1.0 + jnp.exp(-x))


def _softmax_rows(s, exp=jnp.exp):
    m = jnp.max(s, axis=-1, keepdims=True)
    e = exp(s - m)
    return e, jnp.sum(e, axis=-1, keepdims=True)


GDN_HB = 16
CONV_PAD = SUBLANES


def _gdn_kernel(uq_ref, uk_ref, uv_ref, gate_ref, ab_ref, cq_ref, ck_ref, cv_ref, s0_ref,
                wq_ref, wk_ref, wv_ref, hp_ref, onorm_ref,
                o_ref, s_ref, extq, extk, extv, *, hb, chunk, valid_rows):
    C = chunk
    c = pl.program_id(2)

    @pl.when(c == 0)
    def _():
        s_ref[...] = s0_ref[...]
        extq[0:CONV_PAD, :] = cq_ref[0]
        extk[0:CONV_PAD, :] = ck_ref[0]
        extv[0:CONV_PAD, :] = cv_ref[0]

    def conv(ext, u_ref, w_ref):
        ext[CONV_PAD:CONV_PAD + C, :] = u_ref[...]
        base = CONV_PAD - (CONV_W - 1)
        acc = ext[base:base + C, :] * w_ref[0:1, :]
        for i in range(1, CONV_W):
            acc = acc + ext[base + i:base + i + C, :] * w_ref[i:i + 1, :]
        ext[0:CONV_PAD, :] = ext[C:C + CONV_PAD, :]
        return acc * _sigmoid(acc)

    qc = conv(extq, uq_ref, wq_ref)
    kc = conv(extk, uk_ref, wk_ref)
    vc = conv(extv, uv_ref, wv_ref)

    row = _iota2((C, C), 0)
    col = _iota2((C, C), 1)
    tri_incl = row >= col
    tri_strict = row > col
    row_ok = None
    if valid_rows < C:
        row_ok = _iota2((C, 1), 0) < valid_rows

    ab = ab_ref[...]
    x = ab + hp_ref[0, 1:2, :]
    softplus = jnp.maximum(x, 0.0) + jnp.log(1.0 + jnp.exp(-jnp.abs(x)))
    gmat = -jnp.exp(hp_ref[0, 0:1, :]) * softplus
    if row_ok is not None:
        gmat = jnp.where(row_ok, gmat, 0.0)
    beta = _sigmoid(ab)
    gam = _hdot(tri_incl.astype(jnp.float32), gmat)
    eye_l = (_iota2((LANES, LANES), 0) == _iota2((LANES, LANES), 1)).astype(jnp.float32)
    gam_t = _hdot(eye_l, gam, _NT)

    heads = range(hb)
    gc = [gam[:, h:h + 1] for h in heads]
    bc = [beta[:, hb + h:hb + h + 1] for h in heads]
    glast = [gam[C - 1:C, h:h + 1] for h in heads]
    q, k, v, decay = [], [], [], []
    for h in heads:
        qh = qc[:, h * GDN_DK:(h + 1) * GDN_DK]
        kh = kc[:, h * GDN_DK:(h + 1) * GDN_DK]
        vh = vc[:, h * GDN_DV:(h + 1) * GDN_DV]
        qh = qh * lax.rsqrt(jnp.sum(qh * qh, axis=-1, keepdims=True) + EPS) * GDN_DK ** -0.5
        kh = kh * lax.rsqrt(jnp.sum(kh * kh, axis=-1, keepdims=True) + EPS)
        if row_ok is not None:
            qh = jnp.where(row_ok, qh, 0.0)
            kh = jnp.where(row_ok, kh, 0.0)
            vh = jnp.where(row_ok, vh, 0.0)
        q.append(qh)
        k.append(kh)
        v.append(vh)
        decay.append(jnp.exp(jnp.where(tri_incl, gc[h] - gam_t[h:h + 1, :], -jnp.inf)))
    kk = [_bdot(k[h], k[h], _NT) for h in heads]
    qk = [_bdot(q[h], k[h], _NT) for h in heads]
    s_old = [s_ref[0, h] for h in heads]
    kq_s = [_bdot(jnp.concatenate([k[h], q[h]], axis=0), s_old[h]) for h in heads]
    pw = [jnp.where(tri_strict, decay[h] * kk[h], 0.0) * bc[h] for h in heads]
    nil = [-pw[h] for h in heads]
    for _ in range(int(math.log2(C)) - 1):
        pw = [_bdot(pw[h], pw[h]) for h in heads]
        nil = [nil[h] + pw[h] + _bdot(nil[h], pw[h]) for h in heads]
    gt = [jnp.exp(gc[h]) for h in heads]
    rhs = [bc[h] * (v[h] - gt[h] * kq_s[h][:C]) for h in heads]
    u = [rhs[h] + _bdot(nil[h], rhs[h]) for h in heads]
    o = [gt[h] * kq_s[h][C:] + _bdot(decay[h] * qk[h], u[h]) for h in heads]
    for h in heads:
        s_ref[0, h] = jnp.exp(glast[h]) * s_old[h] + _bdot(k[h] * jnp.exp(glast[h] - gc[h]), u[h], _TN)
    for h in heads:
        on = o[h] * lax.rsqrt(jnp.mean(o[h] * o[h], axis=-1, keepdims=True) + EPS) * onorm_ref[...]
        gate = gate_ref[:, h * GDN_DV:(h + 1) * GDN_DV]
        o_ref[:, h * GDN_DV:(h + 1) * GDN_DV] = (on * gate * _sigmoid(gate)).astype(o_ref.dtype)


def gdn_mixer(proj, ab, conv_init, s0, conv_w8, hp, o_norm, *, batch, n_chunks, valid_rows=GDN_CHUNK, hb=GDN_HB):
    C = GDN_CHUNK
    ng = GDN_HEADS // hb
    rows = batch * n_chunks * C
    qw, vw = hb * GDN_DK, hb * GDN_DV

    def rowblk(off):
        return lambda b, g, c: (b * n_chunks + c, off + g)

    def fixed3(off):
        return lambda b, g, c: (b, 0, off + g)

    def wblk(off):
        return lambda b, g, c: (0, off + g)

    in_specs = [
        pl.BlockSpec((C, qw), rowblk(0)),
        pl.BlockSpec((C, qw), rowblk(ng)),
        pl.BlockSpec((C, vw), rowblk(ng)),
        pl.BlockSpec((C, vw), rowblk(2 * ng)),
        pl.BlockSpec((C, LANES), rowblk(0)),
        pl.BlockSpec((1, CONV_PAD, qw), fixed3(0)),
        pl.BlockSpec((1, CONV_PAD, qw), fixed3(ng)),
        pl.BlockSpec((1, CONV_PAD, vw), fixed3(ng)),
        pl.BlockSpec((1, hb, GDN_DK, GDN_DV), lambda b, g, c: (b, g, 0, 0)),
        pl.BlockSpec((CONV_PAD, qw), wblk(0)),
        pl.BlockSpec((CONV_PAD, qw), wblk(ng)),
        pl.BlockSpec((CONV_PAD, vw), wblk(ng)),
        pl.BlockSpec((1, CONV_PAD, LANES), lambda b, g, c: (g, 0, 0)),
        pl.BlockSpec((1, GDN_DV), lambda b, g, c: (0, 0)),
    ]
    out_specs = [
        pl.BlockSpec((C, vw), rowblk(0)),
        pl.BlockSpec((1, hb, GDN_DK, GDN_DV), lambda b, g, c: (b, g, 0, 0)),
    ]
    return pl.pallas_call(
        functools.partial(_gdn_kernel, hb=hb, chunk=C, valid_rows=valid_rows),
        grid=(batch, ng, n_chunks),
        in_specs=in_specs,
        out_specs=out_specs,
        out_shape=[jax.ShapeDtypeStruct((rows, GDN_V), jnp.bfloat16),
                   jax.ShapeDtypeStruct((batch, GDN_HEADS, GDN_DK, GDN_DV), jnp.float32)],
        scratch_shapes=[pltpu.VMEM((CONV_PAD + C, qw), jnp.float32),
                        pltpu.VMEM((CONV_PAD + C, qw), jnp.float32),
                        pltpu.VMEM((CONV_PAD + C, vw), jnp.float32)],
        compiler_params=_params("parallel", "parallel", "arbitrary"),
        name="gdn_mixer",
    )(proj, proj, proj, proj, ab, conv_init, conv_init, conv_init, s0,
      conv_w8, conv_w8, conv_w8, hp, o_norm.reshape(1, GDN_DV))


def gdn_group_cols(w_ab, hb=GDN_HB):
    d = w_ab.shape[0]
    ng = GDN_HEADS // hb
    a = w_ab[:, :GDN_HEADS].reshape(d, ng, hb)
    b = w_ab[:, GDN_HEADS:].reshape(d, ng, hb)
    blk = jnp.concatenate([a, b, jnp.zeros((d, ng, LANES - 2 * hb), w_ab.dtype)], axis=-1)
    return blk.reshape(d, ng * LANES)


def gdn_head_params(a_log, dt_bias, hb=GDN_HB):
    ng = GDN_HEADS // hb
    rows = jnp.stack([a_log.reshape(ng, hb), dt_bias.reshape(ng, hb)], axis=1)
    return jnp.pad(rows, ((0, 0), (0, CONV_PAD - 2), (0, LANES - hb)))


KV_W = 3 * 2 * NSA_KV_HEADS * HEAD_DIM
KV_HALF = NSA_KV_HEADS * HEAD_DIM
N_KV_COLS = KV_W // HEAD_DIM
NB_PAD = LANES
_NORMED_KV_COLS = tuple(range(2 * NSA_KV_HEADS, 3 * NSA_KV_HEADS)) + tuple(range(4 * NSA_KV_HEADS, 5 * NSA_KV_HEADS))


def _kv_finish_kernel(kv_ref, gain_ref, o32_ref, o16_ref):
    for j in range(N_KV_COLS):
        sl = slice(j * HEAD_DIM, (j + 1) * HEAD_DIM)
        x = kv_ref[:, sl]
        if j in _NORMED_KV_COLS:
            x = x * lax.rsqrt(jnp.mean(x * x, axis=-1, keepdims=True) + EPS) * gain_ref[0:1, sl]
        o32_ref[:, sl] = x
        o16_ref[:, sl] = x.astype(o16_ref.dtype)


def kv_finish(kv, k_norm):
    n = kv.shape[0]
    ones = jnp.ones((KV_HALF,), jnp.float32)
    gain = jnp.concatenate([ones, ones, jnp.tile(k_norm[1], NSA_KV_HEADS), ones,
                            jnp.tile(k_norm[2], NSA_KV_HEADS), ones])
    gain = jnp.broadcast_to(gain[None], (SUBLANES, KV_W))
    return pl.pallas_call(
        _kv_finish_kernel,
        grid=(n // NORM_ROWS,),
        in_specs=[pl.BlockSpec((NORM_ROWS, KV_W), lambda i: (i, 0)),
                  pl.BlockSpec((SUBLANES, KV_W), lambda i: (0, 0))],
        out_specs=[pl.BlockSpec((NORM_ROWS, KV_W), lambda i: (i, 0)),
                   pl.BlockSpec((NORM_ROWS, KV_W), lambda i: (i, 0))],
        out_shape=[jax.ShapeDtypeStruct((n, KV_W), jnp.float32),
                   jax.ShapeDtypeStruct((n, KV_W), jnp.bfloat16)],
        compiler_params=_params("parallel"),
        name="kv_finish",
    )(kv, gain)


def _compress_prompt_kernel(rows_ref, pw_ref, wcmp_ref, kn_ref, ck_ref, cv_ref, *, nb):
    pooled_k = _hdot(pw_ref[0], rows_ref[:, 0:KV_HALF])
    pooled_v = _hdot(pw_ref[1], rows_ref[:, KV_HALF:2 * KV_HALF])
    ck_ref[...] = jnp.zeros_like(ck_ref)
    cv_ref[...] = jnp.zeros_like(cv_ref)
    for g in range(NSA_KV_HEADS):
        sl = slice(g * HEAD_DIM, (g + 1) * HEAD_DIM)
        k = _hdot(pooled_k[:, sl], wcmp_ref[0, g])
        k = k * lax.rsqrt(jnp.mean(k * k, axis=-1, keepdims=True) + EPS) * kn_ref[...]
        v = _hdot(pooled_v[:, sl], wcmp_ref[1, g])
        ck_ref[0, 0:nb, sl] = k.astype(ck_ref.dtype)
        cv_ref[0, 0:nb, sl] = v.astype(cv_ref.dtype)


def compress_prompt(kv32, cmp_pos_w, w_cmp, kn_cmp, *, batch, seq):
    nb = seq // BLOCK
    pw = jnp.einsum('nm,jc->cnmj', jnp.eye(nb, dtype=jnp.float32), cmp_pos_w).reshape(2, nb, seq)
    shape = jax.ShapeDtypeStruct((batch, NB_PAD, KV_HALF), jnp.bfloat16)
    return pl.pallas_call(
        functools.partial(_compress_prompt_kernel, nb=nb),
        grid=(batch,),
        in_specs=[pl.BlockSpec((seq, 2 * KV_HALF), lambda b: (b, 0)),
                  pl.BlockSpec((2, nb, seq), lambda b: (0, 0, 0)),
                  pl.BlockSpec((2, NSA_KV_HEADS, HEAD_DIM, HEAD_DIM), lambda b: (0, 0, 0, 0)),
                  pl.BlockSpec((1, HEAD_DIM), lambda b: (0, 0))],
        out_specs=[pl.BlockSpec((1, NB_PAD, KV_HALF), lambda b: (b, 0, 0)),
                   pl.BlockSpec((1, NB_PAD, KV_HALF), lambda b: (b, 0, 0))],
        out_shape=[shape, shape],
        compiler_params=_params("parallel"),
        name="compress_prompt",
    )(kv32, pw, w_cmp, kn_cmp.reshape(1, HEAD_DIM))


NSA_TQ = 256
NSA_PROMPT_CALLS = 8


def _select_mask(imp, qpos, nb, n_causal=None):
    lane = _iota2(imp.shape, 1)
    cur = qpos // BLOCK
    causal = lane <= cur
    forced = (lane == 0) | (causal & (lane > cur - N_LOCAL))
    score = jnp.where(forced, FORCE, jnp.where(causal, imp, -1.0))
    score = jnp.where(lane < nb, score, -2.0)
    rank = jnp.zeros(imp.shape, jnp.float32)
    for j in range(nb if n_causal is None else min(nb, n_causal)):
        cj = score[:, j:j + 1]
        rank = rank + jnp.where((cj > score) | ((cj == score) & (lane > j)), 1.0, 0.0)
    return (rank < float(min(N_SELECT, nb))) & (lane < nb)


def _nsa_prompt_kernel(q_ref, gl_ref, ck_ref, cv_ref, ksel_ref, vsel_ref, kwin_ref, vwin_ref,
                       qn_ref, sl_ref, ex_ref, *rest, tq, seq, tile0, n_keys):
    o_ref = rest[-1]
    nb = seq // BLOCK
    t0 = (tile0 + pl.program_id(2)) * tq
    qpos = t0 + _iota2((tq, 1), 0)
    gates = _sigmoid(gl_ref[...])
    qs = []
    for r in range(GROUP):
        x = q_ref[:, r * HEAD_DIM:(r + 1) * HEAD_DIM]
        x = x * lax.rsqrt(jnp.mean(x * x, axis=-1, keepdims=True) + EPS) * qn_ref[...]
        qs.append((x * (SCALE * LOG2E)).astype(jnp.bfloat16))
    slopes = [sl_ref[0, r:r + 1, 0:1] * LOG2E for r in range(GROUP)]

    lane = _iota2((1, NB_PAD), 1)
    blk_end = (lane + 1) * BLOCK - 1
    valid_c = (qpos >= blk_end) & (lane < nb)
    off_c = jnp.where(lane < nb, NEG, -jnp.inf)
    any_c = (qpos >= BLOCK - 1).astype(jnp.float32)
    rel_c = (blk_end - t0).astype(jnp.float32)
    ck = ck_ref[0]
    cv = cv_ref[0]
    imp = jnp.zeros((tq, NB_PAD), jnp.float32)
    outs = []
    for r in range(GROUP):
        s = _bdot(qs[r], ck, _NT) + slopes[r] * rel_c
        e, l = _softmax_rows(jnp.where(valid_c, s, off_c), jnp.exp2)
        p = e / l * any_c
        imp = imp + p
        outs.append(gates[:, r:r + 1] * _bdot(p, cv))

    sel = _select_mask(imp, qpos, nb, n_keys // BLOCK).astype(jnp.bfloat16)
    kpos = _iota2((1, n_keys), 1)
    allowed = (_bdot(sel, ex_ref[:, 0:n_keys]) > 0.5) & (kpos <= qpos)
    mask_s = jnp.where(allowed, 0.0, NEG)
    rel_s = (kpos - t0).astype(jnp.float32)
    ksel = ksel_ref[0:n_keys, :]
    vsel = vsel_ref[0:n_keys, :]
    for r in range(GROUP):
        s = _bdot(qs[r], ksel, _NT) + slopes[r] * rel_s + mask_s
        e, l = _softmax_rows(s, jnp.exp2)
        outs[r] = outs[r] + gates[:, GROUP + r:GROUP + r + 1] * (_bdot(e, vsel) / l)

    nw = min(WINDOW + tq, n_keys)
    start = pl.multiple_of(jnp.maximum(t0 - WINDOW, 0), LANES)
    kwin = kwin_ref[pl.ds(start, nw), :]
    vwin = vwin_ref[pl.ds(start, nw), :]
    kpos_w = start + _iota2((1, nw), 1)
    dist = qpos - kpos_w
    mask_w = jnp.where((dist >= 0) & (dist < WINDOW), 0.0, NEG)
    rel_w = (kpos_w - t0).astype(jnp.float32)
    for r in range(GROUP):
        s = _bdot(qs[r], kwin, _NT) + slopes[r] * rel_w + mask_w
        e, l = _softmax_rows(s, jnp.exp2)
        o = outs[r] + gates[:, 2 * GROUP + r:2 * GROUP + r + 1] * (_bdot(e, vwin) / l)
        o_ref[:, r * HEAD_DIM:(r + 1) * HEAD_DIM] = o.astype(o_ref.dtype)


def nsa_slopes():
    h = jnp.arange(1, NSA_HEADS + 1, dtype=jnp.float32)
    s = (2.0 ** (-8.0 * h / NSA_HEADS)).reshape(NSA_KV_HEADS, GROUP, 1)
    return jnp.broadcast_to(s, (NSA_KV_HEADS, GROUP, LANES))


def nsa_gate_cols(w_g):
    d = w_g.shape[0]
    w = w_g.reshape(d, 3, NSA_KV_HEADS, GROUP).transpose(0, 2, 1, 3).reshape(d, NSA_KV_HEADS, 3 * GROUP)
    return jnp.pad(w, ((0, 0), (0, 0), (0, LANES - 3 * GROUP))).reshape(d, NSA_KV_HEADS * LANES)


def nsa_prompt(q, gl, ck, cv, kv16, q_norm, *, batch, seq, tq=NSA_TQ):
    nt = seq // tq
    expand = (jnp.arange(NB_PAD)[:, None] == (jnp.arange(seq)[None, :] // BLOCK)).astype(jnp.bfloat16)
    kvw = 2 * NSA_KV_HEADS
    n_calls = min(NSA_PROMPT_CALLS, nt)
    tiles_per_call = nt // n_calls

    def kvspec(col0):
        return pl.BlockSpec((seq, HEAD_DIM), lambda b, g, i: (b, col0 + g))

    out = None
    for part in range(n_calls):
        tile0 = part * tiles_per_call

        def rows(b, g, i, tile0=tile0):
            return (b * nt + tile0 + i, g)

        operands = [q, gl, ck, cv, kv16, kv16, kv16, kv16, q_norm.reshape(1, HEAD_DIM), nsa_slopes(), expand]
        in_specs = [pl.BlockSpec((tq, GROUP * HEAD_DIM), rows),
                    pl.BlockSpec((tq, LANES), rows),
                    pl.BlockSpec((1, NB_PAD, HEAD_DIM), lambda b, g, i: (b, 0, g)),
                    pl.BlockSpec((1, NB_PAD, HEAD_DIM), lambda b, g, i: (b, 0, g)),
                    kvspec(kvw), kvspec(kvw + NSA_KV_HEADS), kvspec(2 * kvw), kvspec(2 * kvw + NSA_KV_HEADS),
                    pl.BlockSpec((1, HEAD_DIM), lambda b, g, i: (0, 0)),
                    pl.BlockSpec((1, GROUP, LANES), lambda b, g, i: (g, 0, 0)),
                    pl.BlockSpec((NB_PAD, seq), lambda b, g, i: (0, 0))]
        aliases = {}
        if out is not None:
            aliases = {len(operands): 0}
            operands.append(out)
            in_specs.append(pl.BlockSpec(memory_space=pl.ANY))
        out = pl.pallas_call(
            functools.partial(_nsa_prompt_kernel, tq=tq, seq=seq, tile0=tile0, n_keys=(tile0 + tiles_per_call) * tq),
            grid=(batch, NSA_KV_HEADS, tiles_per_call),
            in_specs=in_specs,
            out_specs=pl.BlockSpec((tq, GROUP * HEAD_DIM), rows),
            out_shape=jax.ShapeDtypeStruct((batch * seq, NSA_HEADS * HEAD_DIM), jnp.bfloat16),
            input_output_aliases=aliases,
            compiler_params=_params("parallel", "parallel", "arbitrary"),
            name="nsa_prompt",
        )(*operands)
    return out


KV_SLOTS = 2 * NSA_KV_HEADS
N_PAGES = PAST_LEN // PAGE_SIZE
PAGES_PER_STEP = 8
PAGE_ROWS = PAGE_SIZE * KV_SLOTS
BLOCK_ROWS = BLOCK * KV_SLOTS
BLOCKS_PER_PAGE = PAGE_SIZE // BLOCK
N_PAST_BLOCKS = PAST_LEN // BLOCK
N_SAMPLE_BLOCKS = -(-(PAST_LEN + DEC_SEQ) // BLOCK)
SEL_LANES = -(-N_SAMPLE_BLOCKS // LANES) * LANES
QROWS = GROUP * DEC_SEQ


def _slot_rows(ref, lead, slot, n):
    return ref[lead, pl.ds(slot, n, stride=KV_SLOTS), :]


def _page_specs():
    def spec(j):
        return pl.BlockSpec((1, PAGE_ROWS, HEAD_DIM),
                            lambda b, i, pt: (pt[b * N_PAGES + PAGES_PER_STEP * i + j], 0, 0))
    return [spec(j) for j in range(PAGES_PER_STEP)]


def _paged(cache):
    return [cache.reshape(-1, PAGE_ROWS, HEAD_DIM)] * PAGES_PER_STEP


def _cmp_pool_kernel(pt_ref, *refs):
    pages, (w_ref, o_ref) = refs[:PAGES_PER_STEP], refs[PAGES_PER_STEP:]
    i = pl.program_id(1)
    tiles = []
    for p in pages:
        prod = p[0] * w_ref[...]
        for h in range(BLOCKS_PER_PAGE):
            blk = prod[h * BLOCK_ROWS:(h + 1) * BLOCK_ROWS].reshape(BLOCK, KV_SLOTS, HEAD_DIM)
            tiles.append(jnp.sum(blk, axis=0))
    n = PAGES_PER_STEP * BLOCKS_PER_PAGE * KV_SLOTS
    o_ref[0, pl.ds(pl.multiple_of(i * n, n), n), :] = jnp.concatenate(tiles, axis=0)


def cmp_pool_pages(cache, page_table, cmp_pos_w):
    w = jnp.repeat(jnp.tile(cmp_pos_w, (BLOCKS_PER_PAGE, 1)), NSA_KV_HEADS, axis=1)
    wt = jnp.broadcast_to(w.reshape(PAGE_ROWS, 1), (PAGE_ROWS, HEAD_DIM))
    return pl.pallas_call(
        _cmp_pool_kernel,
        grid_spec=pltpu.PrefetchScalarGridSpec(
            num_scalar_prefetch=1,
            grid=(DEC_BATCH, N_PAGES // PAGES_PER_STEP),
            in_specs=_page_specs() + [pl.BlockSpec((PAGE_ROWS, HEAD_DIM), lambda b, i, pt: (0, 0))],
            out_specs=pl.BlockSpec((1, N_PAST_BLOCKS * KV_SLOTS, HEAD_DIM), lambda b, i, pt: (b, 0, 0))),
        out_shape=jax.ShapeDtypeStruct((DEC_BATCH, N_PAST_BLOCKS * KV_SLOTS, HEAD_DIM), jnp.float32),
        compiler_params=_params("parallel", "arbitrary"),
        name="cmp_pool_pages",
    )(page_table.reshape(-1), *_paged(cache), wt)


def _sample_queries(q_ref, qn_ref, g):
    parts = []
    for r in range(GROUP):
        c0 = (g * GROUP + r) * HEAD_DIM
        x = q_ref[:, c0:c0 + HEAD_DIM]
        x = x * lax.rsqrt(jnp.mean(x * x, axis=-1, keepdims=True) + EPS) * qn_ref[...]
        parts.append(x * SCALE)
    return jnp.concatenate(parts, axis=0).astype(jnp.bfloat16)


def _head_major_col(x, lane0):
    return jnp.concatenate([x[:, lane0 + r:lane0 + r + 1] for r in range(GROUP)], axis=0)


def _slope_col(sl_ref, g):
    return jnp.concatenate([jnp.broadcast_to(sl_ref[g, r:r + 1, 0:1], (DEC_SEQ, 1)) for r in range(GROUP)], axis=0)


def _sample_select_kernel(pooled_ref, wcmp_ref, kn_ref, q_ref, gl_ref, qn_ref, sl_ref, wcache_ref, new_ref, ex_ref,
                          ocw_ref, mask_ref, newmask_ref):
    srow = _iota2((QROWS, 1), 0) % DEC_SEQ
    qpos = PAST_LEN + srow
    gl = gl_ref[...]
    nbp = N_PAST_BLOCKS
    lane_c = _iota2((1, nbp), 1)
    blk_end = (lane_c + 1) * BLOCK - 1
    rel_c = (blk_end - PAST_LEN).astype(jnp.float32)
    nw = wcache_ref.shape[1] // KV_SLOTS
    nwk = nw + LANES
    jw = _iota2((1, nwk), 1)
    kpos_w = jnp.where(jw < nw, PAST_LEN - nw + jw, PAST_LEN + jw - nw)
    dist_w = qpos - kpos_w
    ok_w = (dist_w >= 0) & (dist_w < WINDOW) & (jw < nw + DEC_SEQ)
    mask_w = jnp.where(ok_w, 0.0, NEG)
    rel_w = (kpos_w - PAST_LEN).astype(jnp.float32)
    pad_rows = jnp.zeros((LANES - DEC_SEQ, HEAD_DIM), jnp.float32)
    for g in range(NSA_KV_HEADS):
        ck = _hdot(_slot_rows(pooled_ref, 0, g, nbp), wcmp_ref[0, g])
        ck = ck * lax.rsqrt(jnp.mean(ck * ck, axis=-1, keepdims=True) + EPS) * kn_ref[...]
        cv = _hdot(_slot_rows(pooled_ref, 0, NSA_KV_HEADS + g, nbp), wcmp_ref[1, g])
        qg = _sample_queries(q_ref, qn_ref, g)
        slope = _slope_col(sl_ref, g)
        gates = _sigmoid(gl[:, g * LANES:(g + 1) * LANES])
        s = _bdot(qg, ck, _NT) + slope * rel_c
        e, l = _softmax_rows(jnp.where(qpos >= blk_end, s, NEG))
        p = e / l
        o = _head_major_col(gates, 0) * _bdot(p, cv)
        imp = p[0:DEC_SEQ]
        for r in range(1, GROUP):
            imp = imp + p[r * DEC_SEQ:(r + 1) * DEC_SEQ]
        imp = jnp.concatenate([imp, jnp.zeros((DEC_SEQ, SEL_LANES - nbp), jnp.float32)], axis=1)
        sel = _select_mask(imp, PAST_LEN + _iota2((DEC_SEQ, 1), 0), N_SAMPLE_BLOCKS).astype(jnp.bfloat16)
        for half in range(nbp // LANES):
            keys = _bdot(sel[:, half * LANES:(half + 1) * LANES], ex_ref[...])
            mask_ref[0, g * DEC_SEQ:(g + 1) * DEC_SEQ, half * LANES * BLOCK:(half + 1) * LANES * BLOCK] = keys
        newmask_ref[0, g * DEC_SEQ:(g + 1) * DEC_SEQ, :] = jnp.broadcast_to(
            sel[:, nbp:nbp + 1].astype(jnp.float32), (DEC_SEQ, LANES))
        c0 = 4 * KV_HALF + g * HEAD_DIM
        kw = jnp.concatenate([_slot_rows(wcache_ref, 0, g, nw), new_ref[:, c0:c0 + HEAD_DIM], pad_rows], axis=0)
        vw = jnp.concatenate([_slot_rows(wcache_ref, 0, NSA_KV_HEADS + g, nw),
                              new_ref[:, c0 + KV_HALF:c0 + KV_HALF + HEAD_DIM], pad_rows], axis=0)
        s = _bdot(qg, kw, _NT) + slope * rel_w + mask_w
        e, l = _softmax_rows(s)
        o = o + _head_major_col(gates, 2 * GROUP) * (_bdot(e, vw) / l)
        for r in range(GROUP):
            c0 = (g * GROUP + r) * HEAD_DIM
            ocw_ref[:, c0:c0 + HEAD_DIM] = o[r * DEC_SEQ:(r + 1) * DEC_SEQ]


def nsa_sample_select(pooled, w_cmp, kn_cmp, q, gl, q_norm, cache_win, kv32):
    expand = (jnp.arange(LANES)[:, None] == (jnp.arange(LANES * BLOCK)[None, :] // BLOCK)).astype(jnp.bfloat16)
    row0 = N_PROMPT // DEC_SEQ
    nw = cache_win.shape[1]
    return pl.pallas_call(
        _sample_select_kernel,
        grid=(DEC_BATCH,),
        in_specs=[pl.BlockSpec((1, N_PAST_BLOCKS * KV_SLOTS, HEAD_DIM), lambda b: (b, 0, 0)),
                  pl.BlockSpec((2, NSA_KV_HEADS, HEAD_DIM, HEAD_DIM), lambda b: (0, 0, 0, 0)),
                  pl.BlockSpec((1, HEAD_DIM), lambda b: (0, 0)),
                  pl.BlockSpec((DEC_SEQ, NSA_HEADS * HEAD_DIM), lambda b: (row0 + b, 0)),
                  pl.BlockSpec((DEC_SEQ, NSA_KV_HEADS * LANES), lambda b: (row0 + b, 0)),
                  pl.BlockSpec((1, HEAD_DIM), lambda b: (0, 0)),
                  pl.BlockSpec((NSA_KV_HEADS, GROUP, LANES), lambda b: (0, 0, 0)),
                  pl.BlockSpec((1, nw * KV_SLOTS, HEAD_DIM), lambda b: (b, 0, 0)),
                  pl.BlockSpec((DEC_SEQ, KV_W), lambda b: (row0 + b, 0)),
                  pl.BlockSpec((LANES, LANES * BLOCK), lambda b: (0, 0))],
        out_specs=[pl.BlockSpec((DEC_SEQ, NSA_HEADS * HEAD_DIM), lambda b: (b, 0)),
                   pl.BlockSpec((1, NSA_KV_HEADS * DEC_SEQ, PAST_LEN), lambda b: (b, 0, 0)),
                   pl.BlockSpec((1, NSA_KV_HEADS * DEC_SEQ, LANES), lambda b: (b, 0, 0))],
        out_shape=[jax.ShapeDtypeStruct((N_SAMPLE, NSA_HEADS * HEAD_DIM), jnp.float32),
                   jax.ShapeDtypeStruct((DEC_BATCH, NSA_KV_HEADS * DEC_SEQ, PAST_LEN), jnp.float32),
                   jax.ShapeDtypeStruct((DEC_BATCH, NSA_KV_HEADS * DEC_SEQ, LANES), jnp.float32)],
        compiler_params=_params("parallel"),
        name="nsa_sample_select",
    )(pooled, w_cmp, kn_cmp.reshape(1, HEAD_DIM), q, gl, q_norm.reshape(1, HEAD_DIM), nsa_slopes(),
      cache_win.reshape(DEC_BATCH, nw * KV_SLOTS, HEAD_DIM), kv32, expand)


def _sample_sel_kernel(pt_ref, *refs):
    pages = refs[:PAGES_PER_STEP]
    (mask_ref, newmask_ref, q_ref, gl_ref, qn_ref, sl_ref, new_ref, ocw_ref,
     o_ref, qs, m_sc, l_sc, acc_sc) = refs[PAGES_PER_STEP:]
    i = pl.program_id(1)
    groups = range(NSA_KV_HEADS)

    @pl.when(i == 0)
    def _():
        for g in groups:
            qs[g] = _sample_queries(q_ref, qn_ref, g)
        m_sc[...] = jnp.full(m_sc.shape, NEG, jnp.float32)
        l_sc[...] = jnp.zeros(l_sc.shape, jnp.float32)
        acc_sc[...] = jnp.zeros(acc_sc.shape, jnp.float32)

    def accumulate(keys, vals, rel, keep8):
        keep = [jnp.concatenate([keep8[g]] * GROUP, axis=0) for g in groups]
        s = [_bdot(qs[g], keys[g], _NT) + _slope_col(sl_ref, g) * rel + jnp.where(keep[g] > 0.5, 0.0, NEG)
             for g in groups]
        m_old = [m_sc[g] for g in groups]
        m_new = [jnp.maximum(m_old[g], jnp.max(s[g], axis=-1, keepdims=True)) for g in groups]
        p = [jnp.exp(s[g] - m_new[g]) * keep[g] for g in groups]
        pv = [_bdot(p[g], vals[g]) for g in groups]
        for g in groups:
            alpha = jnp.exp(m_old[g] - m_new[g])
            l_sc[g] = alpha * l_sc[g] + jnp.sum(p[g], axis=-1, keepdims=True)
            acc_sc[g] = alpha * acc_sc[g] + pv[g]
            m_sc[g] = m_new[g]

    lane = _iota2((1, PAGE_SIZE), 1)
    for j, page in enumerate(pages):
        keep_all = mask_ref[0, :, j * PAGE_SIZE:(j + 1) * PAGE_SIZE]

        @pl.when(jnp.max(keep_all) > 0.5)
        def _(j=j, page=page, keep_all=keep_all):
            rel = ((i * PAGES_PER_STEP + j) * PAGE_SIZE - PAST_LEN + lane).astype(jnp.float32)
            accumulate([_slot_rows(page, 0, g, PAGE_SIZE) for g in groups],
                       [_slot_rows(page, 0, NSA_KV_HEADS + g, PAGE_SIZE) for g in groups],
                       rel, [keep_all[g * DEC_SEQ:(g + 1) * DEC_SEQ] for g in groups])

    @pl.when(i == pl.num_programs(1) - 1)
    def _():
        gl = gl_ref[...]
        pad_rows = jnp.zeros((PAGE_SIZE - DEC_SEQ, HEAD_DIM), jnp.float32)
        causal = ((lane <= _iota2((DEC_SEQ, 1), 0)) & (lane < DEC_SEQ)).astype(jnp.float32)
        c0 = 2 * KV_HALF
        accumulate([jnp.concatenate([new_ref[:, c0 + g * HEAD_DIM:c0 + (g + 1) * HEAD_DIM], pad_rows], axis=0)
                    for g in groups],
                   [jnp.concatenate([new_ref[:, c0 + KV_HALF + g * HEAD_DIM:c0 + KV_HALF + (g + 1) * HEAD_DIM], pad_rows],
                                    axis=0) for g in groups],
                   lane.astype(jnp.float32),
                   [newmask_ref[0, g * DEC_SEQ:(g + 1) * DEC_SEQ, :] * causal for g in groups])
        for g in groups:
            gates = _sigmoid(gl[:, g * LANES:(g + 1) * LANES])
            o = _head_major_col(gates, GROUP) * (acc_sc[g] / l_sc[g])
            for r in range(GROUP):
                c0 = (g * GROUP + r) * HEAD_DIM
                o_ref[:, c0:c0 + HEAD_DIM] = ocw_ref[:, c0:c0 + HEAD_DIM] + o[r * DEC_SEQ:(r + 1) * DEC_SEQ]


def nsa_sample_sel(cache, page_table, mask, newmask, q, gl, q_norm, kv32, ocw):
    row0 = N_PROMPT // DEC_SEQ
    nrow = NSA_KV_HEADS * DEC_SEQ
    step_keys = PAGES_PER_STEP * PAGE_SIZE
    return pl.pallas_call(
        _sample_sel_kernel,
        grid_spec=pltpu.PrefetchScalarGridSpec(
            num_scalar_prefetch=1,
            grid=(DEC_BATCH, N_PAGES // PAGES_PER_STEP),
            in_specs=_page_specs() + [
                pl.BlockSpec((1, nrow, step_keys), lambda b, i, pt: (b, 0, i)),
                pl.BlockSpec((1, nrow, LANES), lambda b, i, pt: (b, 0, 0)),
                pl.BlockSpec((DEC_SEQ, NSA_HEADS * HEAD_DIM), lambda b, i, pt: (row0 + b, 0)),
                pl.BlockSpec((DEC_SEQ, NSA_KV_HEADS * LANES), lambda b, i, pt: (row0 + b, 0)),
                pl.BlockSpec((1, HEAD_DIM), lambda b, i, pt: (0, 0)),
                pl.BlockSpec((NSA_KV_HEADS, GROUP, LANES), lambda b, i, pt: (0, 0, 0)),
                pl.BlockSpec((DEC_SEQ, KV_W), lambda b, i, pt: (row0 + b, 0)),
                pl.BlockSpec((DEC_SEQ, NSA_HEADS * HEAD_DIM), lambda b, i, pt: (b, 0))],
            out_specs=pl.BlockSpec((DEC_SEQ, NSA_HEADS * HEAD_DIM), lambda b, i, pt: (b, 0)),
            scratch_shapes=[pltpu.VMEM((NSA_KV_HEADS, QROWS, HEAD_DIM), jnp.bfloat16),
                            pltpu.VMEM((NSA_KV_HEADS, QROWS, 1), jnp.float32),
                            pltpu.VMEM((NSA_KV_HEADS, QROWS, 1), jnp.float32),
                            pltpu.VMEM((NSA_KV_HEADS, QROWS, HEAD_DIM), jnp.float32)]),
        out_shape=jax.ShapeDtypeStruct((N_SAMPLE, NSA_HEADS * HEAD_DIM), jnp.float32),
        compiler_params=_params("parallel", "arbitrary"),
        name="nsa_sample_sel",
    )(page_table.reshape(-1), *_paged(cache), mask, newmask,
      q, gl, q_norm.reshape(1, HEAD_DIM), nsa_slopes(), kv32, ocw)


def nsa_sample(q, gl, kv32, cache_cmp_kv, cache_sel_kv, cache_win_kv, page_table, cmp_pos_w, w_cmp, kn_cmp, q_norm):
    pooled = cmp_pool_pages(cache_cmp_kv, page_table, cmp_pos_w)
    ocw, mask, newmask = nsa_sample_select(pooled, w_cmp, kn_cmp, q, gl, q_norm, cache_win_kv, kv32)
    return nsa_sample_sel(cache_sel_kv, page_table, mask, newmask, q, gl, q_norm, kv32, ocw)


def _split_rows(x):
    return (x[:N_PROMPT].reshape((BATCH, SEQ) + x.shape[1:]),
            x[N_PROMPT:].reshape((DEC_BATCH, DEC_SEQ) + x.shape[1:]))


def kernel(x_prompt, x_sample, state_conv, state_delta, cache_cmp_kv, cache_sel_kv, cache_win_kv, page_table,
           ffn_norm, ffn_w_gate, ffn_w_up, ffn_w_down, mix_norm,
           gdn_w_in, gdn_conv_w, gdn_a_log, gdn_dt_bias, gdn_o_norm, gdn_w_out,
           kv_norm, w_kv, cmp_pos_w, w_cmp, k_norm, nsa_w_q, nsa_q_norm, nsa_w_o):
    bf = jnp.bfloat16
    n_main = GDN_CONV_CH + GDN_V
    n_q = NSA_HEADS * HEAD_DIM
    hist = CONV_W - 1

    def ffn_half(h, hg16, ssq, layer, i, next_gains):
        act, wd16 = swiglu_up(hg16, ssq, ffn_w_gate, ffn_w_up, ffn_w_down, (layer, i))
        return matmul_residual(act, wd16, (), h, 0.5, next_gains, tn=256, tm=DOWN_ROWS)

    h, hg16, ssq = rows_prep(x_prompt.reshape(N_PROMPT, D_MODEL), x_sample.reshape(N_SAMPLE, D_MODEL), ffn_norm[0, 0])
    h, (hg16,), ssq = ffn_half(h, hg16, ssq, 0, 0, [mix_norm[0]])
    proj = matmul_normed(hg16, ssq, jnp.swapaxes(gdn_w_in[0], 0, 1), n_cols=n_main, tn=512, w_is_transposed=True)
    ab = matmul_normed(hg16, ssq, gdn_group_cols(gdn_w_in[0, :, n_main:]), tn=LANES)
    conv_w8 = jnp.pad(gdn_conv_w[0], ((0, CONV_PAD - CONV_W), (0, 0)))
    hp = gdn_head_params(gdn_a_log[0], gdn_dt_bias[0])

    def pad_sample(x):
        x = x[N_PROMPT:].reshape(DEC_BATCH, DEC_SEQ, x.shape[1])
        return jnp.pad(x, ((0, 0), (0, GDN_CHUNK - DEC_SEQ), (0, 0))).reshape(DEC_BATCH * GDN_CHUNK, x.shape[2])

    o_p, delta_p = gdn_mixer(proj, ab, jnp.zeros((BATCH, CONV_PAD, GDN_CONV_CH), jnp.float32),
                             jnp.zeros((BATCH, GDN_HEADS, GDN_DK, GDN_DV), jnp.float32),
                             conv_w8, hp, gdn_o_norm[0], batch=BATCH, n_chunks=SEQ // GDN_CHUNK)
    o_s, delta_s = gdn_mixer(pad_sample(proj), pad_sample(ab),
                             jnp.pad(state_conv[0], ((0, 0), (CONV_PAD - hist, 0), (0, 0))), state_delta[0],
                             conv_w8, hp, gdn_o_norm[0], batch=DEC_BATCH, n_chunks=1, valid_rows=DEC_SEQ)
    o_s = o_s.reshape(DEC_BATCH, GDN_CHUNK, GDN_V)[:, :DEC_SEQ].reshape(N_SAMPLE, GDN_V)
    conv_p = jnp.stack([lax.slice(proj, ((b + 1) * SEQ - hist, 0), ((b + 1) * SEQ, GDN_CONV_CH)) for b in range(BATCH)])
    u_s = lax.slice(proj, (N_PROMPT, 0), (N_ROWS, GDN_CONV_CH)).reshape(DEC_BATCH, DEC_SEQ, GDN_CONV_CH)
    conv_s = jnp.concatenate([state_conv[0], u_s], axis=1)[:, DEC_SEQ:]
    h, (hg16,), ssq = matmul_residual(jnp.concatenate([o_p, o_s], axis=0), gdn_w_out, (0,), h, 1.0, [ffn_norm[0, 1]],
                                      tn=256, tm=MM_ROWS)
    h, (hg16_kv, hg16), ssq = ffn_half(h, hg16, ssq, 0, 1, [kv_norm, ffn_norm[1, 0]])

    kv32, kv16 = kv_finish(matmul_normed(hg16_kv, ssq, w_kv, tn=512), k_norm)
    kv5 = kv32.reshape(N_ROWS, 3, 2, NSA_KV_HEADS, HEAD_DIM)
    cmp_p, cmp_s = _split_rows(kv5[:, 0])
    sel_p, sel_s = _split_rows(kv5[:, 1])
    win_rows_p, win_rows_s = _split_rows(kv5[:, 2])

    h, (hg16,), ssq = ffn_half(h, hg16, ssq, 1, 0, [mix_norm[1]])
    q = matmul_normed(hg16, ssq, jnp.swapaxes(nsa_w_q[0], 0, 1), n_cols=n_q, tn=512, w_is_transposed=True)
    gl = matmul_normed(hg16, ssq, nsa_gate_cols(nsa_w_q[0, :, n_q:]), tn=LANES)
    ck, cv = compress_prompt(kv32, cmp_pos_w, w_cmp, k_norm[0], batch=BATCH, seq=SEQ)
    o_p = nsa_prompt(q, gl, ck, cv, kv16, nsa_q_norm[0], batch=BATCH, seq=SEQ)
    o_s = nsa_sample(q, gl, kv32, cache_cmp_kv, cache_sel_kv, cache_win_kv, page_table, cmp_pos_w, w_cmp, k_norm[0],
                     nsa_q_norm[0])
    h, (hg16,), ssq = matmul_residual(jnp.concatenate([o_p, o_s.astype(bf)], axis=0), nsa_w_o, (0,), h, 1.0,
                                      [ffn_norm[1, 1]], tn=256, tm=MM_ROWS)
    h, _, _ = ffn_half(h, hg16, ssq, 1, 1, [])

    y_p, y_s = _split_rows(h)
    win_p = win_rows_p[:, -min(WINDOW, SEQ):]
    win_s = jnp.concatenate([cache_win_kv, win_rows_s], axis=1)[:, DEC_SEQ:]
    return (y_p, y_s, conv_p[None], conv_s[None], delta_p[None], delta_s[None],
            cmp_p, cmp_s, sel_p, sel_s, win_p, win_s)
```
